```python
import jax, jax.numpy as jnp
from jax import lax
import numpy as np

D_MODEL = 1024
BATCH = 16
SEQ = 256
DEPTH = 2
DEC_BATCH = 4
DEC_SEQ = 1024
PAST_LEN = 256

GRID_W = 64
FNET_GROUPS = 4
FNET_GDIM = 64
FNET_WIDTH = FNET_GROUPS * FNET_GDIM
RET_HEADS = 6
RET_DK = 64
RET_DV = 64
HG_HEADS = 6
HG_DK = 64
HG_DV = 64
RET_WIDTH = RET_HEADS * RET_DV
HG_WIDTH = HG_HEADS * HG_DV
MIX_WIDTH = FNET_WIDTH + RET_WIDTH + HG_WIDTH
IN_SIZES = (FNET_WIDTH,
            RET_HEADS * RET_DK, RET_HEADS * RET_DK, RET_WIDTH, RET_WIDTH,
            HG_HEADS * HG_DK, HG_HEADS * HG_DK, HG_HEADS * HG_DK, HG_WIDTH, HG_WIDTH)
IN_WIDTH = sum(IN_SIZES)
D_FF = -(-(8 * D_MODEL) // (3 * 256)) * 256
CHUNK = 64
ROPE_BASE = 10000.0
LN_EPS = 1e-5
N_MOD = 6
ALPHA = (2 * DEPTH) ** 0.25
BETA = (8 * DEPTH) ** -0.25

kernel_name = "hybrid_fnet_retnet_hgrn2_diffusion_step"


def _layernorm(x, g=None, b=None):
    xf = x.astype(jnp.float32)
    mu = xf.mean(-1, keepdims=True)
    var = jnp.square(xf - mu).mean(-1, keepdims=True)
    y = (xf - mu) * lax.rsqrt(var + LN_EPS)
    if g is not None:
        y = y * g.astype(jnp.float32) + b.astype(jnp.float32)
    return y.astype(x.dtype)


def _headnorm(x):
    mu = x.mean(-1, keepdims=True)
    var = jnp.square(x - mu).mean(-1, keepdims=True)
    return (x - mu) * lax.rsqrt(var + LN_EPS)


def _rmsnorm_heads(x):
    return x * lax.rsqrt(jnp.square(x).mean(-1, keepdims=True) + LN_EPS)


def _modulation(cvec, w_mod_l, b_mod_l):
    m = jax.nn.silu(cvec) @ w_mod_l + b_mod_l
    return jnp.split(m[..., None, :], N_MOD, axis=-1)


def _axial_rope(x, rows, cols):
    half = x.shape[-1] // 2
    inv = ROPE_BASE ** (-jnp.arange(0, half, 2, dtype=jnp.float32) / half)

    def rot(t, pos):
        ang = pos.astype(jnp.float32)[:, None] * inv
        cos, sin = jnp.cos(ang), jnp.sin(ang)
        t1, t2 = t[..., : half // 2], t[..., half // 2:]
        return jnp.concatenate([t1 * cos - t2 * sin, t1 * sin + t2 * cos], axis=-1)

    return jnp.concatenate([rot(x[..., :half], rows), rot(x[..., half:], cols)], axis=-1)


def _to_chunks(t):
    B, H, L, d = t.shape
    return t.reshape(B, H, L // CHUNK, CHUNK, d).transpose(2, 0, 1, 3, 4)


def _from_chunks(t):
    n, B, H, C, d = t.shape
    return t.transpose(1, 2, 0, 3, 4).reshape(B, H, n * C, d)


def _retention_chunked(q, k, v, log_gamma, s0):
    idx = jnp.arange(CHUNK, dtype=jnp.float32)
    lg = log_gamma[:, None]
    rel = idx[:, None] - idx[None, :]
    decay = jnp.exp(jnp.maximum(rel, 0.0)[None] * lg[:, :, None]) * (rel >= 0)[None]
    xi = jnp.exp((idx + 1.0) * lg)
    zeta = jnp.exp((CHUNK - 1.0 - idx) * lg)
    g_chunk = jnp.exp(CHUNK * log_gamma)

    def body(S, inp):
        qc, kc, vc = inp
        scores = jnp.einsum('bhid,bhjd->bhij', qc, kc) * decay
        o = (jnp.einsum('bhij,bhjv->bhiv', scores, vc)
             + jnp.einsum('bhid,bhdv->bhiv', qc * xi[:, :, None], S))
        S = g_chunk[:, None, None] * S + jnp.einsum('bhjd,bhjv->bhdv', kc * zeta[:, :, None], vc)
        return S, o

    s_fin, o = lax.scan(body, s0, (_to_chunks(q), _to_chunks(k), _to_chunks(v)))
    return _from_chunks(o), s_fin


def _hgrn2_chunked(q, log_f, k, v, s0):
    tri = jnp.tril(jnp.ones((CHUNK, CHUNK), dtype=bool))
    b_all = jnp.cumsum(_to_chunks(log_f), axis=3)

    def body(S, inp):
        qc, bc, kc, vc = inp
        inter = jnp.einsum('bhid,bhdv->bhiv', qc * jnp.exp(bc), S)
        rel = bc[:, :, :, None, :] - bc[:, :, None, :, :]
        dec = jnp.exp(jnp.where(tri[:, :, None], rel, -jnp.inf))
        A = jnp.sum(qc[:, :, :, None, :] * kc[:, :, None, :, :] * dec, axis=-1)
        o = inter + jnp.einsum('bhij,bhjv->bhiv', A, vc)
        b_last = bc[:, :, -1:, :]
        S = (jnp.exp(b_last[:, :, 0, :])[..., None] * S
             + jnp.einsum('bhjd,bhjv->bhdv', kc * jnp.exp(b_last - bc), vc))
        return S, o

    s_fin, o = lax.scan(body, s0, (_to_chunks(q), b_all, _to_chunks(k), _to_chunks(v)))
    return _from_chunks(o), s_fin


def _hgrn_gate(f_logit, lb):
    log_f = jnp.logaddexp(jnp.log(lb), jnp.log1p(-lb) + jax.nn.log_sigmoid(f_logit))
    k = (1.0 - lb) * jax.nn.sigmoid(-f_logit)
    return log_f, k


def _token_mixers(h, w_in_l, w_out_l, log_decay_l, lb_l, s_ret0, s_hg0, positions):
    B, L, _ = h.shape
    f32 = jnp.float32
    points = [sum(IN_SIZES[: i + 1]) for i in range(len(IN_SIZES) - 1)]
    proj = h @ w_in_l
    u_f, rq, rk, rv, rg, hq, hff, hfb, hi, hg = jnp.split(proj, points, axis=-1)

    def heads(t, d):
        return t.reshape(B, L, -1, d).transpose(0, 2, 1, 3).astype(f32)

    def merge(t):
        return t.transpose(0, 2, 1, 3).reshape(B, L, -1)

    def flip(t):
        return jnp.flip(t, axis=2)

    o_f = jnp.fft.fft2(u_f.reshape(B, L, FNET_GROUPS, FNET_GDIM).astype(f32),
                       axes=(1, 3), norm='ortho').real.reshape(B, L, FNET_WIDTH)

    q = heads(rq, RET_DK)
    k = heads(rk, RET_DK) * (RET_DK ** -0.5)
    v = heads(rv, RET_DV)
    if positions is not None:
        q = _axial_rope(q, positions[0], positions[1])
        k = _axial_rope(k, positions[0], positions[1])
    lg = -jnp.exp(log_decay_l.astype(f32))
    s_r0 = s_ret0.astype(f32)
    o_fw, sr_fw = _retention_chunked(q, k, v, lg[0], s_r0[:, 0])
    o_bw, sr_bw = _retention_chunked(flip(q), flip(k), flip(v), lg[1], s_r0[:, 1])
    o_r = merge(_headnorm(o_fw + flip(o_bw)) * jax.nn.silu(heads(rg, RET_DV)))

    hq_ = jax.nn.silu(heads(hq, HG_DK))
    hv = heads(hi, HG_DV)
    lb = lb_l.reshape(2, HG_HEADS, 1, HG_DK)
    logf_fw, k_fw = _hgrn_gate(heads(hff, HG_DK), lb[0])
    logf_bw, k_bw = _hgrn_gate(heads(hfb, HG_DK), lb[1])
    s_h0 = s_hg0.astype(f32)
    oh_fw, sh_fw = _hgrn2_chunked(hq_, logf_fw, k_fw, hv, s_h0[:, 0])
    oh_bw, sh_bw = _hgrn2_chunked(flip(hq_), flip(logf_bw), flip(k_bw), flip(hv), s_h0[:, 1])
    o_h = merge(_rmsnorm_heads(oh_fw + flip(oh_bw)) * jax.nn.silu(heads(hg, HG_DV)))

    mix = jnp.concatenate([o_f, o_r, o_h], axis=-1).astype(h.dtype) @ w_out_l
    return mix, jnp.stack([sr_fw, sr_bw], axis=1), jnp.stack([sh_fw, sh_bw], axis=1)


def _layer(x, cvec, s_ret0, s_hg0, positions, w_mod_l, b_mod_l, w_in_l, w_out_l,
           log_decay_l, lb_l, ln_g_l, ln_b_l, w_gate_l, w_up_l, w_down_l):
    sh1, sc1, g1, sh2, sc2, g2 = _modulation(cvec, w_mod_l, b_mod_l)
    h = _layernorm(x) * (1.0 + sc1) + sh1
    mix, s_ret, s_hg = _token_mixers(h, w_in_l, w_out_l, log_decay_l, lb_l, s_ret0, s_hg0, positions)
    x = _layernorm(ALPHA * x + g1 * mix, ln_g_l[0], ln_b_l[0])
    h = _layernorm(x) * (1.0 + sc2) + sh2
    ffn = (jax.nn.silu(h @ w_gate_l) * (h @ w_up_l)) @ w_down_l
    x = _layernorm(ALPHA * x + g2 * ffn, ln_g_l[1], ln_b_l[1])
    return x, s_ret, s_hg


def setup_inputs(seed: int = 0) -> dict:
    key = jax.random.key(seed)
    ks = jax.random.split(key, 17)
    f32 = jnp.float32

    def nrm(k, shape, s):
        return s * jax.random.normal(k, shape, f32)

    base_decay = np.log(-np.log(1.0 - 2.0 ** (-5.0 - np.arange(RET_HEADS))))
    return {
        "x_prompt": nrm(ks[0], (BATCH, SEQ, D_MODEL), 1.0),
        "x_sample": nrm(ks[1], (DEC_BATCH, DEC_SEQ, D_MODEL), 1.0),
        "c": nrm(ks[2], (DEC_BATCH, D_MODEL), 1.0),
        "state_ret": nrm(ks[3], (DEC_BATCH, DEPTH, 2, RET_HEADS, RET_DK, RET_DV), 0.5),
        "state_hgrn": nrm(ks[4], (DEC_BATCH, DEPTH, 2, HG_HEADS, HG_DK, HG_DV), 0.5),
        "c_ctx": nrm(ks[5], (D_MODEL,), 1.0),
        "w_mod": nrm(ks[6], (DEPTH, D_MODEL, N_MOD * D_MODEL), 0.5 * D_MODEL ** -0.5),
        "b_mod": nrm(ks[7], (DEPTH, N_MOD * D_MODEL), 0.02),
        "w_in": nrm(ks[8], (DEPTH, D_MODEL, IN_WIDTH), D_MODEL ** -0.5),
        "w_out": nrm(ks[9], (DEPTH, MIX_WIDTH, D_MODEL), BETA * MIX_WIDTH ** -0.5),
        "ret_log_decay": jnp.asarray(base_decay, f32) + nrm(ks[10], (DEPTH, 2, RET_HEADS), 0.05),
        "hg_lower_bound": nrm(ks[11], (2, DEPTH, HG_HEADS * HG_DK), 1.0),
        "ln_g": 1.0 + nrm(ks[12], (DEPTH, 2, D_MODEL), 0.02),
        "ln_b": nrm(ks[13], (DEPTH, 2, D_MODEL), 0.02),
        "w_gate": nrm(ks[14], (DEPTH, D_MODEL, D_FF), D_MODEL ** -0.5),
        "w_up": nrm(ks[15], (DEPTH, D_MODEL, D_FF), D_MODEL ** -0.5),
        "w_down": nrm(ks[16], (DEPTH, D_FF, D_MODEL), BETA * D_FF ** -0.5),
    }


def reference(x_prompt, x_sample, c, state_ret, state_hgrn, c_ctx, w_mod, b_mod, w_in, w_out,
              ret_log_decay, hg_lower_bound, ln_g, ln_b, w_gate, w_up, w_down):
    f32 = jnp.float32
    p = jax.nn.softmax(hg_lower_bound.astype(f32), axis=1)
    cum = jnp.cumsum(p, axis=1)
    lbs = cum - cum[:, :1]

    B = x_prompt.shape[0]
    zr = jnp.zeros((B, 2, RET_HEADS, RET_DK, RET_DV), f32)
    zh = jnp.zeros((B, 2, HG_HEADS, HG_DK, HG_DV), f32)
    y = x_prompt
    ret_states, hg_states = [], []
    for l in range(DEPTH):
        y, s_r, s_h = _layer(y, c_ctx, zr, zh, None, w_mod[l], b_mod[l], w_in[l], w_out[l],
                             ret_log_decay[l], lbs[:, l], ln_g[l], ln_b[l],
                             w_gate[l], w_up[l], w_down[l])
        ret_states.append(s_r)
        hg_states.append(s_h)
    new_state_ret = jnp.stack(ret_states, axis=1)
    new_state_hgrn = jnp.stack(hg_states, axis=1)

    L = x_sample.shape[1]
    rows = L // GRID_W
    rr, cc = jnp.meshgrid(jnp.arange(rows, dtype=jnp.int32), jnp.arange(GRID_W, dtype=jnp.int32),
                          indexing='ij')
    positions = (rr.reshape(-1), cc.reshape(-1))
    z = x_sample
    for l in range(DEPTH):
        z, _, _ = _layer(z, c, state_ret[:, l], state_hgrn[:, l], positions, w_mod[l], b_mod[l],
                         w_in[l], w_out[l], ret_log_decay[l], lbs[:, l], ln_g[l], ln_b[l],
                         w_gate[l], w_up[l], w_down[l])
    return (y, z, new_state_ret, new_state_hgrn)
```

```python
import functools

import numpy as np
import jax
import jax.numpy as jnp
from jax import lax
from jax.experimental import pallas as pl
from jax.experimental.pallas import tpu as pltpu

F32 = jnp.float32
BF16 = jnp.bfloat16

D_MODEL = 1024
DEPTH = 2
GRID_W = 64
FNET_WIDTH = 256
FNET_GDIM = 64
N_HEADS = 6
N_PAIRS = N_HEADS // 2
HEAD_DIM = 64
LANES = 128
SUBLANES = 8
HEADS_WIDTH = N_HEADS * HEAD_DIM
IN_WIDTH = 3712
D_FF = 2816
ROPE_BASE = 10000.0
LN_EPS = 1e-5
N_MOD = 6
ALPHA = (2 * DEPTH) ** 0.25
MOD_ROWS = 8
ROW_TILE = 256
HG_CHUNK = 128
VMEM_LIMIT = 56 * 1024 * 1024

COL_RQ, COL_RK, COL_RV, COL_RG = 2, 5, 8, 11
COL_HQ, COL_HFF, COL_HFB, COL_HI, COL_HG = 14, 17, 20, 23, 26


def _silu(x):
    return x / (1.0 + jnp.exp(-x))


def _ln(x):
    mu = jnp.mean(x, axis=-1, keepdims=True)
    xc = x - mu
    var = jnp.mean(xc * xc, axis=-1, keepdims=True)
    return xc * lax.rsqrt(var + LN_EPS)


def _dot(a, b):
    return jnp.dot(a, b, preferred_element_type=F32)


def _dot_nt(a, b):
    return lax.dot_general(a, b, (((1,), (1,)), ((), ())), preferred_element_type=F32)


def _split2(x):
    hi = x.astype(BF16)
    lo = (x - hi.astype(F32)).astype(BF16)
    return hi, lo


def _split3(x):
    x1 = x.astype(BF16)
    r1 = x - x1.astype(F32)
    x2 = r1.astype(BF16)
    x3 = (r1 - x2.astype(F32)).astype(BF16)
    return x1, x2, x3


def _head_sums(x, bd_ones):
    hi, lo = _split2(x)
    return _dot(hi, bd_ones) + _dot(lo, bd_ones)


def _bd_ones():
    r = lax.broadcasted_iota(jnp.int32, (LANES, LANES), 0)
    c = lax.broadcasted_iota(jnp.int32, (LANES, LANES), 1)
    return jnp.where((r < HEAD_DIM) == (c < HEAD_DIM), 1.0, 0.0).astype(BF16)


def _mod_kernel(cv_ref, w_ref, b_ref, o_ref):
    s = _silu(cv_ref[...]).astype(BF16)
    o_ref[...] = _dot(s, w_ref[...].astype(BF16)) + b_ref[...]


def _modulation(cv, w_mod, b_mod):
    tn = 512
    width = N_MOD * D_MODEL
    return pl.pallas_call(
        _mod_kernel,
        grid=(DEPTH, width // tn),
        in_specs=[pl.BlockSpec((MOD_ROWS, D_MODEL), lambda l, j: (0, 0)),
                  pl.BlockSpec((None, D_MODEL, tn), lambda l, j: (l, 0, j)),
                  pl.BlockSpec((None, 1, tn), lambda l, j: (l, 0, j))],
        out_specs=pl.BlockSpec((None, MOD_ROWS, tn), lambda l, j: (l, 0, j)),
        out_shape=jax.ShapeDtypeStruct((DEPTH, MOD_ROWS, width), F32),
        compiler_params=pltpu.CompilerParams(
            dimension_semantics=("arbitrary", "arbitrary"), vmem_limit_bytes=VMEM_LIMIT),
        name="modulation",
    )(cv, w_mod, b_mod.reshape(DEPTH, 1, width))


def _inproj_kernel(x_ref, mod_ref, w_ref, o_ref):
    sh = mod_ref[:, 0:D_MODEL]
    sc = mod_ref[:, D_MODEL:2 * D_MODEL]
    h = _ln(x_ref[...]) * (1.0 + sc) + sh
    o_ref[...] = _dot(h.astype(BF16), w_ref[...])


def _inproj(x2d, mod, w_in_bf, layer, mod_row):
    rows = x2d.shape[0]
    return pl.pallas_call(
        _inproj_kernel,
        grid=(rows // ROW_TILE,),
        in_specs=[pl.BlockSpec((ROW_TILE, D_MODEL), lambda i: (i, 0)),
                  pl.BlockSpec((None, None, 1, N_MOD * D_MODEL), lambda i: (layer, mod_row(i), 0, 0)),
                  pl.BlockSpec((None, D_MODEL, IN_WIDTH), lambda i: (layer, 0, 0))],
        out_specs=pl.BlockSpec((ROW_TILE, IN_WIDTH), lambda i: (i, 0)),
        out_shape=jax.ShapeDtypeStruct((rows, IN_WIDTH), F32),
        compiler_params=pltpu.CompilerParams(
            dimension_semantics=("arbitrary",), vmem_limit_bytes=VMEM_LIMIT),
        name="inproj",
    )(x2d, mod, w_in_bf)


def _fnet_kernel(u_ref, w1_ref, cl_ref, sl_ref, o_ref):
    u = u_ref[...].astype(BF16)
    t = _dot(u, w1_ref[...])
    uc = t[:, :FNET_WIDTH].astype(BF16)
    us = t[:, FNET_WIDTH:].astype(BF16)
    o = _dot(cl_ref[...], uc) - _dot(sl_ref[...], us)
    o_ref[...] = o.astype(BF16)


def _fnet_consts(L):
    scale = 1.0 / np.sqrt(L * FNET_GDIM)
    k = np.arange(FNET_GDIM)
    ang = 2.0 * np.pi * ((k[:, None] * k[None, :]) % FNET_GDIM) / FNET_GDIM
    eye = np.eye(FNET_WIDTH // FNET_GDIM)
    w1 = np.concatenate([np.kron(eye, np.cos(ang)), np.kron(eye, np.sin(ang))], axis=1) * scale
    n = np.arange(L)
    angl = 2.0 * np.pi * ((n[:, None] * n[None, :]) % L) / L
    tables = (w1, np.cos(angl), np.sin(angl))
    return tuple(jnp.asarray(t, F32).astype(BF16) for t in tables)


def _fnet(proj, L):
    rows = proj.shape[0]
    w1, cl, sl = _fnet_consts(L)
    return pl.pallas_call(
        _fnet_kernel,
        grid=(rows // L,),
        in_specs=[pl.BlockSpec((L, FNET_WIDTH), lambda b: (b, 0)),
                  pl.BlockSpec((FNET_WIDTH, 2 * FNET_WIDTH), lambda b: (0, 0)),
                  pl.BlockSpec((L, L), lambda b: (0, 0)),
                  pl.BlockSpec((L, L), lambda b: (0, 0))],
        out_specs=pl.BlockSpec((L, FNET_WIDTH), lambda b: (b, 0)),
        out_shape=jax.ShapeDtypeStruct((rows, FNET_WIDTH), BF16),
        compiler_params=pltpu.CompilerParams(
            dimension_semantics=("arbitrary",), vmem_limit_bytes=VMEM_LIMIT),
        name="fnet",
    )(proj, w1, cl, sl)


def _rope_tables(L):
    half = HEAD_DIM // 2
    inv = ROPE_BASE ** (-np.arange(0, half, 2, dtype=np.float64) / half)
    n = np.arange(L)
    rows, cols = n // GRID_W, n % GRID_W
    lane = np.arange(LANES)
    f = lane % (half // 2)
    use_col = (lane % HEAD_DIM) >= half
    pos = np.where(use_col[None, :], cols[:, None], rows[:, None]).astype(np.float64)
    ang = pos * inv[f][None, :]
    lo = (lane % half) < (half // 2)
    cos = np.cos(ang)
    sin = np.where(lo[None, :], -np.sin(ang), np.sin(ang))
    return jnp.asarray(cos, F32), jnp.asarray(sin, F32)


def _rope(x, cos, sin_signed, lo_mask):
    quarter = HEAD_DIM // 4
    swapped = jnp.where(lo_mask, pltpu.roll(x, LANES - quarter, axis=1), pltpu.roll(x, quarter, axis=1))
    return x * cos + swapped * sin_signed


def _ret_kernel(*refs, L, layer, use_rope, has_state, want_state):
    it = iter(refs)
    lg_ref = next(it)
    q_ref, k_ref, v_ref, g_ref = next(it), next(it), next(it), next(it)
    if use_rope:
        cos_ref, sin_ref = next(it), next(it)
    if has_state:
        s0_ref = next(it)
    o_ref = next(it)
    if want_state:
        st_ref = next(it)
    m_ref = next(it)

    p = pl.program_id(0)
    b = pl.program_id(1)
    rb = ROW_TILE
    nrb = L // rb
    lane = lax.broadcasted_iota(jnp.int32, (1, LANES), 1)
    h0 = lane < HEAD_DIM

    def lg(direction, head):
        return lg_ref[layer * 2 * N_HEADS + direction * N_HEADS + 2 * p + head]

    @pl.when(b == 0)
    def _():
        for hh in range(2):
            lgf, lgb = lg(0, hh), lg(1, hh)
            for r in range(nrb):
                i = lax.broadcasted_iota(jnp.int32, (rb, L), 0) + r * rb
                j = lax.broadcasted_iota(jnp.int32, (rb, L), 1)
                rel = (i - j).astype(F32)
                arg = jnp.where(rel > 0, rel * lgf, -rel * lgb)
                m = jnp.exp(arg) + jnp.where(rel == 0, 1.0, 0.0)
                m_ref[hh, r * rb:(r + 1) * rb, :] = m.astype(BF16)

    q = q_ref[...]
    k = k_ref[...] * (HEAD_DIM ** -0.5)
    if use_rope:
        lo_mask = (lane % (HEAD_DIM // 2)) < (HEAD_DIM // 4)
        q = _rope(q, cos_ref[...], sin_ref[...], lo_mask)
        k = _rope(k, cos_ref[...], sin_ref[...], lo_mask)
    kb = k.astype(BF16)
    vb = v_ref[...].astype(BF16)
    lgf_vec = jnp.where(h0, lg(0, 0), lg(0, 1))
    lgb_vec = jnp.where(h0, lg(1, 0), lg(1, 1))
    bd = _bd_ones()

    for r in range(nrb):
        qr = q[r * rb:(r + 1) * rb, :]
        qs = jnp.concatenate([jnp.where(h0, qr, 0.0), jnp.where(h0, 0.0, qr)], axis=0).astype(BF16)
        s = _dot_nt(qs, kb)
        mblk = jnp.concatenate([m_ref[0, r * rb:(r + 1) * rb, :], m_ref[1, r * rb:(r + 1) * rb, :]], axis=0)
        pm = (s * mblk.astype(F32)).astype(BF16)
        o2 = _dot(pm, vb)
        o = jnp.where(h0, o2[:rb], o2[rb:])
        if has_state:
            pos = (lax.broadcasted_iota(jnp.int32, (rb, LANES), 0) + r * rb).astype(F32)
            wf = jnp.exp((pos + 1.0) * lgf_vec)
            wb = jnp.exp((float(L) - pos) * lgb_vec)
            xs = jnp.concatenate([qr * wf, qr * wb], axis=1).astype(BF16)
            sbd = jnp.concatenate([s0_ref[0], s0_ref[1]], axis=0).astype(BF16)
            o = o + _dot(xs, sbd)
        mu = _head_sums(o, bd) * (1.0 / HEAD_DIM)
        oc = o - mu
        var = _head_sums(oc * oc, bd) * (1.0 / HEAD_DIM)
        y = oc * lax.rsqrt(var + LN_EPS)
        gr = g_ref[r * rb:(r + 1) * rb, :]
        o_ref[r * rb:(r + 1) * rb, :] = (y * _silu(gr)).astype(BF16)

    if want_state:
        pos = lax.broadcasted_iota(jnp.int32, (L, LANES), 0).astype(F32)
        zf = jnp.exp((float(L) - 1.0 - pos) * lgf_vec)
        zb = jnp.exp(pos * lgb_vec)
        st_ref[0] = _dot((k * zf).T.astype(BF16), vb)
        st_ref[1] = _dot((k * zb).T.astype(BF16), vb)


def _retention(proj, lg_flat, L, layer, rope, s0_bd, want_state):
    rows = proj.shape[0]
    nb = rows // L
    blk = lambda col: pl.BlockSpec((L, LANES), lambda p, b, col=col: (b, col + p))
    in_specs = [pl.BlockSpec(memory_space=pltpu.SMEM),
                blk(COL_RQ), blk(COL_RK), blk(COL_RV), blk(COL_RG)]
    args = [lg_flat, proj, proj, proj, proj]
    if rope is not None:
        in_specs += [pl.BlockSpec((L, LANES), lambda p, b: (0, 0))] * 2
        args += list(rope)
    if s0_bd is not None:
        in_specs.append(pl.BlockSpec((None, None, 2, None, LANES, LANES),
                                     lambda p, b: (b, layer, 0, p, 0, 0)))
        args.append(s0_bd)
    out_specs = [pl.BlockSpec((L, LANES), lambda p, b: (b, p))]
    out_shape = [jax.ShapeDtypeStruct((rows, HEADS_WIDTH), BF16)]
    if want_state:
        out_specs.append(pl.BlockSpec((None, None, 2, LANES, LANES), lambda p, b: (b, p, 0, 0, 0)))
        out_shape.append(jax.ShapeDtypeStruct((nb, N_PAIRS, 2, LANES, LANES), F32))
    kern = functools.partial(_ret_kernel, L=L, layer=layer, use_rope=rope is not None,
                             has_state=s0_bd is not None, want_state=want_state)
    return pl.pallas_call(
        kern,
        grid=(N_PAIRS, nb),
        in_specs=in_specs,
        out_specs=out_specs,
        out_shape=out_shape,
        scratch_shapes=[pltpu.VMEM((2, L, L), BF16)],
        compiler_params=pltpu.CompilerParams(
            dimension_semantics=("arbitrary", "arbitrary"), vmem_limit_bytes=VMEM_LIMIT),
        name="retention",
    )(*args)


def _hgrn_gate(x, llb, l1mlb, omlb):
    ls = jnp.minimum(x, 0.0) - jnp.log1p(jnp.exp(-jnp.abs(x)))
    t = l1mlb + ls
    logf = jnp.maximum(llb, t) + jnp.log1p(jnp.exp(-jnp.abs(llb - t)))
    kk = omlb / (1.0 + jnp.exp(x))
    return logf, kk


def _hgrn_kernel(*refs, L, has_state, want_state):
    it = iter(refs)
    gp_ref = next(it)
    hq_ref, ff_ref, fb_ref, hi_ref, hg_ref = next(it), next(it), next(it), next(it), next(it)
    if has_state:
        s0_ref = next(it)
    o_ref = next(it)
    if want_state:
        st_ref = next(it)
    oacc, xif_ref, xib_ref, dstf_ref, dstb_ref, dec_ref = (next(it) for _ in range(6))

    C = HG_CHUNK
    nc = L // C
    lane = lax.broadcasted_iota(jnp.int32, (1, LANES), 1)
    h0 = lane < HEAD_DIM
    bd = _bd_ones()
    bd_mask = bd > 0
    rowi = lax.broadcasted_iota(jnp.int32, (C, C), 0)
    coli = lax.broadcasted_iota(jnp.int32, (C, C), 1)
    tri = jnp.where(rowi >= coli, 1.0, 0.0).astype(BF16)
    rowl = lax.broadcasted_iota(jnp.int32, (C, LANES), 0)
    sub = lax.broadcasted_iota(jnp.int32, (C // SUBLANES, SUBLANES, LANES), 1)
    row2 = lax.broadcasted_iota(jnp.int32, (2 * C, C), 0) % C
    col2 = lax.broadcasted_iota(jnp.int32, (2 * C, C), 1)
    lane2 = lax.broadcasted_iota(jnp.int32, (1, 2 * LANES), 1) % LANES
    h0_2 = lane2 < HEAD_DIM

    llb_f, llb_b = gp_ref[0:1, :], gp_ref[1:2, :]
    l1m_f, l1m_b = gp_ref[2:3, :], gp_ref[3:4, :]
    oml_f, oml_b = gp_ref[4:5, :], gp_ref[5:6, :]

    def to3(x):
        return x.reshape(C // SUBLANES, SUBLANES, LANES)

    def chunk_body(c, carry):
        r0 = pl.multiple_of(c * C, C)
        rows = pl.ds(r0, C)
        q = _silu(hq_ref[rows, :])
        v = hi_ref[rows, :]
        lff, kf = _hgrn_gate(ff_ref[rows, :], llb_f, l1m_f, oml_f)
        lfb, kb = _hgrn_gate(fb_ref[rows, :], llb_b, l1m_b, oml_b)
        x1, x2, x3 = _split3(jnp.concatenate([lff, lfb], axis=1))
        cs = _dot(tri, x1) + _dot(tri, x2) + _dot(tri, x3)
        bf = cs[:, :LANES]
        bb_incl = cs[:, LANES:]
        pb = bb_incl - lfb
        bf_last = bf[C - 1:C, :]
        tot_b = bb_incl[C - 1:C, :]
        vb = v.astype(BF16)

        d0 = (q * (kf + kb)).astype(BF16)
        o = _dot(d0, bd) * v
        q3, v3, bf3, pb3, kf3, kb3 = to3(q), to3(v), to3(bf), to3(pb), to3(kf), to3(kb)
        for dl in range(1, SUBLANES):
            fw = sub >= dl
            rel = jnp.where(fw, bf3 - pltpu.roll(bf3, dl, axis=1), pltpu.roll(pb3, dl, axis=1) - pb3)
            ks = jnp.where(fw, pltpu.roll(kf3, dl, axis=1), pltpu.roll(kb3, dl, axis=1))
            d = (q3 * ks * jnp.exp(rel)).reshape(C, LANES).astype(BF16)
            o = o + _dot(d, bd) * pltpu.roll(v3, dl, axis=1).reshape(C, LANES)

        a_tot = jnp.zeros((2 * C, C), F32)
        m = SUBLANES
        while m < C:
            nblk = C // (2 * m)
            rf = jnp.broadcast_to(bf.reshape(nblk, 2 * m, LANES)[:, m - 1:m, :], (nblk, 2 * m, LANES)).reshape(C, LANES)
            rb_ = jnp.broadcast_to(pb.reshape(nblk, 2 * m, LANES)[:, m:m + 1, :], (nblk, 2 * m, LANES)).reshape(C, LANES)
            ef = jnp.exp(-jnp.abs(bf - rf))
            eb = jnp.exp(-jnp.abs(pb - rb_))
            up = (rowl & m) != 0
            x = jnp.concatenate([jnp.where(up, q * ef, 0.0), jnp.where(up, 0.0, q * eb)], axis=1)
            y = jnp.concatenate([jnp.where(up, 0.0, kf * ef), jnp.where(up, kb * eb, 0.0)], axis=1)
            xs = jnp.concatenate([jnp.where(h0_2, x, 0.0), jnp.where(h0_2, 0.0, x)], axis=0).astype(BF16)
            a = _dot_nt(xs, y.astype(BF16))
            if 2 * m < C:
                a = jnp.where((row2 // (2 * m)) == (col2 // (2 * m)), a, 0.0)
            a_tot = a_tot + a
            m *= 2
        o2 = _dot(a_tot.astype(BF16), vb)
        o = o + jnp.where(h0, o2[:C], o2[C:])
        oacc[rows, :] = o

        vt = v.T.astype(BF16)
        dstf_ref[c] = jnp.where(bd_mask, _dot(vt, (kf * jnp.exp(bf_last - bf)).astype(BF16)), 0.0)
        dstb_ref[c] = jnp.where(bd_mask, _dot(vt, (kb * jnp.exp(pb)).astype(BF16)), 0.0)
        dec_ref[c, 0:1, :] = jnp.exp(bf_last)
        dec_ref[c, 1:2, :] = jnp.exp(tot_b)
        xif_ref[rows, :] = (q * jnp.exp(bf)).astype(BF16)
        xib_ref[rows, :] = (q * jnp.exp(tot_b - pb)).astype(BF16)
        return carry

    lax.fori_loop(0, nc, chunk_body, 0)

    if has_state:
        stf0, stb0 = s0_ref[0], s0_ref[1]
    else:
        stf0 = jnp.zeros((LANES, LANES), F32)
        stb0 = stf0

    def fw_body(c, st):
        rows = pl.ds(pl.multiple_of(c * C, C), C)
        oacc[rows, :] += _dot_nt(xif_ref[rows, :], st.astype(BF16))
        return st * dec_ref[c, 0:1, :] + dstf_ref[c]

    stf = lax.fori_loop(0, nc, fw_body, stf0)

    def bw_body(t, st):
        c = nc - 1 - t
        rows = pl.ds(pl.multiple_of(c * C, C), C)
        o = oacc[rows, :] + _dot_nt(xib_ref[rows, :], st.astype(BF16))
        ms = _head_sums(o * o, bd) * (1.0 / HEAD_DIM)
        y = o * lax.rsqrt(ms + LN_EPS)
        o_ref[rows, :] = (y * _silu(hg_ref[rows, :])).astype(BF16)
        return st * dec_ref[c, 1:2, :] + dstb_ref[c]

    stb = lax.fori_loop(0, nc, bw_body, stb0)

    if want_state:
        st_ref[0] = stf
        st_ref[1] = stb


def _hgrn(proj, gate_params, L, layer, s0_t, want_state):
    rows = proj.shape[0]
    nb = rows // L
    nc = L // HG_CHUNK
    blk = lambda col: pl.BlockSpec((L, LANES), lambda p, b, col=col: (b, col + p))
    in_specs = [pl.BlockSpec((None, None, SUBLANES, LANES), lambda p, b: (layer, p, 0, 0)),
                blk(COL_HQ), blk(COL_HFF), blk(COL_HFB), blk(COL_HI), blk(COL_HG)]
    args = [gate_params, proj, proj, proj, proj, proj]
    if s0_t is not None:
        in_specs.append(pl.BlockSpec((None, None, 2, None, LANES, LANES),
                                     lambda p, b: (b, layer, 0, p, 0, 0)))
        args.append(s0_t)
    out_specs = [pl.BlockSpec((L, LANES), lambda p, b: (b, p))]
    out_shape = [jax.ShapeDtypeStruct((rows, HEADS_WIDTH), BF16)]
    if want_state:
        out_specs.append(pl.BlockSpec((None, None, 2, LANES, LANES), lambda p, b: (b, p, 0, 0, 0)))
        out_shape.append(jax.ShapeDtypeStruct((nb, N_PAIRS, 2, LANES, LANES), F32))
    kern = functools.partial(_hgrn_kernel, L=L, has_state=s0_t is not None, want_state=want_state)
    return pl.pallas_call(
        kern,
        grid=(N_PAIRS, nb),
        in_specs=in_specs,
        out_specs=out_specs,
        out_shape=out_shape,
        scratch_shapes=[pltpu.VMEM((L, LANES), F32),
                        pltpu.VMEM((L, LANES), BF16),
                        pltpu.VMEM((L, LANES), BF16),
                        pltpu.VMEM((nc, LANES, LANES), F32),
                        pltpu.VMEM((nc, LANES, LANES), F32),
                        pltpu.VMEM((nc, SUBLANES, LANES), F32)],
        compiler_params=pltpu.CompilerParams(
            dimension_semantics=("arbitrary", "arbitrary"), vmem_limit_bytes=VMEM_LIMIT),
        name="hgrn2",
    )(*args)


def _ffn_kernel(x_ref, of_ref, or_ref, oh_ref, mod_ref, wo_ref, lng_ref, lnb_ref,
                wg_ref, wu_ref, wd_ref, y_ref):
    c1 = FNET_WIDTH
    c2 = FNET_WIDTH + HEADS_WIDTH
    mix = (_dot(of_ref[...], wo_ref[0:c1, :]) + _dot(or_ref[...], wo_ref[c1:c2, :])
           + _dot(oh_ref[...], wo_ref[c2:, :]))
    g1 = mod_ref[:, 2 * D_MODEL:3 * D_MODEL]
    sh2 = mod_ref[:, 3 * D_MODEL:4 * D_MODEL]
    sc2 = mod_ref[:, 4 * D_MODEL:5 * D_MODEL]
    g2 = mod_ref[:, 5 * D_MODEL:6 * D_MODEL]
    x1 = _ln(ALPHA * x_ref[...] + g1 * mix) * lng_ref[0:1, :] + lnb_ref[0:1, :]
    h2 = (_ln(x1) * (1.0 + sc2) + sh2).astype(BF16)
    gate = _dot(h2, wg_ref[...])
    up = _dot(h2, wu_ref[...])
    act = (_silu(gate) * up).astype(BF16)
    ffn = _dot(act, wd_ref[...])
    y_ref[...] = _ln(ALPHA * x1 + g2 * ffn) * lng_ref[1:2, :] + lnb_ref[1:2, :]


def _out_ffn(x2d, o_f, o_r, o_h, mod, w_out_bf, ln_g, ln_b, wg_bf, wu_bf, wd_bf, layer, mod_row):
    rows = x2d.shape[0]
    once = pl.Buffered(1)
    row_blk = lambda w: pl.BlockSpec((ROW_TILE, w), lambda i: (i, 0))
    return pl.pallas_call(
        _ffn_kernel,
        grid=(rows // ROW_TILE,),
        in_specs=[row_blk(D_MODEL), row_blk(FNET_WIDTH), row_blk(HEADS_WIDTH), row_blk(HEADS_WIDTH),
                  pl.BlockSpec((None, None, 1, N_MOD * D_MODEL), lambda i: (layer, mod_row(i), 0, 0)),
                  pl.BlockSpec((None, D_MODEL, D_MODEL), lambda i: (layer, 0, 0), pipeline_mode=once),
                  pl.BlockSpec((None, 2, D_MODEL), lambda i: (layer, 0, 0)),
                  pl.BlockSpec((None, 2, D_MODEL), lambda i: (layer, 0, 0)),
                  pl.BlockSpec((None, D_MODEL, D_FF), lambda i: (layer, 0, 0), pipeline_mode=once),
                  pl.BlockSpec((None, D_MODEL, D_FF), lambda i: (layer, 0, 0), pipeline_mode=once),
                  pl.BlockSpec((None, D_FF, D_MODEL), lambda i: (layer, 0, 0), pipeline_mode=once)],
        out_specs=row_blk(D_MODEL),
        out_shape=jax.ShapeDtypeStruct((rows, D_MODEL), F32),
        compiler_params=pltpu.CompilerParams(
            dimension_semantics=("arbitrary",), vmem_limit_bytes=VMEM_LIMIT),
        name="out_ffn",
    )(x2d, o_f, o_r, o_h, mod, w_out_bf, ln_g, ln_b, wg_bf, wu_bf, wd_bf)


def _pair_blockdiag(s, transpose):
    if transpose:
        s = jnp.swapaxes(s, -1, -2)
    lead = s.shape[:-3]
    s = s.reshape(lead + (N_PAIRS, 2, HEAD_DIM, HEAD_DIM))
    z = jnp.zeros_like(s[..., 0, :, :])
    top = jnp.concatenate([s[..., 0, :, :], z], axis=-1)
    bot = jnp.concatenate([z, s[..., 1, :, :]], axis=-1)
    return jnp.concatenate([top, bot], axis=-2)


def _pair_diag_blocks(s, transpose):
    a = s[..., :HEAD_DIM, :HEAD_DIM]
    b = s[..., HEAD_DIM:, HEAD_DIM:]
    out = jnp.stack([a, b], axis=-3)
    if transpose:
        out = jnp.swapaxes(out, -1, -2)
    return out


def _states_out(st, transpose):
    blocks = _pair_diag_blocks(st, transpose)
    blocks = jnp.transpose(blocks, (0, 2, 1, 3, 4, 5))
    return blocks.reshape(st.shape[0], 2, N_HEADS, HEAD_DIM, HEAD_DIM)


def kernel(x_prompt, x_sample, c, state_ret, state_hgrn, c_ctx, w_mod, b_mod, w_in, w_out,
           ret_log_decay, hg_lower_bound, ln_g, ln_b, w_gate, w_up, w_down):
    B, S, _ = x_prompt.shape
    DB, DS, _ = x_sample.shape

    p = jax.nn.softmax(hg_lower_bound.astype(F32), axis=1)
    cum = jnp.cumsum(p, axis=1)
    lbs = cum - cum[:, :1]
    kinds = jnp.stack([jnp.log(lbs), jnp.log1p(-lbs), 1.0 - lbs], axis=0)
    gp = jnp.transpose(kinds.reshape(3, 2, DEPTH, N_PAIRS, LANES), (2, 3, 0, 1, 4))
    gp = gp.reshape(DEPTH, N_PAIRS, 6, LANES)
    gp = jnp.concatenate([gp, jnp.zeros((DEPTH, N_PAIRS, SUBLANES - 6, LANES), F32)], axis=2)
    lg_flat = (-jnp.exp(ret_log_decay.astype(F32))).reshape(-1)

    w_in_bf = w_in.astype(BF16)
    w_out_bf = w_out.astype(BF16)
    wg_bf = w_gate.astype(BF16)
    wu_bf = w_up.astype(BF16)
    wd_bf = w_down.astype(BF16)

    cv = jnp.concatenate([c_ctx[None, :], c, jnp.zeros((MOD_ROWS - 1 - DB, D_MODEL), F32)], axis=0)
    mod = _modulation(cv, w_mod, b_mod).reshape(DEPTH, MOD_ROWS, 1, N_MOD * D_MODEL)

    sr_bd = _pair_blockdiag(state_ret.astype(F32), transpose=False)
    sh_t = _pair_blockdiag(state_hgrn.astype(F32), transpose=True)
    rope = _rope_tables(DS)

    ctx_row = lambda i: 0
    smp_row = lambda i: 1 + i // (DS // ROW_TILE)

    y = x_prompt.reshape(B * S, D_MODEL)
    z = x_sample.reshape(DB * DS, D_MODEL)
    ret_states, hg_states = [], []
    for l in range(DEPTH):
        proj = _inproj(y, mod, w_in_bf, l, ctx_row)
        o_f = _fnet(proj, S)
        o_r, st_r = _retention(proj, lg_flat, S, l, None, None, True)
        o_h, st_h = _hgrn(proj, gp, S, l, None, True)
        y = _out_ffn(y, o_f, o_r, o_h, mod, w_out_bf, ln_g, ln_b, wg_bf, wu_bf, wd_bf, l, ctx_row)
        ret_states.append(_states_out(st_r, transpose=False))
        hg_states.append(_states_out(st_h, transpose=True))
        proj = _inproj(z, mod, w_in_bf, l, smp_row)
        o_f = _fnet(proj, DS)
        (o_r,) = _retention(proj, lg_flat, DS, l, rope, sr_bd, False)
        (o_h,) = _hgrn(proj, gp, DS, l, sh_t, False)
        z = _out_ffn(z, o_f, o_r, o_h, mod, w_out_bf, ln_g, ln_b, wg_bf, wu_bf, wd_bf, l, smp_row)

    new_state_ret = jnp.stack(ret_states, axis=1)
    new_state_hgrn = jnp.stack(hg_states, axis=1)
    return (y.reshape(B, S, D_MODEL), z.reshape(DB, DS, D_MODEL), new_state_ret, new_state_hgrn)
```

```python
import functools

import numpy as np
import jax
import jax.numpy as jnp
from jax import lax
from jax.experimental import pallas as pl
from jax.experimental.pallas import tpu as pltpu

F32 = jnp.float32
BF16 = jnp.bfloat16

D_MODEL = 1024
DEPTH = 2
GRID_W = 64
FNET_WIDTH = 256
FNET_GDIM = 64
N_HEADS = 6
N_PAIRS = N_HEADS // 2
HEAD_DIM = 64
LANES = 128
SUBLANES = 8
HEADS_WIDTH = N_HEADS * HEAD_DIM
IN_WIDTH = 3712
D_FF = 2816
ROPE_BASE = 10000.0
LN_EPS = 1e-5
LOG2E = 1.4426950408889634
N_MOD = 6
ALPHA = (2 * DEPTH) ** 0.25
MOD_ROWS = 8
ROW_TILE = 512
RET_ROWS = 256
HG_CHUNK = 128
VMEM_LIMIT = 56 * 1024 * 1024

COL_RQ, COL_RK, COL_RV, COL_RG = 2, 5, 8, 11
COL_HQ, COL_HFF, COL_HFB, COL_HI, COL_HG = 14, 17, 20, 23, 26


def _silu(x):
    return x / (1.0 + jnp.exp(-x))


def _ln(x):
    mu = jnp.mean(x, axis=-1, keepdims=True)
    xc = x - mu
    var = jnp.mean(xc * xc, axis=-1, keepdims=True)
    return xc * lax.rsqrt(var + LN_EPS)


def _dot(a, b):
    return jnp.dot(a, b, preferred_element_type=F32)


def _dot_nt(a, b):
    return lax.dot_general(a, b, (((1,), (1,)), ((), ())), preferred_element_type=F32)


def _split2(x):
    hi = x.astype(BF16)
    lo = (x - hi.astype(F32)).astype(BF16)
    return hi, lo


def _split3(x):
    x1 = x.astype(BF16)
    r1 = x - x1.astype(F32)
    x2 = r1.astype(BF16)
    x3 = (r1 - x2.astype(F32)).astype(BF16)
    return x1, x2, x3


def _head_sums(x, bd_ones):
    hi, lo = _split2(x)
    return _dot(hi, bd_ones) + _dot(lo, bd_ones)


def _bd_ones():
    r = lax.broadcasted_iota(jnp.int32, (LANES, LANES), 0)
    c = lax.broadcasted_iota(jnp.int32, (LANES, LANES), 1)
    return jnp.where((r < HEAD_DIM) == (c < HEAD_DIM), 1.0, 0.0).astype(BF16)


def _mod_kernel(cv_ref, w_ref, b_ref, o_ref):
    s = _silu(cv_ref[...]).astype(BF16)
    o_ref[...] = _dot(s, w_ref[...].astype(BF16)) + b_ref[...]


def _modulation(cv, w_mod, b_mod):
    tn = 512
    width = N_MOD * D_MODEL
    return pl.pallas_call(
        _mod_kernel,
        grid=(DEPTH, width // tn),
        in_specs=[pl.BlockSpec((MOD_ROWS, D_MODEL), lambda l, j: (0, 0)),
                  pl.BlockSpec((None, D_MODEL, tn), lambda l, j: (l, 0, j)),
                  pl.BlockSpec((None, 1, tn), lambda l, j: (l, 0, j))],
        out_specs=pl.BlockSpec((None, MOD_ROWS, tn), lambda l, j: (l, 0, j)),
        out_shape=jax.ShapeDtypeStruct((DEPTH, MOD_ROWS, width), F32),
        compiler_params=pltpu.CompilerParams(
            dimension_semantics=("arbitrary", "arbitrary"), vmem_limit_bytes=VMEM_LIMIT),
        name="modulation",
    )(cv, w_mod, b_mod.reshape(DEPTH, 1, width))


def _inproj_kernel(x_ref, mod_ref, w_ref, o_ref):
    sh = mod_ref[:, 0:D_MODEL]
    sc = mod_ref[:, D_MODEL:2 * D_MODEL]
    h = _ln(x_ref[...]) * (1.0 + sc) + sh
    o_ref[...] = _dot(h.astype(BF16), w_ref[...])


def _inproj(x2d, mod, w_in_bf, layer, mod_row):
    rows = x2d.shape[0]
    return pl.pallas_call(
        _inproj_kernel,
        grid=(rows // ROW_TILE,),
        in_specs=[pl.BlockSpec((ROW_TILE, D_MODEL), lambda i: (i, 0)),
                  pl.BlockSpec((None, None, 1, N_MOD * D_MODEL), lambda i: (layer, mod_row(i), 0, 0)),
                  pl.BlockSpec((None, D_MODEL, IN_WIDTH), lambda i: (layer, 0, 0))],
        out_specs=pl.BlockSpec((ROW_TILE, IN_WIDTH), lambda i: (i, 0)),
        out_shape=jax.ShapeDtypeStruct((rows, IN_WIDTH), F32),
        compiler_params=pltpu.CompilerParams(
            dimension_semantics=("arbitrary",), vmem_limit_bytes=VMEM_LIMIT),
        name="inproj",
    )(x2d, mod, w_in_bf)


def _fnet_kernel(u_ref, w1_ref, cl_ref, sl_ref, o_ref):
    u = u_ref[...].astype(BF16)
    t = _dot(u, w1_ref[...])
    uc = t[:, :FNET_WIDTH].astype(BF16)
    us = t[:, FNET_WIDTH:].astype(BF16)
    o = _dot(cl_ref[...], uc) - _dot(sl_ref[...], us)
    o_ref[...] = o.astype(BF16)


def _fnet_consts(L):
    scale = 1.0 / np.sqrt(L * FNET_GDIM)
    k = np.arange(FNET_GDIM)
    ang = 2.0 * np.pi * ((k[:, None] * k[None, :]) % FNET_GDIM) / FNET_GDIM
    eye = np.eye(FNET_WIDTH // FNET_GDIM)
    w1 = np.concatenate([np.kron(eye, np.cos(ang)), np.kron(eye, np.sin(ang))], axis=1) * scale
    n = np.arange(L)
    angl = 2.0 * np.pi * ((n[:, None] * n[None, :]) % L) / L
    tables = (w1, np.cos(angl), np.sin(angl))
    return tuple(jnp.asarray(t, F32).astype(BF16) for t in tables)


def _fnet(proj, L):
    rows = proj.shape[0]
    w1, cl, sl = _fnet_consts(L)
    return pl.pallas_call(
        _fnet_kernel,
        grid=(rows // L,),
        in_specs=[pl.BlockSpec((L, FNET_WIDTH), lambda b: (b, 0)),
                  pl.BlockSpec((FNET_WIDTH, 2 * FNET_WIDTH), lambda b: (0, 0)),
                  pl.BlockSpec((L, L), lambda b: (0, 0)),
                  pl.BlockSpec((L, L), lambda b: (0, 0))],
        out_specs=pl.BlockSpec((L, FNET_WIDTH), lambda b: (b, 0)),
        out_shape=jax.ShapeDtypeStruct((rows, FNET_WIDTH), BF16),
        compiler_params=pltpu.CompilerParams(
            dimension_semantics=("arbitrary",), vmem_limit_bytes=VMEM_LIMIT),
        name="fnet",
    )(proj, w1, cl, sl)


def _rope_tables(L):
    half = HEAD_DIM // 2
    inv = ROPE_BASE ** (-np.arange(0, half, 2, dtype=np.float64) / half)
    n = np.arange(L)
    rows, cols = n // GRID_W, n % GRID_W
    lane = np.arange(LANES)
    f = lane % (half // 2)
    use_col = (lane % HEAD_DIM) >= half
    pos = np.where(use_col[None, :], cols[:, None], rows[:, None]).astype(np.float64)
    ang = pos * inv[f][None, :]
    lo = (lane % half) < (half // 2)
    cos = np.cos(ang)
    sin = np.where(lo[None, :], -np.sin(ang), np.sin(ang))
    return jnp.asarray(cos, F32), jnp.asarray(sin, F32)


def _rope(x, cos, sin_signed, lo_mask):
    quarter = HEAD_DIM // 4
    swapped = jnp.where(lo_mask, pltpu.roll(x, LANES - quarter, axis=1), pltpu.roll(x, quarter, axis=1))
    return x * cos + swapped * sin_signed


def _ret_kernel(*refs, L, layer, use_rope, has_state, want_state):
    it = iter(refs)
    lg_ref = next(it)
    q_ref, k_ref, v_ref, g_ref = next(it), next(it), next(it), next(it)
    if use_rope:
        cos_ref, sin_ref = next(it), next(it)
    if has_state:
        s0_ref = next(it)
    o_ref = next(it)
    if want_state:
        st_ref = next(it)
    oacc = next(it)

    p = pl.program_id(0)
    T = RET_ROWS
    nc = L // T
    lane = lax.broadcasted_iota(jnp.int32, (1, LANES), 1)
    h0 = lane < HEAD_DIM
    bd = _bd_ones()
    bd_mask = bd > 0

    def lg(direction, head):
        return lg_ref[layer * 2 * N_HEADS + direction * N_HEADS + 2 * p + head]

    i = lax.broadcasted_iota(jnp.int32, (T, T), 0)
    j = lax.broadcasted_iota(jnp.int32, (T, T), 1)
    rel = (i - j).astype(F32)
    masks = []
    for hh in range(2):
        arg = jnp.where(rel > 0, rel * lg(0, hh), -rel * lg(1, hh))
        masks.append(jnp.exp(arg) + jnp.where(rel == 0, 1.0, 0.0))
    mcat = jnp.concatenate(masks, axis=0)

    lgf_vec = jnp.where(h0, lg(0, 0), lg(0, 1))
    lgb_vec = jnp.where(h0, lg(1, 0), lg(1, 1))
    pos = lax.broadcasted_iota(jnp.int32, (T, LANES), 0).astype(F32)
    wf = jnp.exp((pos + 1.0) * lgf_vec)
    wb = jnp.exp((float(T) - pos) * lgb_vec)
    zf = jnp.exp((float(T) - 1.0 - pos) * lgf_vec)
    zb = jnp.exp(pos * lgb_vec)
    gf_chunk = jnp.exp(float(T) * lgf_vec)
    gb_chunk = jnp.exp(float(T) * lgb_vec)
    use_states = has_state or nc > 1

    q = q_ref[...]
    k = k_ref[...] * (HEAD_DIM ** -0.5)
    if use_rope:
        lo_mask = (lane % (HEAD_DIM // 2)) < (HEAD_DIM // 4)
        q = _rope(q, cos_ref[...], sin_ref[...], lo_mask)
        k = _rope(k, cos_ref[...], sin_ref[...], lo_mask)

    dstf, dstb, xis = [], [], []
    for c in range(nc):
        rows = slice(c * T, (c + 1) * T)
        qc, kc = q[rows, :], k[rows, :]
        vb = v_ref[rows, :].astype(BF16)
        qs = jnp.concatenate([jnp.where(h0, qc, 0.0), jnp.where(h0, 0.0, qc)], axis=0).astype(BF16)
        s = _dot_nt(qs, kc.astype(BF16))
        o2 = _dot((s * mcat).astype(BF16), vb)
        oacc[rows, :] = jnp.where(h0, o2[:T], o2[T:])
        if use_states or want_state:
            vt = v_ref[rows, :].T.astype(BF16)
            dstf.append(jnp.where(bd_mask, _dot(vt, (kc * zf).astype(BF16)), 0.0))
            dstb.append(jnp.where(bd_mask, _dot(vt, (kc * zb).astype(BF16)), 0.0))
        if use_states:
            xis.append(jnp.concatenate([qc * wf, qc * wb], axis=1).astype(BF16))

    if has_state:
        stf, stb = s0_ref[0], s0_ref[1]
    else:
        stf = jnp.zeros((LANES, LANES), F32)
        stb = stf
    stf_in, stb_in = [None] * nc, [None] * nc
    if use_states or want_state:
        for c in range(nc):
            stf_in[c] = stf
            stf = stf * gf_chunk + dstf[c]
        for c in reversed(range(nc)):
            stb_in[c] = stb
            stb = stb * gb_chunk + dstb[c]

    for c in range(nc):
        rows = slice(c * T, (c + 1) * T)
        o = oacc[rows, :]
        if use_states:
            st = jnp.concatenate([stf_in[c], stb_in[c]], axis=1).astype(BF16)
            o = o + _dot_nt(xis[c], st)
        mu = _head_sums(o, bd) * (1.0 / HEAD_DIM)
        oc = o - mu
        var = _head_sums(oc * oc, bd) * (1.0 / HEAD_DIM)
        y = oc * lax.rsqrt(var + LN_EPS)
        o_ref[rows, :] = (y * _silu(g_ref[rows, :])).astype(BF16)

    if want_state:
        st_ref[0] = stf
        st_ref[1] = stb


def _retention(proj, lg_flat, L, layer, rope, s0_bd, want_state):
    rows = proj.shape[0]
    nb = rows // L
    blk = lambda col: pl.BlockSpec((L, LANES), lambda p, b, col=col: (b, col + p))
    in_specs = [pl.BlockSpec(memory_space=pltpu.SMEM),
                blk(COL_RQ), blk(COL_RK), blk(COL_RV), blk(COL_RG)]
    args = [lg_flat, proj, proj, proj, proj]
    if rope is not None:
        in_specs += [pl.BlockSpec((L, LANES), lambda p, b: (0, 0))] * 2
        args += list(rope)
    if s0_bd is not None:
        in_specs.append(pl.BlockSpec((None, None, 2, None, LANES, LANES),
                                     lambda p, b: (b, layer, 0, p, 0, 0)))
        args.append(s0_bd)
    out_specs = [pl.BlockSpec((L, LANES), lambda p, b: (b, p))]
    out_shape = [jax.ShapeDtypeStruct((rows, HEADS_WIDTH), BF16)]
    if want_state:
        out_specs.append(pl.BlockSpec((None, None, 2, LANES, LANES), lambda p, b: (b, p, 0, 0, 0)))
        out_shape.append(jax.ShapeDtypeStruct((nb, N_PAIRS, 2, LANES, LANES), F32))
    kern = functools.partial(_ret_kernel, L=L, layer=layer, use_rope=rope is not None,
                             has_state=s0_bd is not None, want_state=want_state)
    return pl.pallas_call(
        kern,
        grid=(N_PAIRS, nb),
        in_specs=in_specs,
        out_specs=out_specs,
        out_shape=out_shape,
        scratch_shapes=[pltpu.VMEM((L, LANES), F32)],
        compiler_params=pltpu.CompilerParams(
            dimension_semantics=("arbitrary", "arbitrary"), vmem_limit_bytes=VMEM_LIMIT),
        name="retention",
    )(*args)


def _hgrn_gate(x, llb, l1mlb, omlb):
    ls = jnp.minimum(x, 0.0) - jnp.log(1.0 + jnp.exp(-jnp.abs(x)))
    t = l1mlb + ls
    logf = jnp.maximum(llb, t) + jnp.log(1.0 + jnp.exp(-jnp.abs(llb - t)))
    kk = omlb / (1.0 + jnp.exp(x))
    return logf, kk


def _hgrn_kernel(*refs, L, has_state, want_state):
    it = iter(refs)
    gp_ref = next(it)
    hq_ref, ff_ref, fb_ref, hi_ref, hg_ref = next(it), next(it), next(it), next(it), next(it)
    if has_state:
        s0_ref = next(it)
    o_ref = next(it)
    if want_state:
        st_ref = next(it)
    oacc, xi_ref, dstf_ref, dstb_ref, dec_ref, stcat_ref, lvl_ref = (next(it) for _ in range(7))

    C = HG_CHUNK
    nc = L // C
    lane = lax.broadcasted_iota(jnp.int32, (1, LANES), 1)
    h0 = lane < HEAD_DIM
    bd = _bd_ones()
    bd_mask = bd > 0
    rowi = lax.broadcasted_iota(jnp.int32, (C, C), 0)
    coli = lax.broadcasted_iota(jnp.int32, (C, C), 1)
    tri = jnp.where(rowi >= coli, 1.0, 0.0).astype(BF16)
    sub = lax.broadcasted_iota(jnp.int32, (C // SUBLANES, SUBLANES, LANES), 1)
    row2 = lax.broadcasted_iota(jnp.int32, (2 * C, C), 0) % C
    col2 = lax.broadcasted_iota(jnp.int32, (2 * C, C), 1)
    lvl_ref[...] = 31 - lax.clz(row2 ^ col2)

    llb_f, llb_b = gp_ref[0:1, :], gp_ref[1:2, :]
    l1m_f, l1m_b = gp_ref[2:3, :], gp_ref[3:4, :]
    oml_f, oml_b = gp_ref[4:5, :], gp_ref[5:6, :]

    def to3(x):
        return x.reshape(C // SUBLANES, SUBLANES, LANES)

    def chunk_body(c, carry):
        r0 = pl.multiple_of(c * C, C)
        rows = pl.ds(r0, C)
        q = _silu(hq_ref[rows, :])
        v = hi_ref[rows, :]
        lff, kf = _hgrn_gate(ff_ref[rows, :], llb_f, l1m_f, oml_f)
        lfb, kb = _hgrn_gate(fb_ref[rows, :], llb_b, l1m_b, oml_b)
        lf2 = jnp.concatenate([lff, lfb], axis=1) * LOG2E
        x1, x2, x3 = _split3(lf2)
        cs = _dot(tri, x1) + _dot(tri, x2) + _dot(tri, x3)
        bf = cs[:, :LANES]
        bb_incl = cs[:, LANES:]
        pb = bb_incl - lf2[:, LANES:]
        bf_last = bf[C - 1:C, :]
        tot_b = bb_incl[C - 1:C, :]
        vb = v.astype(BF16)

        d0 = (q * (kf + kb)).astype(BF16)
        o = _dot(d0, bd) * v
        q3, v3, bf3, pb3, kf3, kb3 = to3(q), to3(v), to3(bf), to3(pb), to3(kf), to3(kb)
        nbf3, npb3 = -bf3, -pb3
        for dl in range(1, SUBLANES):
            src_fw = sub < (SUBLANES - dl)
            u = jnp.where(src_fw, nbf3, pb3)
            kmix = jnp.where(src_fw, kf3, kb3)
            w = jnp.where(sub >= dl, bf3, npb3)
            rel = w + pltpu.roll(u, dl, axis=1)
            d = (q3 * pltpu.roll(kmix, dl, axis=1) * jnp.exp2(rel)).reshape(C, LANES).astype(BF16)
            o = o + _dot(d, bd) * pltpu.roll(v3, dl, axis=1).reshape(C, LANES)

        lvl = lvl_ref[...]
        a_tot = jnp.zeros((2 * C, C), F32)
        m = SUBLANES
        while m < C:
            nblk = C // (2 * m)

            def halves(x):
                x4 = x.reshape(nblk, 2, m, LANES)
                return x4[:, 0], x4[:, 1]

            def join(lo, up):
                return jnp.stack([lo, up], axis=1).reshape(C, LANES)

            q_lo, q_up = halves(q)
            kf_lo, _ = halves(kf)
            _, kb_up = halves(kb)
            bf_lo, bf_up = halves(bf)
            pb_lo, pb_up = halves(pb)
            rf = bf_lo[:, m - 1:m, :]
            rb_ = pb_up[:, 0:1, :]
            xq = join(q_lo * jnp.exp2(rb_ - pb_lo), q_up * jnp.exp2(bf_up - rf))
            yk = join(kf_lo * jnp.exp2(rf - bf_lo), kb_up * jnp.exp2(pb_up - rb_))
            xs = jnp.concatenate([jnp.where(h0, xq, 0.0), jnp.where(h0, 0.0, xq)], axis=0).astype(BF16)
            a = _dot_nt(xs, yk.astype(BF16))
            a_tot = jnp.where(lvl == (m.bit_length() - 1), a, a_tot)
            m *= 2
        o2 = _dot(a_tot.astype(BF16), vb)
        o = o + jnp.where(h0, o2[:C], o2[C:])
        oacc[rows, :] = o

        vt = v.T.astype(BF16)
        dstf_ref[c] = jnp.where(bd_mask, _dot(vt, (kf * jnp.exp2(bf_last - bf)).astype(BF16)), 0.0)
        dstb_ref[c] = jnp.where(bd_mask, _dot(vt, (kb * jnp.exp2(pb)).astype(BF16)), 0.0)
        dec_ref[c, 0:1, :] = jnp.exp2(bf_last)
        dec_ref[c, 1:2, :] = jnp.exp2(tot_b)
        xi_ref[rows, 0:LANES] = (q * jnp.exp2(bf)).astype(BF16)
        xi_ref[rows, LANES:] = (q * jnp.exp2(tot_b - pb)).astype(BF16)
        return carry

    lax.fori_loop(0, nc, chunk_body, 0, unroll=2)

    if has_state:
        stf, stb = s0_ref[0], s0_ref[1]
    else:
        stf = jnp.zeros((LANES, LANES), F32)
        stb = stf

    for c in range(nc):
        stcat_ref[c, :, 0:LANES] = stf.astype(BF16)
        stf = stf * dec_ref[c, 0:1, :] + dstf_ref[c]
    for c in reversed(range(nc)):
        stcat_ref[c, :, LANES:] = stb.astype(BF16)
        stb = stb * dec_ref[c, 1:2, :] + dstb_ref[c]

    for c in range(nc):
        rows = slice(c * C, (c + 1) * C)
        o = oacc[rows, :] + _dot_nt(xi_ref[rows, :], stcat_ref[c])
        ms = _head_sums(o * o, bd) * (1.0 / HEAD_DIM)
        y = o * lax.rsqrt(ms + LN_EPS)
        o_ref[rows, :] = (y * _silu(hg_ref[rows, :])).astype(BF16)

    if want_state:
        st_ref[0] = stf
        st_ref[1] = stb


def _hgrn(proj, gate_params, L, layer, s0_t, want_state):
    rows = proj.shape[0]
    nb = rows // L
    nc = L // HG_CHUNK
    blk = lambda col: pl.BlockSpec((L, LANES), lambda p, b, col=col: (b, col + p))
    in_specs = [pl.BlockSpec((None, None, SUBLANES, LANES), lambda p, b: (layer, p, 0, 0)),
                blk(COL_HQ), blk(COL_HFF), blk(COL_HFB), blk(COL_HI), blk(COL_HG)]
    args = [gate_params, proj, proj, proj, proj, proj]
    if s0_t is not None:
        in_specs.append(pl.BlockSpec((None, None, 2, None, LANES, LANES),
                                     lambda p, b: (b, layer, 0, p, 0, 0)))
        args.append(s0_t)
    out_specs = [pl.BlockSpec((L, LANES), lambda p, b: (b, p))]
    out_shape = [jax.ShapeDtypeStruct((rows, HEADS_WIDTH), BF16)]
    if want_state:
        out_specs.append(pl.BlockSpec((None, None, 2, LANES, LANES), lambda p, b: (b, p, 0, 0, 0)))
        out_shape.append(jax.ShapeDtypeStruct((nb, N_PAIRS, 2, LANES, LANES), F32))
    kern = functools.partial(_hgrn_kernel, L=L, has_state=s0_t is not None, want_state=want_state)
    return pl.pallas_call(
        kern,
        grid=(N_PAIRS, nb),
        in_specs=in_specs,
        out_specs=out_specs,
        out_shape=out_shape,
        scratch_shapes=[pltpu.VMEM((L, LANES), F32),
                        pltpu.VMEM((L, 2 * LANES), BF16),
                        pltpu.VMEM((nc, LANES, LANES), F32),
                        pltpu.VMEM((nc, LANES, LANES), F32),
                        pltpu.VMEM((nc, SUBLANES, LANES), F32),
                        pltpu.VMEM((nc, LANES, 2 * LANES), BF16),
                        pltpu.VMEM((2 * HG_CHUNK, HG_CHUNK), jnp.int32)],
        compiler_params=pltpu.CompilerParams(
            dimension_semantics=("arbitrary", "arbitrary"), vmem_limit_bytes=VMEM_LIMIT),
        name="hgrn2",
    )(*args)


def _ffn_kernel(x_ref, of_ref, or_ref, oh_ref, mod_ref, wo_ref, lng_ref, lnb_ref,
                wg_ref, wu_ref, wd_ref, y_ref):
    c1 = FNET_WIDTH
    c2 = FNET_WIDTH + HEADS_WIDTH
    g1 = mod_ref[:, 2 * D_MODEL:3 * D_MODEL]
    sh2 = mod_ref[:, 3 * D_MODEL:4 * D_MODEL]
    sc2 = mod_ref[:, 4 * D_MODEL:5 * D_MODEL]
    g2 = mod_ref[:, 5 * D_MODEL:6 * D_MODEL]
    mix = (_dot(of_ref[...], wo_ref[0:c1, :]) + _dot(or_ref[...], wo_ref[c1:c2, :])
           + _dot(oh_ref[...], wo_ref[c2:, :]))
    x1 = _ln(ALPHA * x_ref[...] + g1 * mix) * lng_ref[0:1, :] + lnb_ref[0:1, :]
    h2 = (_ln(x1) * (1.0 + sc2) + sh2).astype(BF16)
    gate = _dot(h2, wg_ref[...])
    up = _dot(h2, wu_ref[...])
    act = (_silu(gate) * up).astype(BF16)
    ffn = _dot(act, wd_ref[...])
    y_ref[...] = _ln(ALPHA * x1 + g2 * ffn) * lng_ref[1:2, :] + lnb_ref[1:2, :]


def _out_ffn(x2d, o_f, o_r, o_h, mod, w_out_bf, ln_g, ln_b, wg_bf, wu_bf, wd_bf, layer, mod_row):
    rows = x2d.shape[0]
    once = pl.Buffered(1)
    row_blk = lambda w: pl.BlockSpec((ROW_TILE, w), lambda i: (i, 0))
    return pl.pallas_call(
        _ffn_kernel,
        grid=(rows // ROW_TILE,),
        in_specs=[row_blk(D_MODEL), row_blk(FNET_WIDTH), row_blk(HEADS_WIDTH), row_blk(HEADS_WIDTH),
                  pl.BlockSpec((None, None, 1, N_MOD * D_MODEL), lambda i: (layer, mod_row(i), 0, 0)),
                  pl.BlockSpec((None, D_MODEL, D_MODEL), lambda i: (layer, 0, 0), pipeline_mode=once),
                  pl.BlockSpec((None, 2, D_MODEL), lambda i: (layer, 0, 0)),
                  pl.BlockSpec((None, 2, D_MODEL), lambda i: (layer, 0, 0)),
                  pl.BlockSpec((None, D_MODEL, D_FF), lambda i: (layer, 0, 0), pipeline_mode=once),
                  pl.BlockSpec((None, D_MODEL, D_FF), lambda i: (layer, 0, 0), pipeline_mode=once),
                  pl.BlockSpec((None, D_FF, D_MODEL), lambda i: (layer, 0, 0), pipeline_mode=once)],
        out_specs=row_blk(D_MODEL),
        out_shape=jax.ShapeDtypeStruct((rows, D_MODEL), F32),
        compiler_params=pltpu.CompilerParams(
            dimension_semantics=("arbitrary",), vmem_limit_bytes=VMEM_LIMIT),
        name="out_ffn",
    )(x2d, o_f, o_r, o_h, mod, w_out_bf, ln_g, ln_b, wg_bf, wu_bf, wd_bf)


def _pair_blockdiag(s, transpose):
    if transpose:
        s = jnp.swapaxes(s, -1, -2)
    lead = s.shape[:-3]
    s = s.reshape(lead + (N_PAIRS, 2, HEAD_DIM, HEAD_DIM))
    z = jnp.zeros_like(s[..., 0, :, :])
    top = jnp.concatenate([s[..., 0, :, :], z], axis=-1)
    bot = jnp.concatenate([z, s[..., 1, :, :]], axis=-1)
    return jnp.concatenate([top, bot], axis=-2)


def _pair_diag_blocks(s, transpose):
    a = s[..., :HEAD_DIM, :HEAD_DIM]
    b = s[..., HEAD_DIM:, HEAD_DIM:]
    out = jnp.stack([a, b], axis=-3)
    if transpose:
        out = jnp.swapaxes(out, -1, -2)
    return out


def _states_out(st, transpose):
    blocks = _pair_diag_blocks(st, transpose)
    blocks = jnp.transpose(blocks, (0, 2, 1, 3, 4, 5))
    return blocks.reshape(st.shape[0], 2, N_HEADS, HEAD_DIM, HEAD_DIM)


def kernel(x_prompt, x_sample, c, state_ret, state_hgrn, c_ctx, w_mod, b_mod, w_in, w_out,
           ret_log_decay, hg_lower_bound, ln_g, ln_b, w_gate, w_up, w_down):
    B, S, _ = x_prompt.shape
    DB, DS, _ = x_sample.shape

    p = jax.nn.softmax(hg_lower_bound.astype(F32), axis=1)
    cum = jnp.cumsum(p, axis=1)
    lbs = cum - cum[:, :1]
    kinds = jnp.stack([jnp.log(lbs), jnp.log1p(-lbs), 1.0 - lbs], axis=0)
    gp = jnp.transpose(kinds.reshape(3, 2, DEPTH, N_PAIRS, LANES), (2, 3, 0, 1, 4))
    gp = gp.reshape(DEPTH, N_PAIRS, 6, LANES)
    gp = jnp.concatenate([gp, jnp.zeros((DEPTH, N_PAIRS, SUBLANES - 6, LANES), F32)], axis=2)
    lg_flat = (-jnp.exp(ret_log_decay.astype(F32))).reshape(-1)

    w_in_bf = w_in.astype(BF16)
    w_out_bf = w_out.astype(BF16)
    wg_bf = w_gate.astype(BF16)
    wu_bf = w_up.astype(BF16)
    wd_bf = w_down.astype(BF16)

    cv = jnp.concatenate([c_ctx[None, :], c, jnp.zeros((MOD_ROWS - 1 - DB, D_MODEL), F32)], axis=0)
    mod = _modulation(cv, w_mod, b_mod).reshape(DEPTH, MOD_ROWS, 1, N_MOD * D_MODEL)

    sr_bd = _pair_blockdiag(state_ret.astype(F32), transpose=True)
    sh_t = _pair_blockdiag(state_hgrn.astype(F32), transpose=True)
    rope = _rope_tables(DS)

    ctx_row = lambda i: 0
    smp_row = lambda i: 1 + i // (DS // ROW_TILE)

    y = x_prompt.reshape(B * S, D_MODEL)
    z = x_sample.reshape(DB * DS, D_MODEL)
    ret_states, hg_states = [], []
    for l in range(DEPTH):
        proj = _inproj(y, mod, w_in_bf, l, ctx_row)
        o_f = _fnet(proj, S)
        o_r, st_r = _retention(proj, lg_flat, S, l, None, None, True)
        o_h, st_h = _hgrn(proj, gp, S, l, None, True)
        y = _out_ffn(y, o_f, o_r, o_h, mod, w_out_bf, ln_g, ln_b, wg_bf, wu_bf, wd_bf, l, ctx_row)
        ret_states.append(_states_out(st_r, transpose=True))
        hg_states.append(_states_out(st_h, transpose=True))
        proj = _inproj(z, mod, w_in_bf, l, smp_row)
        o_f = _fnet(proj, DS)
        (o_r,) = _retention(proj, lg_flat, DS, l, rope, sr_bd, False)
        (o_h,) = _hgrn(proj, gp, DS, l, sh_t, False)
        z = _out_ffn(z, o_f, o_r, o_h, mod, w_out_bf, ln_g, ln_b, wg_bf, wu_bf, wd_bf, l, smp_row)

    new_state_ret = jnp.stack(ret_states, axis=1)
    new_state_hgrn = jnp.stack(hg_states, axis=1)
    return (y.reshape(B, S, D_MODEL), z.reshape(DB, DS, D_MODEL), new_state_ret, new_state_hgrn)
```

```python
import functools

import numpy as np
import jax
import jax.numpy as jnp
from jax import lax
from jax.experimental import pallas as pl
from jax.experimental.pallas import tpu as pltpu

F32 = jnp.float32
BF16 = jnp.bfloat16

D_MODEL = 1024
DEPTH = 2
GRID_W = 64
FNET_WIDTH = 256
FNET_GDIM = 64
N_HEADS = 6
N_PAIRS = N_HEADS // 2
HEAD_DIM = 64
LANES = 128
SUBLANES = 8
HEADS_WIDTH = N_HEADS * HEAD_DIM
IN_WIDTH = 3712
D_FF = 2816
ROPE_BASE = 10000.0
LN_EPS = 1e-5
LOG2E = 1.4426950408889634
N_MOD = 6
ALPHA = (2 * DEPTH) ** 0.25
MOD_ROWS = 8
ROW_TILE = 512
RET_ROWS = 256
MIX_ROWS = 1024
HG_CHUNK = 128
HG_BLOCK = 32
HG_SAFE_LOG2 = 96.0
VMEM_LIMIT = 56 * 1024 * 1024

COL_RQ, COL_RK, COL_RV, COL_RG = 2, 5, 8, 11
COL_HQ, COL_HFF, COL_HFB, COL_HI, COL_HG = 14, 17, 20, 23, 26


def _silu(x):
    return x / (1.0 + jnp.exp(-x))


def _ln(x):
    mu = jnp.mean(x, axis=-1, keepdims=True)
    xc = x - mu
    var = jnp.mean(xc * xc, axis=-1, keepdims=True)
    return xc * lax.rsqrt(var + LN_EPS)


def _dot(a, b):
    return jnp.dot(a, b, preferred_element_type=F32)


def _dot_nt(a, b):
    return lax.dot_general(a, b, (((1,), (1,)), ((), ())), preferred_element_type=F32)


def _split2(x):
    hi = x.astype(BF16)
    lo = (x - hi.astype(F32)).astype(BF16)
    return hi, lo


def _split3(x):
    x1 = x.astype(BF16)
    r1 = x - x1.astype(F32)
    x2 = r1.astype(BF16)
    x3 = (r1 - x2.astype(F32)).astype(BF16)
    return x1, x2, x3


def _head_sums(x, bd_ones):
    hi, lo = _split2(x)
    return _dot(hi, bd_ones) + _dot(lo, bd_ones)


def _bd_ones():
    r = lax.broadcasted_iota(jnp.int32, (LANES, LANES), 0)
    c = lax.broadcasted_iota(jnp.int32, (LANES, LANES), 1)
    return jnp.where((r < HEAD_DIM) == (c < HEAD_DIM), 1.0, 0.0).astype(BF16)


def _dot3_nt(sel, x):
    return sum(_dot_nt(sel, part) for part in _split3(x))


def _pair_state_in(s_ref, direction):
    z = jnp.zeros((HEAD_DIM, HEAD_DIM), F32)
    r = lax.broadcasted_iota(jnp.int32, (LANES, HEAD_DIM), 0)
    c = lax.broadcasted_iota(jnp.int32, (LANES, HEAD_DIM), 1)
    out = None
    for hh in range(2):
        s = s_ref[direction, hh]
        rows = jnp.concatenate([s, z] if hh == 0 else [z, s], axis=0)
        place = jnp.where(r == c + hh * HEAD_DIM, 1.0, 0.0).astype(BF16)
        term = _dot3_nt(place, rows)
        out = term if out is None else out + term
    return out


def _pair_state_out(st, st_ref, direction):
    r = lax.broadcasted_iota(jnp.int32, (HEAD_DIM, LANES), 0)
    c = lax.broadcasted_iota(jnp.int32, (HEAD_DIM, LANES), 1)
    for hh in range(2):
        pick = jnp.where(c == r + hh * HEAD_DIM, 1.0, 0.0).astype(BF16)
        st_ref[direction, hh] = _dot3_nt(pick, st[hh * HEAD_DIM:(hh + 1) * HEAD_DIM, :])


def _mod_kernel(cv_ref, w_ref, b_ref, o_ref):
    s = _silu(cv_ref[...]).astype(BF16)
    o_ref[...] = _dot(s, w_ref[...].astype(BF16)) + b_ref[...]


def _modulation(cv, w_mod, b_mod):
    tn = 512
    width = N_MOD * D_MODEL
    return pl.pallas_call(
        _mod_kernel,
        grid=(DEPTH, width // tn),
        in_specs=[pl.BlockSpec((MOD_ROWS, D_MODEL), lambda l, j: (0, 0)),
                  pl.BlockSpec((None, D_MODEL, tn), lambda l, j: (l, 0, j)),
                  pl.BlockSpec((None, 1, tn), lambda l, j: (l, 0, j))],
        out_specs=pl.BlockSpec((None, MOD_ROWS, tn), lambda l, j: (l, 0, j)),
        out_shape=jax.ShapeDtypeStruct((DEPTH, MOD_ROWS, width), F32),
        compiler_params=pltpu.CompilerParams(
            dimension_semantics=("arbitrary", "arbitrary"), vmem_limit_bytes=VMEM_LIMIT),
        name="modulation",
    )(cv, w_mod, b_mod.reshape(DEPTH, 1, width))


def _inproj_kernel(x_ref, mod_ref, w_ref, o_ref):
    sh = mod_ref[:, 0:D_MODEL]
    sc = mod_ref[:, D_MODEL:2 * D_MODEL]
    h = _ln(x_ref[...]) * (1.0 + sc) + sh
    o_ref[...] = _dot(h.astype(BF16), w_ref[...])


def _inproj(x2d, mod, w_in_bf, layer, mod_row):
    rows = x2d.shape[0]
    return pl.pallas_call(
        _inproj_kernel,
        grid=(rows // ROW_TILE,),
        in_specs=[pl.BlockSpec((ROW_TILE, D_MODEL), lambda i: (i, 0)),
                  pl.BlockSpec((None, None, 1, N_MOD * D_MODEL), lambda i: (layer, mod_row(i), 0, 0)),
                  pl.BlockSpec((None, D_MODEL, IN_WIDTH), lambda i: (layer, 0, 0))],
        out_specs=pl.BlockSpec((ROW_TILE, IN_WIDTH), lambda i: (i, 0)),
        out_shape=jax.ShapeDtypeStruct((rows, IN_WIDTH), F32),
        compiler_params=pltpu.CompilerParams(
            dimension_semantics=("arbitrary",), vmem_limit_bytes=VMEM_LIMIT),
        name="inproj",
    )(x2d, mod, w_in_bf)


def _fnet_kernel(u_ref, w1_ref, cl_ref, sl_ref, o_ref):
    u = u_ref[...].astype(BF16)
    t = _dot(u, w1_ref[...])
    uc = t[:, :FNET_WIDTH].astype(BF16)
    us = t[:, FNET_WIDTH:].astype(BF16)
    o = _dot(cl_ref[...], uc) - _dot(sl_ref[...], us)
    o_ref[...] = o.astype(BF16)


def _fnet_consts(L):
    scale = 1.0 / np.sqrt(L * FNET_GDIM)
    k = np.arange(FNET_GDIM)
    ang = 2.0 * np.pi * ((k[:, None] * k[None, :]) % FNET_GDIM) / FNET_GDIM
    eye = np.eye(FNET_WIDTH // FNET_GDIM)
    w1 = np.concatenate([np.kron(eye, np.cos(ang)), np.kron(eye, np.sin(ang))], axis=1) * scale
    n = np.arange(L)
    angl = 2.0 * np.pi * ((n[:, None] * n[None, :]) % L) / L
    tables = (w1, np.cos(angl), np.sin(angl))
    return tuple(jnp.asarray(t, F32).astype(BF16) for t in tables)


def _fnet(proj, L):
    rows = proj.shape[0]
    w1, cl, sl = _fnet_consts(L)
    return pl.pallas_call(
        _fnet_kernel,
        grid=(rows // L,),
        in_specs=[pl.BlockSpec((L, FNET_WIDTH), lambda b: (b, 0)),
                  pl.BlockSpec((FNET_WIDTH, 2 * FNET_WIDTH), lambda b: (0, 0)),
                  pl.BlockSpec((L, L), lambda b: (0, 0)),
                  pl.BlockSpec((L, L), lambda b: (0, 0))],
        out_specs=pl.BlockSpec((L, FNET_WIDTH), lambda b: (b, 0)),
        out_shape=jax.ShapeDtypeStruct((rows, FNET_WIDTH), BF16),
        compiler_params=pltpu.CompilerParams(
            dimension_semantics=("arbitrary",), vmem_limit_bytes=VMEM_LIMIT),
        name="fnet",
    )(proj, w1, cl, sl)


def _rope_tables(L):
    half = HEAD_DIM // 2
    inv = ROPE_BASE ** (-np.arange(0, half, 2, dtype=np.float64) / half)
    n = np.arange(L)
    rows, cols = n // GRID_W, n % GRID_W
    lane = np.arange(LANES)
    f = lane % (half // 2)
    use_col = (lane % HEAD_DIM) >= half
    pos = np.where(use_col[None, :], cols[:, None], rows[:, None]).astype(np.float64)
    ang = pos * inv[f][None, :]
    lo = (lane % half) < (half // 2)
    cos = np.cos(ang)
    sin = np.where(lo[None, :], -np.sin(ang), np.sin(ang))
    return jnp.asarray(cos, F32), jnp.asarray(sin, F32)


def _rope(x, cos, sin_signed, lo_mask):
    quarter = HEAD_DIM // 4
    swapped = jnp.where(lo_mask, pltpu.roll(x, LANES - quarter, axis=1), pltpu.roll(x, quarter, axis=1))
    return x * cos + swapped * sin_signed


def _ret_kernel(*refs, L, seq, layer, use_rope, has_state, want_state):
    it = iter(refs)
    lg_ref = next(it)
    q_ref, k_ref, v_ref, g_ref = next(it), next(it), next(it), next(it)
    if use_rope:
        cos_ref, sin_ref = next(it), next(it)
    if has_state:
        s0_ref = next(it)
    o_ref = next(it)
    if want_state:
        st_ref = next(it)
    oacc = next(it)

    p = pl.program_id(0)
    T = RET_ROWS
    nc = L // T
    ncs = seq // T
    lane = lax.broadcasted_iota(jnp.int32, (1, LANES), 1)
    h0 = lane < HEAD_DIM
    bd = _bd_ones()
    bd_mask = bd > 0

    def lg(direction, head):
        return lg_ref[layer * 2 * N_HEADS + direction * N_HEADS + 2 * p + head]

    i = lax.broadcasted_iota(jnp.int32, (T, T), 0)
    j = lax.broadcasted_iota(jnp.int32, (T, T), 1)
    rel = (i - j).astype(F32)
    masks = []
    for hh in range(2):
        arg = jnp.where(rel > 0, rel * lg(0, hh), -rel * lg(1, hh))
        masks.append(jnp.exp(arg) + jnp.where(rel == 0, 1.0, 0.0))
    mcat = jnp.concatenate(masks, axis=0)

    lgf_vec = jnp.where(h0, lg(0, 0), lg(0, 1))
    lgb_vec = jnp.where(h0, lg(1, 0), lg(1, 1))
    pos = lax.broadcasted_iota(jnp.int32, (T, LANES), 0).astype(F32)
    wf = jnp.exp((pos + 1.0) * lgf_vec)
    wb = jnp.exp((float(T) - pos) * lgb_vec)
    zf = jnp.exp((float(T) - 1.0 - pos) * lgf_vec)
    zb = jnp.exp(pos * lgb_vec)
    gf_chunk = jnp.exp(float(T) * lgf_vec)
    gb_chunk = jnp.exp(float(T) * lgb_vec)
    use_states = has_state or ncs > 1
    if use_rope:
        lo_mask = (lane % (HEAD_DIM // 2)) < (HEAD_DIM // 4)

    dstf, dstb, xis = [], [], []
    for c in range(nc):
        rows = slice(c * T, (c + 1) * T)
        qc = q_ref[rows, :]
        kc = k_ref[rows, :] * (HEAD_DIM ** -0.5)
        if use_rope:
            qc = _rope(qc, cos_ref[rows, :], sin_ref[rows, :], lo_mask)
            kc = _rope(kc, cos_ref[rows, :], sin_ref[rows, :], lo_mask)
        vb = v_ref[rows, :].astype(BF16)
        qs = jnp.concatenate([jnp.where(h0, qc, 0.0), jnp.where(h0, 0.0, qc)], axis=0).astype(BF16)
        s = _dot_nt(qs, kc.astype(BF16))
        o2 = _dot((s * mcat).astype(BF16), vb)
        oacc[rows, :] = jnp.where(h0, o2[:T], o2[T:])
        if use_states or want_state:
            vt = v_ref[rows, :].T.astype(BF16)
            dstf.append(jnp.where(bd_mask, _dot(vt, (kc * zf).astype(BF16)), 0.0))
            dstb.append(jnp.where(bd_mask, _dot(vt, (kc * zb).astype(BF16)), 0.0))
        if use_states:
            xis.append(jnp.concatenate([qc * wf, qc * wb], axis=1).astype(BF16))

    stf_in, stb_in = [None] * nc, [None] * nc
    if use_states or want_state:
        for sq in range(nc // ncs):
            chunks = range(sq * ncs, (sq + 1) * ncs)
            if has_state:
                stf, stb = _pair_state_in(s0_ref, 0), _pair_state_in(s0_ref, 1)
            else:
                stf = jnp.zeros((LANES, LANES), F32)
                stb = stf
            for c in chunks:
                stf_in[c] = stf
                stf = stf * gf_chunk + dstf[c]
            for c in reversed(chunks):
                stb_in[c] = stb
                stb = stb * gb_chunk + dstb[c]
            if want_state:
                _pair_state_out(stf, st_ref.at[sq], 0)
                _pair_state_out(stb, st_ref.at[sq], 1)

    for c in range(nc):
        rows = slice(c * T, (c + 1) * T)
        o = oacc[rows, :]
        if use_states:
            st = jnp.concatenate([stf_in[c], stb_in[c]], axis=1).astype(BF16)
            o = o + _dot_nt(xis[c], st)
        mu = _head_sums(o, bd) * (1.0 / HEAD_DIM)
        oc = o - mu
        var = _head_sums(oc * oc, bd) * (1.0 / HEAD_DIM)
        y = oc * lax.rsqrt(var + LN_EPS)
        o_ref[rows, :] = (y * _silu(g_ref[rows, :])).astype(BF16)


def _mixer_specs(proj, cols, seq, layer, s0, want_state):
    rows = proj.shape[0]
    blk = lambda col: pl.BlockSpec((MIX_ROWS, LANES), lambda p, b, col=col: (b, col + p))
    in_specs = [blk(col) for col in cols]
    state_spec = None
    if s0 is not None:
        state_spec = pl.BlockSpec((None, None, 2, 2, HEAD_DIM, HEAD_DIM),
                                  lambda p, b: (b, layer, 0, p, 0, 0))
    out_specs = [pl.BlockSpec((MIX_ROWS, LANES), lambda p, b: (b, p))]
    out_shape = [jax.ShapeDtypeStruct((rows, HEADS_WIDTH), BF16)]
    if want_state:
        out_specs.append(pl.BlockSpec((MIX_ROWS // seq, 2, 2, HEAD_DIM, HEAD_DIM),
                                      lambda p, b: (b, 0, p, 0, 0)))
        out_shape.append(jax.ShapeDtypeStruct((rows // seq, 2, N_HEADS, HEAD_DIM, HEAD_DIM), F32))
    return in_specs, state_spec, out_specs, out_shape


def _retention(proj, lg_flat, seq, layer, rope, s0, want_state):
    rows = proj.shape[0]
    in_specs, state_spec, out_specs, out_shape = _mixer_specs(
        proj, (COL_RQ, COL_RK, COL_RV, COL_RG), seq, layer, s0, want_state)
    in_specs = [pl.BlockSpec(memory_space=pltpu.SMEM)] + in_specs
    args = [lg_flat, proj, proj, proj, proj]
    if rope is not None:
        in_specs += [pl.BlockSpec((MIX_ROWS, LANES), lambda p, b: (0, 0))] * 2
        args += list(rope)
    if s0 is not None:
        in_specs.append(state_spec)
        args.append(s0)
    kern = functools.partial(_ret_kernel, L=MIX_ROWS, seq=seq, layer=layer, use_rope=rope is not None,
                             has_state=s0 is not None, want_state=want_state)
    return pl.pallas_call(
        kern,
        grid=(N_PAIRS, rows // MIX_ROWS),
        in_specs=in_specs,
        out_specs=out_specs,
        out_shape=out_shape,
        scratch_shapes=[pltpu.VMEM((MIX_ROWS, LANES), F32)],
        compiler_params=pltpu.CompilerParams(
            dimension_semantics=("arbitrary", "arbitrary"), vmem_limit_bytes=VMEM_LIMIT),
        name="retention",
    )(*args)


def _hgrn_gate(x, llb, l1mlb, omlb):
    ls = jnp.minimum(x, 0.0) - jnp.log(1.0 + jnp.exp(-jnp.abs(x)))
    t = l1mlb + ls
    logf = jnp.maximum(llb, t) + jnp.log(1.0 + jnp.exp(-jnp.abs(llb - t)))
    kk = omlb / (1.0 + jnp.exp(x))
    return logf, kk


def _hgrn_kernel(*refs, L, seq, has_state, want_state):
    it = iter(refs)
    gp_ref = next(it)
    hq_ref, ff_ref, fb_ref, hi_ref, hg_ref = next(it), next(it), next(it), next(it), next(it)
    if has_state:
        s0_ref = next(it)
    o_ref = next(it)
    if want_state:
        st_ref = next(it)
    oacc, xi_ref, dstf_ref, dstb_ref, dec_ref, stcat_ref, lvl_ref = (next(it) for _ in range(7))
    q_s, kf_s, kb_s, bf_s, pb_s = (next(it) for _ in range(5))

    C = HG_CHUNK
    nc = L // C
    ncs = seq // C
    lane = lax.broadcasted_iota(jnp.int32, (1, LANES), 1)
    h0 = lane < HEAD_DIM
    bd = _bd_ones()
    bd_mask = bd > 0
    rowi = lax.broadcasted_iota(jnp.int32, (C, C), 0)
    coli = lax.broadcasted_iota(jnp.int32, (C, C), 1)
    tri = jnp.where(rowi >= coli, 1.0, 0.0).astype(BF16)
    sub = lax.broadcasted_iota(jnp.int32, (C // SUBLANES, SUBLANES, LANES), 1)
    row2 = lax.broadcasted_iota(jnp.int32, (2 * C, C), 0) % C
    col2 = lax.broadcasted_iota(jnp.int32, (2 * C, C), 1)
    lvl_ref[...] = 31 - lax.clz(row2 ^ col2)

    llb_f, llb_b = gp_ref[0:1, :], gp_ref[1:2, :]
    l1m_f, l1m_b = gp_ref[2:3, :], gp_ref[3:4, :]
    oml_f, oml_b = gp_ref[4:5, :], gp_ref[5:6, :]

    def to3(x):
        return x.reshape(C // SUBLANES, SUBLANES, LANES)

    def prep_body(c, gmax):
        rows = pl.ds(pl.multiple_of(c * C, C), C)
        q = _silu(hq_ref[rows, :])
        v = hi_ref[rows, :]
        lff, kf = _hgrn_gate(ff_ref[rows, :], llb_f, l1m_f, oml_f)
        lfb, kb = _hgrn_gate(fb_ref[rows, :], llb_b, l1m_b, oml_b)
        lf2 = jnp.concatenate([lff, lfb], axis=1) * LOG2E
        x1, x2, x3 = _split3(lf2)
        cs = _dot(tri, x1) + _dot(tri, x2) + _dot(tri, x3)
        bf = cs[:, :LANES]
        bb_incl = cs[:, LANES:]
        pb = bb_incl - lf2[:, LANES:]
        bf_last = bf[C - 1:C, :]
        tot_b = bb_incl[C - 1:C, :]
        q_s[rows, :] = q
        kf_s[rows, :] = kf
        kb_s[rows, :] = kb
        bf_s[rows, :] = bf
        pb_s[rows, :] = pb

        vt = v.T.astype(BF16)
        dstf_ref[c] = jnp.where(bd_mask, _dot(vt, (kf * jnp.exp2(bf_last - bf)).astype(BF16)), 0.0)
        dstb_ref[c] = jnp.where(bd_mask, _dot(vt, (kb * jnp.exp2(pb)).astype(BF16)), 0.0)
        dec_ref[c, 0:1, :] = jnp.exp2(bf_last)
        dec_ref[c, 1:2, :] = jnp.exp2(tot_b)
        xi_ref[rows, 0:LANES] = (q * jnp.exp2(bf)).astype(BF16)
        xi_ref[rows, LANES:] = (q * jnp.exp2(tot_b - pb)).astype(BF16)

        for blk in range(C // HG_BLOCK):
            s, e = blk * HG_BLOCK, (blk + 1) * HG_BLOCK - 1
            gmax = jnp.maximum(gmax, bf[s:s + 1, :] - bf[e:e + 1, :])
            gmax = jnp.maximum(gmax, pb[s:s + 1, :] - pb[e:e + 1, :])
        return gmax

    jm = (lax.broadcasted_iota(jnp.int32, (2 * HG_BLOCK, C), 1)
          - lax.broadcasted_iota(jnp.int32, (2 * HG_BLOCK, C), 0) % HG_BLOCK)

    def fast_body(c, carry):
        rows = pl.ds(pl.multiple_of(c * C, C), C)
        q, kf, kb, bf, pb = q_s[rows, :], kf_s[rows, :], kb_s[rows, :], bf_s[rows, :], pb_s[rows, :]
        vb = hi_ref[rows, :].astype(BF16)
        a_rows = []
        for blk in range(C // HG_BLOCK):
            s, e = blk * HG_BLOCK, (blk + 1) * HG_BLOCK
            rf = bf[e - 1:e, :]
            rb_ = pb[s:s + 1, :]
            qf = q[s:e, :] * jnp.exp2(bf[s:e, :] - rf)
            qb = q[s:e, :] * jnp.exp2(rb_ - pb[s:e, :])
            xf = jnp.concatenate([jnp.where(h0, qf, 0.0), jnp.where(h0, 0.0, qf)], axis=0).astype(BF16)
            xb = jnp.concatenate([jnp.where(h0, qb, 0.0), jnp.where(h0, 0.0, qb)], axis=0).astype(BF16)
            kfv = (kf[:e, :] * jnp.exp2(rf - bf[:e, :])).astype(BF16)
            kbv = (kb[s:, :] * jnp.exp2(pb[s:, :] - rb_)).astype(BF16)
            if e < C:
                kfv = jnp.concatenate([kfv, jnp.zeros((C - e, LANES), BF16)], axis=0)
            if s > 0:
                kbv = jnp.concatenate([jnp.zeros((s, LANES), BF16), kbv], axis=0)
            af = _dot_nt(xf, kfv)
            ab = _dot_nt(xb, kbv)
            a_rows.append(jnp.where(jm < s, af, jnp.where(jm > s, ab, af + ab)))
        o2 = _dot(jnp.concatenate(a_rows, axis=0).astype(BF16), vb)
        o_blocks = []
        for blk in range(C // HG_BLOCK):
            base = blk * 2 * HG_BLOCK
            o_blocks.append(jnp.where(h0, o2[base:base + HG_BLOCK], o2[base + HG_BLOCK:base + 2 * HG_BLOCK]))
        oacc[rows, :] = jnp.concatenate(o_blocks, axis=0)
        return carry

    def exact_body(c, carry):
        rows = pl.ds(pl.multiple_of(c * C, C), C)
        q, kf, kb, bf, pb = q_s[rows, :], kf_s[rows, :], kb_s[rows, :], bf_s[rows, :], pb_s[rows, :]
        v = hi_ref[rows, :]
        vb = v.astype(BF16)

        d0 = (q * (kf + kb)).astype(BF16)
        o = _dot(d0, bd) * v
        q3, v3, bf3, pb3, kf3, kb3 = to3(q), to3(v), to3(bf), to3(pb), to3(kf), to3(kb)
        nbf3, npb3 = -bf3, -pb3
        for dl in range(1, SUBLANES):
            src_fw = sub < (SUBLANES - dl)
            u = jnp.where(src_fw, nbf3, pb3)
            kmix = jnp.where(src_fw, kf3, kb3)
            w = jnp.where(sub >= dl, bf3, npb3)
            rel = w + pltpu.roll(u, dl, axis=1)
            d = (q3 * pltpu.roll(kmix, dl, axis=1) * jnp.exp2(rel)).reshape(C, LANES).astype(BF16)
            o = o + _dot(d, bd) * pltpu.roll(v3, dl, axis=1).reshape(C, LANES)

        lvl = lvl_ref[...]
        a_tot = jnp.zeros((2 * C, C), F32)
        m = SUBLANES
        while m < C:
            nblk = C // (2 * m)

            def halves(x):
                x4 = x.reshape(nblk, 2, m, LANES)
                return x4[:, 0], x4[:, 1]

            def join(lo, up):
                return jnp.stack([lo, up], axis=1).reshape(C, LANES)

            q_lo, q_up = halves(q)
            kf_lo, _ = halves(kf)
            _, kb_up = halves(kb)
            bf_lo, bf_up = halves(bf)
            pb_lo, pb_up = halves(pb)
            rf = bf_lo[:, m - 1:m, :]
            rb_ = pb_up[:, 0:1, :]
            xq = join(q_lo * jnp.exp2(rb_ - pb_lo), q_up * jnp.exp2(bf_up - rf))
            yk = join(kf_lo * jnp.exp2(rf - bf_lo), kb_up * jnp.exp2(pb_up - rb_))
            xs = jnp.concatenate([jnp.where(h0, xq, 0.0), jnp.where(h0, 0.0, xq)], axis=0).astype(BF16)
            a = _dot_nt(xs, yk.astype(BF16))
            a_tot = jnp.where(lvl == (m.bit_length() - 1), a, a_tot)
            m *= 2
        o2 = _dot(a_tot.astype(BF16), vb)
        o = o + jnp.where(h0, o2[:C], o2[C:])
        oacc[rows, :] = o
        return carry

    gmax = lax.fori_loop(0, nc, prep_body, jnp.zeros((1, LANES), F32), unroll=2)
    safe = jnp.max(gmax) < HG_SAFE_LOG2

    @pl.when(safe)
    def _():
        lax.fori_loop(0, nc, fast_body, 0, unroll=2)

    @pl.when(jnp.logical_not(safe))
    def _():
        lax.fori_loop(0, nc, exact_body, 0)

    for sq in range(nc // ncs):
        chunks = range(sq * ncs, (sq + 1) * ncs)
        if has_state:
            stf, stb = _pair_state_in(s0_ref, 0), _pair_state_in(s0_ref, 1)
        else:
            stf = jnp.zeros((LANES, LANES), F32)
            stb = stf
        for c in chunks:
            stcat_ref[c, :, 0:LANES] = stf.astype(BF16)
            stf = stf * dec_ref[c, 0:1, :] + dstf_ref[c]
        for c in reversed(chunks):
            stcat_ref[c, :, LANES:] = stb.astype(BF16)
            stb = stb * dec_ref[c, 1:2, :] + dstb_ref[c]
        if want_state:
            _pair_state_out(stf, st_ref.at[sq], 0)
            _pair_state_out(stb, st_ref.at[sq], 1)

    for c in range(nc):
        rows = slice(c * C, (c + 1) * C)
        o = oacc[rows, :] + _dot_nt(xi_ref[rows, :], stcat_ref[c])
        ms = _head_sums(o * o, bd) * (1.0 / HEAD_DIM)
        y = o * lax.rsqrt(ms + LN_EPS)
        o_ref[rows, :] = (y * _silu(hg_ref[rows, :])).astype(BF16)


def _hgrn(proj, gate_params, seq, layer, s0, want_state):
    rows = proj.shape[0]
    L = MIX_ROWS
    nc = L // HG_CHUNK
    in_specs, state_spec, out_specs, out_shape = _mixer_specs(
        proj, (COL_HQ, COL_HFF, COL_HFB, COL_HI, COL_HG), seq, layer, s0, want_state)
    in_specs = [pl.BlockSpec((None, None, SUBLANES, LANES), lambda p, b: (layer, p, 0, 0))] + in_specs
    args = [gate_params, proj, proj, proj, proj, proj]
    if s0 is not None:
        in_specs.append(state_spec)
        args.append(s0)
    kern = functools.partial(_hgrn_kernel, L=L, seq=seq, has_state=s0 is not None, want_state=want_state)
    return pl.pallas_call(
        kern,
        grid=(N_PAIRS, rows // L),
        in_specs=in_specs,
        out_specs=out_specs,
        out_shape=out_shape,
        scratch_shapes=[pltpu.VMEM((L, LANES), F32),
                        pltpu.VMEM((L, 2 * LANES), BF16),
                        pltpu.VMEM((nc, LANES, LANES), F32),
                        pltpu.VMEM((nc, LANES, LANES), F32),
                        pltpu.VMEM((nc, SUBLANES, LANES), F32),
                        pltpu.VMEM((nc, LANES, 2 * LANES), BF16),
                        pltpu.VMEM((2 * HG_CHUNK, HG_CHUNK), jnp.int32)]
                       + [pltpu.VMEM((L, LANES), F32)] * 5,
        compiler_params=pltpu.CompilerParams(
            dimension_semantics=("arbitrary", "arbitrary"), vmem_limit_bytes=VMEM_LIMIT),
        name="hgrn2",
    )(*args)


def _ffn_kernel(x_ref, of_ref, or_ref, oh_ref, mod_ref, wo_ref, lng_ref, lnb_ref,
                wg_ref, wu_ref, wd_ref, y_ref):
    c1 = FNET_WIDTH
    c2 = FNET_WIDTH + HEADS_WIDTH
    g1 = mod_ref[:, 2 * D_MODEL:3 * D_MODEL]
    sh2 = mod_ref[:, 3 * D_MODEL:4 * D_MODEL]
    sc2 = mod_ref[:, 4 * D_MODEL:5 * D_MODEL]
    g2 = mod_ref[:, 5 * D_MODEL:6 * D_MODEL]
    mix = (_dot(of_ref[...], wo_ref[0:c1, :]) + _dot(or_ref[...], wo_ref[c1:c2, :])
           + _dot(oh_ref[...], wo_ref[c2:, :]))
    x1 = _ln(ALPHA * x_ref[...] + g1 * mix) * lng_ref[0:1, :] + lnb_ref[0:1, :]
    h2 = (_ln(x1) * (1.0 + sc2) + sh2).astype(BF16)
    gate = _dot(h2, wg_ref[...])
    up = _dot(h2, wu_ref[...])
    act = (_silu(gate) * up).astype(BF16)
    ffn = _dot(act, wd_ref[...])
    y_ref[...] = _ln(ALPHA * x1 + g2 * ffn) * lng_ref[1:2, :] + lnb_ref[1:2, :]


def _out_ffn(x2d, o_f, o_r, o_h, mod, w_out_bf, ln_g, ln_b, wg_bf, wu_bf, wd_bf, layer, mod_row):
    rows = x2d.shape[0]
    once = pl.Buffered(1)
    row_blk = lambda w: pl.BlockSpec((ROW_TILE, w), lambda i: (i, 0))
    return pl.pallas_call(
        _ffn_kernel,
        grid=(rows // ROW_TILE,),
        in_specs=[row_blk(D_MODEL), row_blk(FNET_WIDTH), row_blk(HEADS_WIDTH), row_blk(HEADS_WIDTH),
                  pl.BlockSpec((None, None, 1, N_MOD * D_MODEL), lambda i: (layer, mod_row(i), 0, 0)),
                  pl.BlockSpec((None, D_MODEL, D_MODEL), lambda i: (layer, 0, 0), pipeline_mode=once),
                  pl.BlockSpec((None, 2, D_MODEL), lambda i: (layer, 0, 0)),
                  pl.BlockSpec((None, 2, D_MODEL), lambda i: (layer, 0, 0)),
                  pl.BlockSpec((None, D_MODEL, D_FF), lambda i: (layer, 0, 0), pipeline_mode=once),
                  pl.BlockSpec((None, D_MODEL, D_FF), lambda i: (layer, 0, 0), pipeline_mode=once),
                  pl.BlockSpec((None, D_FF, D_MODEL), lambda i: (layer, 0, 0), pipeline_mode=once)],
        out_specs=row_blk(D_MODEL),
        out_shape=jax.ShapeDtypeStruct((rows, D_MODEL), F32),
        compiler_params=pltpu.CompilerParams(
            dimension_semantics=("arbitrary",), vmem_limit_bytes=VMEM_LIMIT),
        name="out_ffn",
    )(x2d, o_f, o_r, o_h, mod, w_out_bf, ln_g, ln_b, wg_bf, wu_bf, wd_bf)


def kernel(x_prompt, x_sample, c, state_ret, state_hgrn, c_ctx, w_mod, b_mod, w_in, w_out,
           ret_log_decay, hg_lower_bound, ln_g, ln_b, w_gate, w_up, w_down):
    B, S, _ = x_prompt.shape
    DB, DS, _ = x_sample.shape

    p = jax.nn.softmax(hg_lower_bound.astype(F32), axis=1)
    cum = jnp.cumsum(p, axis=1)
    lbs = cum - cum[:, :1]
    kinds = jnp.stack([jnp.log(lbs), jnp.log1p(-lbs), 1.0 - lbs], axis=0)
    gp = jnp.transpose(kinds.reshape(3, 2, DEPTH, N_PAIRS, LANES), (2, 3, 0, 1, 4))
    gp = gp.reshape(DEPTH, N_PAIRS, 6, LANES)
    gp = jnp.concatenate([gp, jnp.zeros((DEPTH, N_PAIRS, SUBLANES - 6, LANES), F32)], axis=2)
    lg_flat = (-jnp.exp(ret_log_decay.astype(F32))).reshape(-1)

    w_in_bf = w_in.astype(BF16)
    w_out_bf = w_out.astype(BF16)
    wg_bf = w_gate.astype(BF16)
    wu_bf = w_up.astype(BF16)
    wd_bf = w_down.astype(BF16)

    cv = jnp.concatenate([c_ctx[None, :], c, jnp.zeros((MOD_ROWS - 1 - DB, D_MODEL), F32)], axis=0)
    mod = _modulation(cv, w_mod, b_mod).reshape(DEPTH, MOD_ROWS, 1, N_MOD * D_MODEL)

    rope = _rope_tables(DS)

    ctx_row = lambda i: 0
    smp_row = lambda i: 1 + i // (DS // ROW_TILE)

    y = x_prompt.reshape(B * S, D_MODEL)
    z = x_sample.reshape(DB * DS, D_MODEL)
    ret_states, hg_states = [], []
    for l in range(DEPTH):
        proj = _inproj(y, mod, w_in_bf, l, ctx_row)
        o_f = _fnet(proj, S)
        o_r, st_r = _retention(proj, lg_flat, S, l, None, None, True)
        o_h, st_h = _hgrn(proj, gp, S, l, None, True)
        y = _out_ffn(y, o_f, o_r, o_h, mod, w_out_bf, ln_g, ln_b, wg_bf, wu_bf, wd_bf, l, ctx_row)
        ret_states.append(st_r)
        hg_states.append(st_h)
        proj = _inproj(z, mod, w_in_bf, l, smp_row)
        o_f = _fnet(proj, DS)
        (o_r,) = _retention(proj, lg_flat, DS, l, rope, state_ret, False)
        (o_h,) = _hgrn(proj, gp, DS, l, state_hgrn, False)
        z = _out_ffn(z, o_f, o_r, o_h, mod, w_out_bf, ln_g, ln_b, wg_bf, wu_bf, wd_bf, l, smp_row)

    new_state_ret = jnp.stack(ret_states, axis=1)
    new_state_hgrn = jnp.stack(hg_states, axis=1)
    return (y.reshape(B, S, D_MODEL), z.reshape(DB, DS, D_MODEL), new_state_ret, new_state_hgrn)
```

```python
import functools

import numpy as np
import jax
import jax.numpy as jnp
from jax import lax
from jax.experimental import pallas as pl
from jax.experimental.pallas import tpu as pltpu

F32 = jnp.float32
BF16 = jnp.bfloat16

D_MODEL = 1024
DEPTH = 2
GRID_W = 64
FNET_WIDTH = 256
FNET_GDIM = 64
N_HEADS = 6
N_PAIRS = N_HEADS // 2
HEAD_DIM = 64
LANES = 128
SUBLANES = 8
HEADS_WIDTH = N_HEADS * HEAD_DIM
IN_WIDTH = 3712
D_FF = 2816
ROPE_BASE = 10000.0
LN_EPS = 1e-5
LOG2E = 1.4426950408889634
N_MOD = 6
ALPHA = (2 * DEPTH) ** 0.25
MOD_ROWS = 8
ROW_TILE = 512
RET_ROWS = 256
MIX_ROWS = 1024
HG_CHUNK = 128
HG_GROUP = 8
HG_BLOCK = 32
HG_SAFE_LOG2 = 96.0
VMEM_LIMIT = 56 * 1024 * 1024

COL_RQ, COL_RK, COL_RV, COL_RG = 2, 5, 8, 11
COL_HQ, COL_HFF, COL_HFB, COL_HI, COL_HG = 14, 17, 20, 23, 26


def _silu(x):
    return x / (1.0 + jnp.exp(-x))


def _ln(x):
    mu = jnp.mean(x, axis=-1, keepdims=True)
    xc = x - mu
    var = jnp.mean(xc * xc, axis=-1, keepdims=True)
    return xc * lax.rsqrt(var + LN_EPS)


def _dot(a, b):
    return jnp.dot(a, b, preferred_element_type=F32)


def _dot_nt(a, b):
    return lax.dot_general(a, b, (((1,), (1,)), ((), ())), preferred_element_type=F32)


def _split2(x):
    hi = x.astype(BF16)
    lo = (x - hi.astype(F32)).astype(BF16)
    return hi, lo


def _split3(x):
    x1 = x.astype(BF16)
    r1 = x - x1.astype(F32)
    x2 = r1.astype(BF16)
    x3 = (r1 - x2.astype(F32)).astype(BF16)
    return x1, x2, x3


def _head_sums(x, bd_ones):
    hi, lo = _split2(x)
    return _dot(hi, bd_ones) + _dot(lo, bd_ones)


def _head_sums_sq(x, bd_ones):
    return _dot((x * x).astype(BF16), bd_ones)


def _bd_ones():
    r = lax.broadcasted_iota(jnp.int32, (LANES, LANES), 0)
    c = lax.broadcasted_iota(jnp.int32, (LANES, LANES), 1)
    return jnp.where((r < HEAD_DIM) == (c < HEAD_DIM), 1.0, 0.0).astype(BF16)


def _dot3_nt(sel, x):
    return sum(_dot_nt(sel, part) for part in _split3(x))


def _pair_state_in(s_ref, direction):
    z = jnp.zeros((HEAD_DIM, HEAD_DIM), F32)
    r = lax.broadcasted_iota(jnp.int32, (LANES, HEAD_DIM), 0)
    c = lax.broadcasted_iota(jnp.int32, (LANES, HEAD_DIM), 1)
    out = None
    for hh in range(2):
        s = s_ref[direction, hh]
        rows = jnp.concatenate([s, z] if hh == 0 else [z, s], axis=0)
        place = jnp.where(r == c + hh * HEAD_DIM, 1.0, 0.0).astype(BF16)
        term = _dot3_nt(place, rows)
        out = term if out is None else out + term
    return out


def _state_out_view(st_ref, layer, carried):
    if carried:
        return lambda sq: st_ref.at[sq]
    for other in range(DEPTH):
        if other != layer:
            st_ref[:, other] = jnp.zeros(st_ref.shape[:1] + st_ref.shape[2:], F32)
    return lambda sq: st_ref.at[sq, layer]


def _pair_state_out(st, st_ref, direction):
    r = lax.broadcasted_iota(jnp.int32, (HEAD_DIM, LANES), 0)
    c = lax.broadcasted_iota(jnp.int32, (HEAD_DIM, LANES), 1)
    for hh in range(2):
        pick = jnp.where(c == r + hh * HEAD_DIM, 1.0, 0.0).astype(BF16)
        st_ref[direction, hh] = _dot3_nt(pick, st[hh * HEAD_DIM:(hh + 1) * HEAD_DIM, :])


def _mod_kernel(cv_ref, w_ref, b_ref, o_ref):
    s = _silu(cv_ref[...]).astype(BF16)
    o_ref[...] = _dot(s, w_ref[...].astype(BF16)) + b_ref[...]


def _modulation(cv, w_mod, b_mod):
    tn = 1024
    width = N_MOD * D_MODEL
    return pl.pallas_call(
        _mod_kernel,
        grid=(DEPTH, width // tn),
        in_specs=[pl.BlockSpec((MOD_ROWS, D_MODEL), lambda l, j: (0, 0)),
                  pl.BlockSpec((None, D_MODEL, tn), lambda l, j: (l, 0, j)),
                  pl.BlockSpec((None, 1, tn), lambda l, j: (l, 0, j))],
        out_specs=pl.BlockSpec((None, MOD_ROWS, tn), lambda l, j: (l, 0, j)),
        out_shape=jax.ShapeDtypeStruct((DEPTH, MOD_ROWS, width), F32),
        compiler_params=pltpu.CompilerParams(
            dimension_semantics=("arbitrary", "arbitrary"), vmem_limit_bytes=VMEM_LIMIT),
        name="modulation",
    )(cv, w_mod, b_mod.reshape(DEPTH, 1, width))


def _inproj_kernel(x_ref, mod_ref, w_ref, o_ref):
    sh = mod_ref[:, 0:D_MODEL]
    sc = mod_ref[:, D_MODEL:2 * D_MODEL]
    h = _ln(x_ref[...]) * (1.0 + sc) + sh
    o_ref[...] = _dot(h.astype(BF16), w_ref[...])


def _inproj(x2d, mod, w_in_bf, layer, mod_row):
    rows = x2d.shape[0]
    return pl.pallas_call(
        _inproj_kernel,
        grid=(rows // ROW_TILE,),
        in_specs=[pl.BlockSpec((ROW_TILE, D_MODEL), lambda i: (i, 0)),
                  pl.BlockSpec((None, None, 1, N_MOD * D_MODEL), lambda i: (layer, mod_row(i), 0, 0)),
                  pl.BlockSpec((None, D_MODEL, IN_WIDTH), lambda i: (layer, 0, 0))],
        out_specs=pl.BlockSpec((ROW_TILE, IN_WIDTH), lambda i: (i, 0)),
        out_shape=jax.ShapeDtypeStruct((rows, IN_WIDTH), F32),
        compiler_params=pltpu.CompilerParams(
            dimension_semantics=("arbitrary",), vmem_limit_bytes=VMEM_LIMIT),
        name="inproj",
    )(x2d, mod, w_in_bf)


def _fnet_kernel(u_ref, w1_ref, cl_ref, sl_ref, o_ref, *, seq):
    u = u_ref[...].astype(BF16)
    t = _dot(u, w1_ref[...])
    uc = t[:, :FNET_WIDTH].astype(BF16)
    us = t[:, FNET_WIDTH:].astype(BF16)
    for s in range(u.shape[0] // seq):
        rows = slice(s * seq, (s + 1) * seq)
        o = _dot(cl_ref[...], uc[rows, :]) - _dot(sl_ref[...], us[rows, :])
        o_ref[rows, :] = o.astype(BF16)


def _fnet_consts(L):
    scale = 1.0 / np.sqrt(L * FNET_GDIM)
    k = np.arange(FNET_GDIM)
    ang = 2.0 * np.pi * ((k[:, None] * k[None, :]) % FNET_GDIM) / FNET_GDIM
    eye = np.eye(FNET_WIDTH // FNET_GDIM)
    w1 = np.concatenate([np.kron(eye, np.cos(ang)), np.kron(eye, np.sin(ang))], axis=1) * scale
    n = np.arange(L)
    angl = 2.0 * np.pi * ((n[:, None] * n[None, :]) % L) / L
    tables = (w1, np.cos(angl), np.sin(angl))
    return tuple(jnp.asarray(t, F32).astype(BF16) for t in tables)


def _fnet(proj, L):
    rows = proj.shape[0]
    w1, cl, sl = _fnet_consts(L)
    return pl.pallas_call(
        functools.partial(_fnet_kernel, seq=L),
        grid=(rows // MIX_ROWS,),
        in_specs=[pl.BlockSpec((MIX_ROWS, FNET_WIDTH), lambda b: (b, 0)),
                  pl.BlockSpec((FNET_WIDTH, 2 * FNET_WIDTH), lambda b: (0, 0)),
                  pl.BlockSpec((L, L), lambda b: (0, 0)),
                  pl.BlockSpec((L, L), lambda b: (0, 0))],
        out_specs=pl.BlockSpec((MIX_ROWS, FNET_WIDTH), lambda b: (b, 0)),
        out_shape=jax.ShapeDtypeStruct((rows, FNET_WIDTH), BF16),
        compiler_params=pltpu.CompilerParams(
            dimension_semantics=("arbitrary",), vmem_limit_bytes=VMEM_LIMIT),
        name="fnet",
    )(proj, w1, cl, sl)


def _rope_tables(L):
    half = HEAD_DIM // 2
    inv = ROPE_BASE ** (-np.arange(0, half, 2, dtype=np.float64) / half)
    n = np.arange(L)
    rows, cols = n // GRID_W, n % GRID_W
    lane = np.arange(LANES)
    f = lane % (half // 2)
    use_col = (lane % HEAD_DIM) >= half
    pos = np.where(use_col[None, :], cols[:, None], rows[:, None]).astype(np.float64)
    ang = pos * inv[f][None, :]
    lo = (lane % half) < (half // 2)
    cos = np.cos(ang)
    sin = np.where(lo[None, :], -np.sin(ang), np.sin(ang))
    return jnp.asarray(cos, F32), jnp.asarray(sin, F32)


def _rope(x, cos, sin_signed, lo_mask):
    quarter = HEAD_DIM // 4
    swapped = jnp.where(lo_mask, pltpu.roll(x, LANES - quarter, axis=1), pltpu.roll(x, quarter, axis=1))
    return x * cos + swapped * sin_signed


def _ret_kernel(*refs, L, seq, layer, use_rope, has_state, want_state, carried):
    it = iter(refs)
    lg_ref = next(it)
    q_ref, k_ref, v_ref, g_ref = next(it), next(it), next(it), next(it)
    if use_rope:
        cos_ref, sin_ref = next(it), next(it)
    if has_state:
        s0_ref = next(it)
    if carried:
        next(it)
    o_ref = next(it)
    if want_state:
        st_view = _state_out_view(next(it), layer, carried)
    oacc = next(it)

    p = pl.program_id(0)
    T = RET_ROWS
    nc = L // T
    ncs = seq // T
    lane = lax.broadcasted_iota(jnp.int32, (1, LANES), 1)
    h0 = lane < HEAD_DIM
    bd = _bd_ones()
    bd_mask = bd > 0

    def lg(direction, head):
        return lg_ref[layer * 2 * N_HEADS + direction * N_HEADS + 2 * p + head]

    i = lax.broadcasted_iota(jnp.int32, (T, T), 0)
    j = lax.broadcasted_iota(jnp.int32, (T, T), 1)
    rel = (i - j).astype(F32)
    masks = []
    for hh in range(2):
        arg = jnp.where(rel > 0, rel * lg(0, hh), -rel * lg(1, hh))
        masks.append(jnp.exp(arg) + jnp.where(rel == 0, 1.0, 0.0))
    mcat = jnp.concatenate(masks, axis=0)

    lgf_vec = jnp.where(h0, lg(0, 0), lg(0, 1))
    lgb_vec = jnp.where(h0, lg(1, 0), lg(1, 1))
    pos = lax.broadcasted_iota(jnp.int32, (T, LANES), 0).astype(F32)
    wf = jnp.exp((pos + 1.0) * lgf_vec)
    wb = jnp.exp((float(T) - pos) * lgb_vec)
    zf = jnp.exp((float(T) - 1.0 - pos) * lgf_vec)
    zb = jnp.exp(pos * lgb_vec)
    gf_chunk = jnp.exp(float(T) * lgf_vec)
    gb_chunk = jnp.exp(float(T) * lgb_vec)
    use_states = has_state or ncs > 1
    if use_rope:
        lo_mask = (lane % (HEAD_DIM // 2)) < (HEAD_DIM // 4)

    chunk_rows = [slice(c * T, (c + 1) * T) for c in range(nc)]
    qk = []
    for rows in chunk_rows:
        qc = q_ref[rows, :]
        kc = k_ref[rows, :] * (HEAD_DIM ** -0.5)
        if use_rope:
            qc = _rope(qc, cos_ref[rows, :], sin_ref[rows, :], lo_mask)
            kc = _rope(kc, cos_ref[rows, :], sin_ref[rows, :], lo_mask)
        qk.append((qc, kc))
    scores = []
    for qc, kc in qk:
        qs = jnp.concatenate([jnp.where(h0, qc, 0.0), jnp.where(h0, 0.0, qc)], axis=0).astype(BF16)
        scores.append(_dot_nt(qs, kc.astype(BF16)))
    dstf, dstb, xis = [], [], []
    if use_states or want_state:
        for rows, (qc, kc) in zip(chunk_rows, qk):
            vt = v_ref[rows, :].T.astype(BF16)
            dstf.append(jnp.where(bd_mask, _dot(vt, (kc * zf).astype(BF16)), 0.0))
            dstb.append(jnp.where(bd_mask, _dot(vt, (kc * zb).astype(BF16)), 0.0))
    if use_states:
        for qc, kc in qk:
            xis.append(jnp.concatenate([qc * wf, qc * wb], axis=1).astype(BF16))
    probs = [(s * mcat).astype(BF16) for s in scores]
    outs = [_dot(pm, v_ref[rows, :].astype(BF16)) for pm, rows in zip(probs, chunk_rows)]
    for rows, o2 in zip(chunk_rows, outs):
        oacc[rows, :] = jnp.where(h0, o2[:T], o2[T:])

    stf_in, stb_in = [None] * nc, [None] * nc
    if use_states or want_state:
        for sq in range(nc // ncs):
            chunks = range(sq * ncs, (sq + 1) * ncs)
            if has_state:
                stf, stb = _pair_state_in(s0_ref, 0), _pair_state_in(s0_ref, 1)
            else:
                stf = jnp.zeros((LANES, LANES), F32)
                stb = stf
            for c in chunks:
                stf_in[c] = stf
                stf = stf * gf_chunk + dstf[c]
            for c in reversed(chunks):
                stb_in[c] = stb
                stb = stb * gb_chunk + dstb[c]
            if want_state:
                _pair_state_out(stf, st_view(sq), 0)
                _pair_state_out(stb, st_view(sq), 1)

    outs = [oacc[rows, :] for rows in chunk_rows]
    if use_states:
        inter = [_dot_nt(xis[c], jnp.concatenate([stf_in[c], stb_in[c]], axis=1).astype(BF16))
                 for c in range(nc)]
        outs = [o + x for o, x in zip(outs, inter)]
    means = [_head_sums(o, bd) * (1.0 / HEAD_DIM) for o in outs]
    cent = [o - mu for o, mu in zip(outs, means)]
    variances = [_head_sums_sq(oc, bd) * (1.0 / HEAD_DIM) for oc in cent]
    for rows, oc, var in zip(chunk_rows, cent, variances):
        y = oc * lax.rsqrt(var + LN_EPS)
        o_ref[rows, :] = (y * _silu(g_ref[rows, :])).astype(BF16)


def _mixer_specs(proj, cols, seq, layer, s0, want_state, carried):
    rows = proj.shape[0]
    blk = lambda col: pl.BlockSpec((MIX_ROWS, LANES), lambda p, b, col=col: (b, col + p))
    in_specs = [blk(col) for col in cols]
    state_spec = None
    if s0 is not None:
        state_spec = pl.BlockSpec((None, None, 2, 2, HEAD_DIM, HEAD_DIM),
                                  lambda p, b: (b, layer, 0, p, 0, 0))
    out_specs = [pl.BlockSpec((MIX_ROWS, LANES), lambda p, b: (b, p))]
    out_shape = [jax.ShapeDtypeStruct((rows, HEADS_WIDTH), BF16)]
    if want_state:
        if carried is None:
            out_specs.append(pl.BlockSpec((MIX_ROWS // seq, DEPTH, 2, 2, HEAD_DIM, HEAD_DIM),
                                          lambda p, b: (b, 0, 0, p, 0, 0)))
        else:
            out_specs.append(pl.BlockSpec((MIX_ROWS // seq, None, 2, 2, HEAD_DIM, HEAD_DIM),
                                          lambda p, b: (b, layer, 0, p, 0, 0)))
        out_shape.append(jax.ShapeDtypeStruct((rows // seq, DEPTH, 2, N_HEADS, HEAD_DIM, HEAD_DIM), F32))
    return in_specs, state_spec, out_specs, out_shape


def _carry_states(in_specs, args, carried):
    if carried is None:
        return {}
    in_specs.append(pl.BlockSpec(memory_space=pl.ANY))
    args.append(carried)
    return {len(args) - 1: 1}


def _retention(proj, lg_flat, seq, layer, rope, s0, want_state, carried=None):
    rows = proj.shape[0]
    in_specs, state_spec, out_specs, out_shape = _mixer_specs(
        proj, (COL_RQ, COL_RK, COL_RV, COL_RG), seq, layer, s0, want_state, carried)
    in_specs = [pl.BlockSpec(memory_space=pltpu.SMEM)] + in_specs
    args = [lg_flat, proj, proj, proj, proj]
    if rope is not None:
        in_specs += [pl.BlockSpec((MIX_ROWS, LANES), lambda p, b: (0, 0))] * 2
        args += list(rope)
    if s0 is not None:
        in_specs.append(state_spec)
        args.append(s0)
    aliases = _carry_states(in_specs, args, carried)
    kern = functools.partial(_ret_kernel, L=MIX_ROWS, seq=seq, layer=layer, use_rope=rope is not None,
                             has_state=s0 is not None, want_state=want_state, carried=carried is not None)
    return pl.pallas_call(
        kern,
        grid=(N_PAIRS, rows // MIX_ROWS),
        in_specs=in_specs,
        out_specs=out_specs,
        out_shape=out_shape,
        input_output_aliases=aliases,
        scratch_shapes=[pltpu.VMEM((MIX_ROWS, LANES), F32)],
        compiler_params=pltpu.CompilerParams(
            dimension_semantics=("arbitrary", "arbitrary"), vmem_limit_bytes=VMEM_LIMIT),
        name="retention",
    )(*args)


def _hgrn_gate(x, llb, l1mlb, omlb):
    ls = jnp.minimum(x, 0.0) - jnp.log(1.0 + jnp.exp(-jnp.abs(x)))
    t = l1mlb + ls
    logf = jnp.maximum(llb, t) + jnp.log(1.0 + jnp.exp(-jnp.abs(llb - t)))
    kk = omlb / (1.0 + jnp.exp(x))
    return logf, kk


def _hgrn_kernel(*refs, L, seq, layer, has_state, want_state, carried):
    it = iter(refs)
    gp_ref = next(it)
    hq_ref, ff_ref, fb_ref, hi_ref, hg_ref = next(it), next(it), next(it), next(it), next(it)
    if has_state:
        s0_ref = next(it)
    if carried:
        next(it)
    o_ref = next(it)
    if want_state:
        st_view = _state_out_view(next(it), layer, carried)
    oacc, xi_ref, dstf_ref, dstb_ref, dec_ref, stcat_ref, lvl_ref = (next(it) for _ in range(7))
    q_s, kf_s, kb_s, bf_s, pb_s = (next(it) for _ in range(5))

    C = HG_CHUNK
    nc = L // C
    ncs = seq // C
    lane = lax.broadcasted_iota(jnp.int32, (1, LANES), 1)
    h0 = lane < HEAD_DIM
    bd = _bd_ones()
    bd_mask = bd > 0
    rowi = lax.broadcasted_iota(jnp.int32, (C, C), 0)
    coli = lax.broadcasted_iota(jnp.int32, (C, C), 1)
    tri = jnp.where(rowi >= coli, 1.0, 0.0).astype(BF16)
    sub = lax.broadcasted_iota(jnp.int32, (C // SUBLANES, SUBLANES, LANES), 1)
    row2 = lax.broadcasted_iota(jnp.int32, (2 * C, C), 0) % C
    col2 = lax.broadcasted_iota(jnp.int32, (2 * C, C), 1)
    lvl_ref[...] = 31 - lax.clz(row2 ^ col2)

    llb_f, llb_b = gp_ref[0:1, :], gp_ref[1:2, :]
    l1m_f, l1m_b = gp_ref[2:3, :], gp_ref[3:4, :]
    oml_f, oml_b = gp_ref[4:5, :], gp_ref[5:6, :]

    def to3(x):
        return x.reshape(C // SUBLANES, SUBLANES, LANES)

    def prep_gates(c):
        rows = pl.ds(c * C, C)
        q = _silu(hq_ref[rows, :])
        lff, kf = _hgrn_gate(ff_ref[rows, :], llb_f, l1m_f, oml_f)
        lfb, kb = _hgrn_gate(fb_ref[rows, :], llb_b, l1m_b, oml_b)
        lf2 = jnp.concatenate([lff, lfb], axis=1) * LOG2E
        return q, kf, kb, lf2, _split3(lf2)

    def prep_cumsum(vals):
        x1, x2, x3 = vals[4]
        return _dot(tri, x1) + _dot(tri, x2) + _dot(tri, x3)

    def prep_decay(c, gmax, vals, cs):
        rows = pl.ds(c * C, C)
        q, kf, kb, lf2, _ = vals
        bf = cs[:, :LANES]
        bb_incl = cs[:, LANES:]
        pb = bb_incl - lf2[:, LANES:]
        bf_last = bf[C - 1:C, :]
        tot_b = bb_incl[C - 1:C, :]
        q_s[rows, :] = q
        kf_s[rows, :] = kf
        kb_s[rows, :] = kb
        bf_s[rows, :] = bf
        pb_s[rows, :] = pb
        dec_ref[c, 0:1, :] = jnp.exp2(bf_last)
        dec_ref[c, 1:2, :] = jnp.exp2(tot_b)
        xi_ref[rows, 0:LANES] = (q * jnp.exp2(bf)).astype(BF16)
        xi_ref[rows, LANES:] = (q * jnp.exp2(tot_b - pb)).astype(BF16)
        for blk in range(C // HG_BLOCK):
            s, e = blk * HG_BLOCK, (blk + 1) * HG_BLOCK - 1
            gmax = jnp.maximum(gmax, bf[s:s + 1, :] - bf[e:e + 1, :])
            gmax = jnp.maximum(gmax, pb[s:s + 1, :] - pb[e:e + 1, :])
        kfs = (kf * jnp.exp2(bf_last - bf)).astype(BF16)
        kbs = (kb * jnp.exp2(pb)).astype(BF16)
        vt = hi_ref[rows, :].T.astype(BF16)
        return gmax, (vt, kfs, kbs)

    def prep_states(c, ops):
        vt, kfs, kbs = ops
        dstf_ref[c] = jnp.where(bd_mask, _dot(vt, kfs), 0.0)
        dstb_ref[c] = jnp.where(bd_mask, _dot(vt, kbs), 0.0)

    jm = (lax.broadcasted_iota(jnp.int32, (2 * HG_BLOCK, C), 1)
          - lax.broadcasted_iota(jnp.int32, (2 * HG_BLOCK, C), 0) % HG_BLOCK)

    def fast_pack(c):
        rows = pl.ds(c * C, C)
        q, kf, kb, bf, pb = q_s[rows, :], kf_s[rows, :], kb_s[rows, :], bf_s[rows, :], pb_s[rows, :]
        packed = []
        for blk in range(C // HG_BLOCK):
            s, e = blk * HG_BLOCK, (blk + 1) * HG_BLOCK
            rf = bf[e - 1:e, :]
            rb_ = pb[s:s + 1, :]
            qf = q[s:e, :] * jnp.exp2(bf[s:e, :] - rf)
            qb = q[s:e, :] * jnp.exp2(rb_ - pb[s:e, :])
            x = jnp.concatenate([jnp.where(h0, qf, 0.0), jnp.where(h0, 0.0, qf),
                                 jnp.where(h0, qb, 0.0), jnp.where(h0, 0.0, qb)], axis=0).astype(BF16)
            kfv = (kf[:e, :] * jnp.exp2(rf - bf[:e, :])).astype(BF16)
            kbv = (kb[s:, :] * jnp.exp2(pb[s:, :] - rb_)).astype(BF16)
            parts = [kfv]
            if e < C:
                parts.append(jnp.zeros((C - e, LANES), BF16))
            if s > 0:
                parts.append(jnp.zeros((s, LANES), BF16))
            parts.append(kbv)
            packed.append((x, jnp.concatenate(parts, axis=0)))
        return packed

    def fast_scores(packed):
        return [_dot_nt(x, w) for x, w in packed]

    def fast_mask(scores):
        a_rows = []
        for blk, sc in enumerate(scores):
            s = blk * HG_BLOCK
            af = sc[:2 * HG_BLOCK, :C]
            ab = sc[2 * HG_BLOCK:, C:]
            a_rows.append(jnp.where(jm < s, af, jnp.where(jm > s, ab, af + ab)))
        return jnp.concatenate(a_rows, axis=0).astype(BF16)

    def fast_store(c, o2):
        o_blocks = []
        for blk in range(C // HG_BLOCK):
            base = blk * 2 * HG_BLOCK
            o_blocks.append(jnp.where(h0, o2[base:base + HG_BLOCK], o2[base + HG_BLOCK:base + 2 * HG_BLOCK]))
        oacc[pl.ds(c * C, C), :] = jnp.concatenate(o_blocks, axis=0)

    def exact_body(c, carry):
        rows = pl.ds(pl.multiple_of(c * C, C), C)
        q, kf, kb, bf, pb = q_s[rows, :], kf_s[rows, :], kb_s[rows, :], bf_s[rows, :], pb_s[rows, :]
        v = hi_ref[rows, :]
        vb = v.astype(BF16)

        d0 = (q * (kf + kb)).astype(BF16)
        o = _dot(d0, bd) * v
        q3, v3, bf3, pb3, kf3, kb3 = to3(q), to3(v), to3(bf), to3(pb), to3(kf), to3(kb)
        nbf3, npb3 = -bf3, -pb3
        for dl in range(1, SUBLANES):
            src_fw = sub < (SUBLANES - dl)
            u = jnp.where(src_fw, nbf3, pb3)
            kmix = jnp.where(src_fw, kf3, kb3)
            w = jnp.where(sub >= dl, bf3, npb3)
            rel = w + pltpu.roll(u, dl, axis=1)
            d = (q3 * pltpu.roll(kmix, dl, axis=1) * jnp.exp2(rel)).reshape(C, LANES).astype(BF16)
            o = o + _dot(d, bd) * pltpu.roll(v3, dl, axis=1).reshape(C, LANES)

        lvl = lvl_ref[...]
        a_tot = jnp.zeros((2 * C, C), F32)
        m = SUBLANES
        while m < C:
            nblk = C // (2 * m)

            def halves(x):
                x4 = x.reshape(nblk, 2, m, LANES)
                return x4[:, 0], x4[:, 1]

            def join(lo, up):
                return jnp.stack([lo, up], axis=1).reshape(C, LANES)

            q_lo, q_up = halves(q)
            kf_lo, _ = halves(kf)
            _, kb_up = halves(kb)
            bf_lo, bf_up = halves(bf)
            pb_lo, pb_up = halves(pb)
            rf = bf_lo[:, m - 1:m, :]
            rb_ = pb_up[:, 0:1, :]
            xq = join(q_lo * jnp.exp2(rb_ - pb_lo), q_up * jnp.exp2(bf_up - rf))
            yk = join(kf_lo * jnp.exp2(rf - bf_lo), kb_up * jnp.exp2(pb_up - rb_))
            xs = jnp.concatenate([jnp.where(h0, xq, 0.0), jnp.where(h0, 0.0, xq)], axis=0).astype(BF16)
            a = _dot_nt(xs, yk.astype(BF16))
            a_tot = jnp.where(lvl == (m.bit_length() - 1), a, a_tot)
            m *= 2
        o2 = _dot(a_tot.astype(BF16), vb)
        o = o + jnp.where(h0, o2[:C], o2[C:])
        oacc[rows, :] = o
        return carry

    gmax = jnp.zeros((1, LANES), F32)
    for g0 in range(0, nc, HG_GROUP):
        grp = range(g0, g0 + HG_GROUP)
        vals = [prep_gates(c) for c in grp]
        sums = [prep_cumsum(vv) for vv in vals]
        ops = []
        for c, vv, cs in zip(grp, vals, sums):
            gmax, op = prep_decay(c, gmax, vv, cs)
            ops.append(op)
        for c, op in zip(grp, ops):
            prep_states(c, op)
    safe = jnp.max(gmax) < HG_SAFE_LOG2

    @pl.when(safe)
    def _():
        for g0 in range(0, nc, HG_GROUP):
            grp = range(g0, g0 + HG_GROUP)
            packed = [fast_pack(c) for c in grp]
            scores = [fast_scores(pk) for pk in packed]
            amats = [fast_mask(sc) for sc in scores]
            outs = [_dot(a, hi_ref[pl.ds(c * C, C), :].astype(BF16)) for a, c in zip(amats, grp)]
            for c, o2 in zip(grp, outs):
                fast_store(c, o2)

    @pl.when(jnp.logical_not(safe))
    def _():
        lax.fori_loop(0, nc, exact_body, 0)

    for sq in range(nc // ncs):
        chunks = range(sq * ncs, (sq + 1) * ncs)
        if has_state:
            stf, stb = _pair_state_in(s0_ref, 0), _pair_state_in(s0_ref, 1)
        else:
            stf = jnp.zeros((LANES, LANES), F32)
            stb = stf
        for c in chunks:
            stcat_ref[c, :, 0:LANES] = stf.astype(BF16)
            stf = stf * dec_ref[c, 0:1, :] + dstf_ref[c]
        for c in reversed(chunks):
            stcat_ref[c, :, LANES:] = stb.astype(BF16)
            stb = stb * dec_ref[c, 1:2, :] + dstb_ref[c]
        if want_state:
            _pair_state_out(stf, st_view(sq), 0)
            _pair_state_out(stb, st_view(sq), 1)

    for g0 in range(0, nc, HG_GROUP):
        grp = range(g0, g0 + HG_GROUP)
        rows = [slice(c * C, (c + 1) * C) for c in grp]
        inter = [_dot_nt(xi_ref[r, :], stcat_ref[c]) for c, r in zip(grp, rows)]
        outs = [oacc[r, :] + x for r, x in zip(rows, inter)]
        sums = [_head_sums_sq(o, bd) for o in outs]
        for r, o, ms in zip(rows, outs, sums):
            y = o * lax.rsqrt(ms * (1.0 / HEAD_DIM) + LN_EPS)
            o_ref[r, :] = (y * _silu(hg_ref[r, :])).astype(BF16)


def _hgrn(proj, gate_params, seq, layer, s0, want_state, carried=None):
    rows = proj.shape[0]
    L = MIX_ROWS
    nc = L // HG_CHUNK
    in_specs, state_spec, out_specs, out_shape = _mixer_specs(
        proj, (COL_HQ, COL_HFF, COL_HFB, COL_HI, COL_HG), seq, layer, s0, want_state, carried)
    in_specs = [pl.BlockSpec((None, None, SUBLANES, LANES), lambda p, b: (layer, p, 0, 0))] + in_specs
    args = [gate_params, proj, proj, proj, proj, proj]
    if s0 is not None:
        in_specs.append(state_spec)
        args.append(s0)
    aliases = _carry_states(in_specs, args, carried)
    kern = functools.partial(_hgrn_kernel, L=L, seq=seq, layer=layer, has_state=s0 is not None,
                             want_state=want_state, carried=carried is not None)
    return pl.pallas_call(
        kern,
        grid=(N_PAIRS, rows // L),
        in_specs=in_specs,
        out_specs=out_specs,
        out_shape=out_shape,
        input_output_aliases=aliases,
        scratch_shapes=[pltpu.VMEM((L, LANES), F32),
                        pltpu.VMEM((L, 2 * LANES), BF16),
                        pltpu.VMEM((nc, LANES, LANES), F32),
                        pltpu.VMEM((nc, LANES, LANES), F32),
                        pltpu.VMEM((nc, SUBLANES, LANES), F32),
                        pltpu.VMEM((nc, LANES, 2 * LANES), BF16),
                        pltpu.VMEM((2 * HG_CHUNK, HG_CHUNK), jnp.int32)]
                       + [pltpu.VMEM((L, LANES), F32)] * 5,
        compiler_params=pltpu.CompilerParams(
            dimension_semantics=("arbitrary", "arbitrary"), vmem_limit_bytes=VMEM_LIMIT),
        name="hgrn2",
    )(*args)


def _ffn_kernel(x_ref, of_ref, or_ref, oh_ref, mod_ref, wo_ref, lng_ref, lnb_ref,
                wg_ref, wu_ref, wd_ref, y_ref):
    c1 = FNET_WIDTH
    c2 = FNET_WIDTH + HEADS_WIDTH
    g1 = mod_ref[:, 2 * D_MODEL:3 * D_MODEL]
    sh2 = mod_ref[:, 3 * D_MODEL:4 * D_MODEL]
    sc2 = mod_ref[:, 4 * D_MODEL:5 * D_MODEL]
    g2 = mod_ref[:, 5 * D_MODEL:6 * D_MODEL]
    mix = (_dot(of_ref[...], wo_ref[0:c1, :]) + _dot(or_ref[...], wo_ref[c1:c2, :])
           + _dot(oh_ref[...], wo_ref[c2:, :]))
    x1 = _ln(ALPHA * x_ref[...] + g1 * mix) * lng_ref[0:1, :] + lnb_ref[0:1, :]
    h2 = (_ln(x1) * (1.0 + sc2) + sh2).astype(BF16)
    gate = _dot(h2, wg_ref[...])
    up = _dot(h2, wu_ref[...])
    act = (_silu(gate) * up).astype(BF16)
    ffn = _dot(act, wd_ref[...])
    y_ref[...] = _ln(ALPHA * x1 + g2 * ffn) * lng_ref[1:2, :] + lnb_ref[1:2, :]


def _out_ffn(x2d, o_f, o_r, o_h, mod, w_out_bf, ln_g, ln_b, wg_bf, wu_bf, wd_bf, layer, mod_row):
    rows = x2d.shape[0]
    once = pl.Buffered(1)
    row_blk = lambda w: pl.BlockSpec((ROW_TILE, w), lambda i: (i, 0))
    return pl.pallas_call(
        _ffn_kernel,
        grid=(rows // ROW_TILE,),
        in_specs=[row_blk(D_MODEL), row_blk(FNET_WIDTH), row_blk(HEADS_WIDTH), row_blk(HEADS_WIDTH),
                  pl.BlockSpec((None, None, 1, N_MOD * D_MODEL), lambda i: (layer, mod_row(i), 0, 0)),
                  pl.BlockSpec((None, D_MODEL, D_MODEL), lambda i: (layer, 0, 0), pipeline_mode=once),
                  pl.BlockSpec((None, 2, D_MODEL), lambda i: (layer, 0, 0)),
                  pl.BlockSpec((None, 2, D_MODEL), lambda i: (layer, 0, 0)),
                  pl.BlockSpec((None, D_MODEL, D_FF), lambda i: (layer, 0, 0), pipeline_mode=once),
                  pl.BlockSpec((None, D_MODEL, D_FF), lambda i: (layer, 0, 0), pipeline_mode=once),
                  pl.BlockSpec((None, D_FF, D_MODEL), lambda i: (layer, 0, 0), pipeline_mode=once)],
        out_specs=row_blk(D_MODEL),
        out_shape=jax.ShapeDtypeStruct((rows, D_MODEL), F32),
        compiler_params=pltpu.CompilerParams(
            dimension_semantics=("arbitrary",), vmem_limit_bytes=VMEM_LIMIT),
        name="out_ffn",
    )(x2d, o_f, o_r, o_h, mod, w_out_bf, ln_g, ln_b, wg_bf, wu_bf, wd_bf)


def kernel(x_prompt, x_sample, c, state_ret, state_hgrn, c_ctx, w_mod, b_mod, w_in, w_out,
           ret_log_decay, hg_lower_bound, ln_g, ln_b, w_gate, w_up, w_down):
    B, S, _ = x_prompt.shape
    DB, DS, _ = x_sample.shape

    p = jax.nn.softmax(hg_lower_bound.astype(F32), axis=1)
    cum = jnp.cumsum(p, axis=1)
    lbs = cum - cum[:, :1]
    kinds = jnp.stack([jnp.log(lbs), jnp.log1p(-lbs), 1.0 - lbs], axis=0)
    gp = jnp.transpose(kinds.reshape(3, 2, DEPTH, N_PAIRS, LANES), (2, 3, 0, 1, 4))
    gp = gp.reshape(DEPTH, N_PAIRS, 6, LANES)
    gp = jnp.concatenate([gp, jnp.zeros((DEPTH, N_PAIRS, SUBLANES - 6, LANES), F32)], axis=2)
    lg_flat = (-jnp.exp(ret_log_decay.astype(F32))).reshape(-1)

    w_in_bf = w_in.astype(BF16)
    w_out_bf = w_out.astype(BF16)
    wg_bf = w_gate.astype(BF16)
    wu_bf = w_up.astype(BF16)
    wd_bf = w_down.astype(BF16)

    cv = jnp.concatenate([c_ctx[None, :], c, jnp.zeros((MOD_ROWS - 1 - DB, D_MODEL), F32)], axis=0)
    mod = _modulation(cv, w_mod, b_mod).reshape(DEPTH, MOD_ROWS, 1, N_MOD * D_MODEL)

    rope = _rope_tables(DS)

    ctx_row = lambda i: 0
    smp_row = lambda i: 1 + i // (DS // ROW_TILE)

    y = x_prompt.reshape(B * S, D_MODEL)
    z = x_sample.reshape(DB * DS, D_MODEL)
    new_state_ret, new_state_hgrn = None, None
    for l in range(DEPTH):
        proj = _inproj(y, mod, w_in_bf, l, ctx_row)
        o_f = _fnet(proj, S)
        o_r, new_state_ret = _retention(proj, lg_flat, S, l, None, None, True, new_state_ret)
        o_h, new_state_hgrn = _hgrn(proj, gp, S, l, None, True, new_state_hgrn)
        y = _out_ffn(y, o_f, o_r, o_h, mod, w_out_bf, ln_g, ln_b, wg_bf, wu_bf, wd_bf, l, ctx_row)
        proj = _inproj(z, mod, w_in_bf, l, smp_row)
        o_f = _fnet(proj, DS)
        (o_r,) = _retention(proj, lg_flat, DS, l, rope, state_ret, False)
        (o_h,) = _hgrn(proj, gp, DS, l, state_hgrn, False)
        z = _out_ffn(z, o_f, o_r, o_h, mod, w_out_bf, ln_g, ln_b, wg_bf, wu_bf, wd_bf, l, smp_row)

    return (y.reshape(B, S, D_MODEL), z.reshape(DB, DS, D_MODEL), new_state_ret, new_state_hgrn)
```

```python
import functools

import numpy as np
import jax
import jax.numpy as jnp
from jax import lax
from jax.experimental import pallas as pl
from jax.experimental.pallas import tpu as pltpu

F32 = jnp.float32
BF16 = jnp.bfloat16

D_MODEL = 1024
DEPTH = 2
GRID_W = 64
FNET_WIDTH = 256
FNET_GDIM = 64
N_HEADS = 6
N_PAIRS = N_HEADS // 2
HEAD_DIM = 64
LANES = 128
SUBLANES = 8
HEADS_WIDTH = N_HEADS * HEAD_DIM
IN_WIDTH = 3712
D_FF = 2816
ROPE_BASE = 10000.0
LN_EPS = 1e-5
LOG2E = 1.4426950408889634
N_MOD = 6
ALPHA = (2 * DEPTH) ** 0.25
MOD_ROWS = 8
ROW_TILE = 512
RET_ROWS = 256
MIX_ROWS = 1024
HG_CHUNK = 128
HG_GROUP = 8
HG_BLOCK = 32
HG_SAFE_LOG2 = 96.0
VMEM_LIMIT = 56 * 1024 * 1024

COL_RQ, COL_RK, COL_RV, COL_RG = 2, 5, 8, 11
COL_HQ, COL_HFF, COL_HFB, COL_HI, COL_HG = 14, 17, 20, 23, 26


def _silu(x):
    return x / (1.0 + jnp.exp(-x))


def _ln(x):
    mu = jnp.mean(x, axis=-1, keepdims=True)
    xc = x - mu
    var = jnp.mean(xc * xc, axis=-1, keepdims=True)
    return xc * lax.rsqrt(var + LN_EPS)


def _dot(a, b):
    return jnp.dot(a, b, preferred_element_type=F32)


def _dot_nt(a, b):
    return lax.dot_general(a, b, (((1,), (1,)), ((), ())), preferred_element_type=F32)


def _split2(x):
    hi = x.astype(BF16)
    lo = (x - hi.astype(F32)).astype(BF16)
    return hi, lo


def _split3(x):
    x1 = x.astype(BF16)
    r1 = x - x1.astype(F32)
    x2 = r1.astype(BF16)
    x3 = (r1 - x2.astype(F32)).astype(BF16)
    return x1, x2, x3


def _head_sums(x, bd_ones):
    hi, lo = _split2(x)
    return _dot(hi, bd_ones) + _dot(lo, bd_ones)


def _head_sums_sq(x, bd_ones):
    return _dot((x * x).astype(BF16), bd_ones)


def _bd_ones():
    r = lax.broadcasted_iota(jnp.int32, (LANES, LANES), 0)
    c = lax.broadcasted_iota(jnp.int32, (LANES, LANES), 1)
    return jnp.where((r < HEAD_DIM) == (c < HEAD_DIM), 1.0, 0.0).astype(BF16)


def _dot3_nt(sel, x):
    return sum(_dot_nt(sel, part) for part in _split3(x))


def _pair_state_in(s_ref, direction):
    z = jnp.zeros((HEAD_DIM, HEAD_DIM), F32)
    r = lax.broadcasted_iota(jnp.int32, (LANES, HEAD_DIM), 0)
    c = lax.broadcasted_iota(jnp.int32, (LANES, HEAD_DIM), 1)
    out = None
    for hh in range(2):
        s = s_ref[direction, hh]
        rows = jnp.concatenate([s, z] if hh == 0 else [z, s], axis=0)
        place = jnp.where(r == c + hh * HEAD_DIM, 1.0, 0.0).astype(BF16)
        term = _dot3_nt(place, rows)
        out = term if out is None else out + term
    return out


def _state_out_view(st_ref, layer, carried):
    if carried:
        return lambda sq: st_ref.at[sq]
    for other in range(DEPTH):
        if other != layer:
            st_ref[:, other] = jnp.zeros(st_ref.shape[:1] + st_ref.shape[2:], F32)
    return lambda sq: st_ref.at[sq, layer]


def _pair_state_out(st, st_ref, direction):
    r = lax.broadcasted_iota(jnp.int32, (HEAD_DIM, LANES), 0)
    c = lax.broadcasted_iota(jnp.int32, (HEAD_DIM, LANES), 1)
    for hh in range(2):
        pick = jnp.where(c == r + hh * HEAD_DIM, 1.0, 0.0).astype(BF16)
        st_ref[direction, hh] = _dot3_nt(pick, st[hh * HEAD_DIM:(hh + 1) * HEAD_DIM, :])


def _mod_kernel(cv_ref, w_ref, b_ref, o_ref):
    s = _silu(cv_ref[...]).astype(BF16)
    o_ref[...] = _dot(s, w_ref[...].astype(BF16)) + b_ref[...]


def _modulation(cv, w_mod, b_mod):
    tn = 1024
    width = N_MOD * D_MODEL
    return pl.pallas_call(
        _mod_kernel,
        grid=(DEPTH, width // tn),
        in_specs=[pl.BlockSpec((MOD_ROWS, D_MODEL), lambda l, j: (0, 0)),
                  pl.BlockSpec((None, D_MODEL, tn), lambda l, j: (l, 0, j)),
                  pl.BlockSpec((None, 1, tn), lambda l, j: (l, 0, j))],
        out_specs=pl.BlockSpec((None, MOD_ROWS, tn), lambda l, j: (l, 0, j)),
        out_shape=jax.ShapeDtypeStruct((DEPTH, MOD_ROWS, width), F32),
        compiler_params=pltpu.CompilerParams(
            dimension_semantics=("arbitrary", "arbitrary"), vmem_limit_bytes=VMEM_LIMIT),
        name="modulation",
    )(cv, w_mod, b_mod.reshape(DEPTH, 1, width))


def _side_cast_specs(side, nsteps):
    in_specs, out_specs, out_shape, args = [], [], [], []
    for w, layer in side:
        _, r, c = w.shape
        in_specs.append(pl.BlockSpec((None, r // nsteps, c), lambda i, layer=layer: (layer, i, 0)))
        out_specs.append(pl.BlockSpec((r // nsteps, c), lambda i: (i, 0)))
        out_shape.append(jax.ShapeDtypeStruct((r, c), BF16))
        args.append(w)
    return in_specs, out_specs, out_shape, args


def _side_casts(side_refs):
    n = len(side_refs) // 2
    for src, dst in zip(side_refs[:n], side_refs[n:]):
        dst[...] = src[...].astype(BF16)


def _inproj_kernel(x_ref, mod_ref, w_ref, *rest, n_side):
    side_in, o_ref, side_out = rest[:n_side], rest[n_side], rest[n_side + 1:]
    sh = mod_ref[:, 0:D_MODEL]
    sc = mod_ref[:, D_MODEL:2 * D_MODEL]
    half = ROW_TILE // 2
    for r in (slice(0, half), slice(half, ROW_TILE)):
        h = _ln(x_ref[r, :]) * (1.0 + sc) + sh
        o_ref[r, :] = _dot(h.astype(BF16), w_ref[...])
    _side_casts(side_in + side_out)


def _inproj(x2d, mod, w_in_bf, layer, mod_row, side=()):
    rows = x2d.shape[0]
    nsteps = rows // ROW_TILE
    s_in, s_out, s_shape, s_args = _side_cast_specs(side, nsteps)
    return pl.pallas_call(
        functools.partial(_inproj_kernel, n_side=len(side)),
        grid=(nsteps,),
        in_specs=[pl.BlockSpec((ROW_TILE, D_MODEL), lambda i: (i, 0)),
                  pl.BlockSpec((None, None, 1, N_MOD * D_MODEL), lambda i: (layer, mod_row(i), 0, 0)),
                  pl.BlockSpec((D_MODEL, IN_WIDTH), lambda i: (0, 0))] + s_in,
        out_specs=[pl.BlockSpec((ROW_TILE, IN_WIDTH), lambda i: (i, 0))] + s_out,
        out_shape=[jax.ShapeDtypeStruct((rows, IN_WIDTH), F32)] + s_shape,
        compiler_params=pltpu.CompilerParams(
            dimension_semantics=("arbitrary",), vmem_limit_bytes=VMEM_LIMIT),
        name="inproj",
    )(x2d, mod, w_in_bf, *s_args)


def _fnet_kernel(u_ref, w1_ref, cl_ref, sl_ref, o_ref, *, seq):
    u = u_ref[...].astype(BF16)
    t = _dot(u, w1_ref[...])
    uc = t[:, :FNET_WIDTH].astype(BF16)
    us = t[:, FNET_WIDTH:].astype(BF16)
    for s in range(u.shape[0] // seq):
        rows = slice(s * seq, (s + 1) * seq)
        o = _dot(cl_ref[...], uc[rows, :]) - _dot(sl_ref[...], us[rows, :])
        o_ref[rows, :] = o.astype(BF16)


def _fnet_consts(L):
    scale = 1.0 / np.sqrt(L * FNET_GDIM)
    k = np.arange(FNET_GDIM)
    ang = 2.0 * np.pi * ((k[:, None] * k[None, :]) % FNET_GDIM) / FNET_GDIM
    eye = np.eye(FNET_WIDTH // FNET_GDIM)
    w1 = np.concatenate([np.kron(eye, np.cos(ang)), np.kron(eye, np.sin(ang))], axis=1) * scale
    n = np.arange(L)
    angl = 2.0 * np.pi * ((n[:, None] * n[None, :]) % L) / L
    tables = (w1, np.cos(angl), np.sin(angl))
    return tuple(jnp.asarray(t, F32).astype(BF16) for t in tables)


def _fnet(proj, L):
    rows = proj.shape[0]
    w1, cl, sl = _fnet_consts(L)
    return pl.pallas_call(
        functools.partial(_fnet_kernel, seq=L),
        grid=(rows // MIX_ROWS,),
        in_specs=[pl.BlockSpec((MIX_ROWS, FNET_WIDTH), lambda b: (b, 0)),
                  pl.BlockSpec((FNET_WIDTH, 2 * FNET_WIDTH), lambda b: (0, 0)),
                  pl.BlockSpec((L, L), lambda b: (0, 0)),
                  pl.BlockSpec((L, L), lambda b: (0, 0))],
        out_specs=pl.BlockSpec((MIX_ROWS, FNET_WIDTH), lambda b: (b, 0)),
        out_shape=jax.ShapeDtypeStruct((rows, FNET_WIDTH), BF16),
        compiler_params=pltpu.CompilerParams(
            dimension_semantics=("arbitrary",), vmem_limit_bytes=VMEM_LIMIT),
        name="fnet",
    )(proj, w1, cl, sl)


def _rope_tables(L):
    half = HEAD_DIM // 2
    inv = ROPE_BASE ** (-np.arange(0, half, 2, dtype=np.float64) / half)
    n = np.arange(L)
    rows, cols = n // GRID_W, n % GRID_W
    lane = np.arange(LANES)
    f = lane % (half // 2)
    use_col = (lane % HEAD_DIM) >= half
    pos = np.where(use_col[None, :], cols[:, None], rows[:, None]).astype(np.float64)
    ang = pos * inv[f][None, :]
    lo = (lane % half) < (half // 2)
    cos = np.cos(ang)
    sin = np.where(lo[None, :], -np.sin(ang), np.sin(ang))
    return jnp.asarray(cos, F32), jnp.asarray(sin, F32)


def _rope(x, cos, sin_signed, lo_mask):
    quarter = HEAD_DIM // 4
    swapped = jnp.where(lo_mask, pltpu.roll(x, LANES - quarter, axis=1), pltpu.roll(x, quarter, axis=1))
    return x * cos + swapped * sin_signed


def _ret_kernel(*refs, L, seq, layer, use_rope, has_state, want_state, carried):
    it = iter(refs)
    lg_ref = next(it)
    q_ref, k_ref, v_ref, g_ref = next(it), next(it), next(it), next(it)
    if use_rope:
        cos_ref, sin_ref = next(it), next(it)
    if has_state:
        s0_ref = next(it)
    if carried:
        next(it)
    o_ref = next(it)
    if want_state:
        st_view = _state_out_view(next(it), layer, carried)
    oacc = next(it)

    p = pl.program_id(0)
    T = RET_ROWS
    nc = L // T
    ncs = seq // T
    lane = lax.broadcasted_iota(jnp.int32, (1, LANES), 1)
    h0 = lane < HEAD_DIM
    bd = _bd_ones()
    bd_mask = bd > 0

    def lg(direction, head):
        return lg_ref[layer * 2 * N_HEADS + direction * N_HEADS + 2 * p + head]

    i = lax.broadcasted_iota(jnp.int32, (T, T), 0)
    j = lax.broadcasted_iota(jnp.int32, (T, T), 1)
    rel = (i - j).astype(F32)
    masks = []
    for hh in range(2):
        arg = jnp.where(rel > 0, rel * lg(0, hh), -rel * lg(1, hh))
        masks.append(jnp.exp(arg) + jnp.where(rel == 0, 1.0, 0.0))
    mcat = jnp.concatenate(masks, axis=0)

    lgf_vec = jnp.where(h0, lg(0, 0), lg(0, 1))
    lgb_vec = jnp.where(h0, lg(1, 0), lg(1, 1))
    pos = lax.broadcasted_iota(jnp.int32, (T, LANES), 0).astype(F32)
    wf = jnp.exp((pos + 1.0) * lgf_vec)
    wb = jnp.exp((float(T) - pos) * lgb_vec)
    zf = jnp.exp((float(T) - 1.0 - pos) * lgf_vec)
    zb = jnp.exp(pos * lgb_vec)
    gf_chunk = jnp.exp(float(T) * lgf_vec)
    gb_chunk = jnp.exp(float(T) * lgb_vec)
    use_states = has_state or ncs > 1
    if use_rope:
        lo_mask = (lane % (HEAD_DIM // 2)) < (HEAD_DIM // 4)

    chunk_rows = [slice(c * T, (c + 1) * T) for c in range(nc)]
    qk = []
    for rows in chunk_rows:
        qc = q_ref[rows, :]
        kc = k_ref[rows, :] * (HEAD_DIM ** -0.5)
        if use_rope:
            qc = _rope(qc, cos_ref[rows, :], sin_ref[rows, :], lo_mask)
            kc = _rope(kc, cos_ref[rows, :], sin_ref[rows, :], lo_mask)
        qk.append((qc, kc))
    scores = []
    for qc, kc in qk:
        qs = jnp.concatenate([jnp.where(h0, qc, 0.0), jnp.where(h0, 0.0, qc)], axis=0).astype(BF16)
        scores.append(_dot_nt(qs, kc.astype(BF16)))
    dstf, dstb, xis = [], [], []
    if use_states or want_state:
        for rows, (qc, kc) in zip(chunk_rows, qk):
            vt = v_ref[rows, :].T.astype(BF16)
            dstf.append(jnp.where(bd_mask, _dot(vt, (kc * zf).astype(BF16)), 0.0))
            dstb.append(jnp.where(bd_mask, _dot(vt, (kc * zb).astype(BF16)), 0.0))
    if use_states:
        for qc, kc in qk:
            xis.append(jnp.concatenate([qc * wf, qc * wb], axis=1).astype(BF16))
    probs = [(s * mcat).astype(BF16) for s in scores]
    outs = [_dot(pm, v_ref[rows, :].astype(BF16)) for pm, rows in zip(probs, chunk_rows)]
    for rows, o2 in zip(chunk_rows, outs):
        oacc[rows, :] = jnp.where(h0, o2[:T], o2[T:])

    stf_in, stb_in = [None] * nc, [None] * nc
    if use_states or want_state:
        for sq in range(nc // ncs):
            chunks = range(sq * ncs, (sq + 1) * ncs)
            if has_state:
                stf, stb = _pair_state_in(s0_ref, 0), _pair_state_in(s0_ref, 1)
            else:
                stf = jnp.zeros((LANES, LANES), F32)
                stb = stf
            for c in chunks:
                stf_in[c] = stf
                stf = stf * gf_chunk + dstf[c]
            for c in reversed(chunks):
                stb_in[c] = stb
                stb = stb * gb_chunk + dstb[c]
            if want_state:
                _pair_state_out(stf, st_view(sq), 0)
                _pair_state_out(stb, st_view(sq), 1)

    outs = [oacc[rows, :] for rows in chunk_rows]
    if use_states:
        inter = [_dot_nt(xis[c], jnp.concatenate([stf_in[c], stb_in[c]], axis=1).astype(BF16))
                 for c in range(nc)]
        outs = [o + x for o, x in zip(outs, inter)]
    means = [_head_sums(o, bd) * (1.0 / HEAD_DIM) for o in outs]
    cent = [o - mu for o, mu in zip(outs, means)]
    variances = [_head_sums_sq(oc, bd) * (1.0 / HEAD_DIM) for oc in cent]
    for rows, oc, var in zip(chunk_rows, cent, variances):
        y = oc * lax.rsqrt(var + LN_EPS)
        o_ref[rows, :] = (y * _silu(g_ref[rows, :])).astype(BF16)


def _mixer_specs(proj, cols, seq, layer, s0, want_state, carried):
    rows = proj.shape[0]
    blk = lambda col: pl.BlockSpec((MIX_ROWS, LANES), lambda p, b, col=col: (b, col + p))
    in_specs = [blk(col) for col in cols]
    state_spec = None
    if s0 is not None:
        state_spec = pl.BlockSpec((None, None, 2, 2, HEAD_DIM, HEAD_DIM),
                                  lambda p, b: (b, layer, 0, p, 0, 0))
    out_specs = [pl.BlockSpec((MIX_ROWS, LANES), lambda p, b: (b, p))]
    out_shape = [jax.ShapeDtypeStruct((rows, HEADS_WIDTH), BF16)]
    if want_state:
        if carried is None:
            out_specs.append(pl.BlockSpec((MIX_ROWS // seq, DEPTH, 2, 2, HEAD_DIM, HEAD_DIM),
                                          lambda p, b: (b, 0, 0, p, 0, 0)))
        else:
            out_specs.append(pl.BlockSpec((MIX_ROWS // seq, None, 2, 2, HEAD_DIM, HEAD_DIM),
                                          lambda p, b: (b, layer, 0, p, 0, 0)))
        out_shape.append(jax.ShapeDtypeStruct((rows // seq, DEPTH, 2, N_HEADS, HEAD_DIM, HEAD_DIM), F32))
    return in_specs, state_spec, out_specs, out_shape


def _carry_states(in_specs, args, carried):
    if carried is None:
        return {}
    in_specs.append(pl.BlockSpec(memory_space=pl.ANY))
    args.append(carried)
    return {len(args) - 1: 1}


def _retention(proj, lg_flat, seq, layer, rope, s0, want_state, carried=None):
    rows = proj.shape[0]
    in_specs, state_spec, out_specs, out_shape = _mixer_specs(
        proj, (COL_RQ, COL_RK, COL_RV, COL_RG), seq, layer, s0, want_state, carried)
    in_specs = [pl.BlockSpec(memory_space=pltpu.SMEM)] + in_specs
    args = [lg_flat, proj, proj, proj, proj]
    if rope is not None:
        in_specs += [pl.BlockSpec((MIX_ROWS, LANES), lambda p, b: (0, 0))] * 2
        args += list(rope)
    if s0 is not None:
        in_specs.append(state_spec)
        args.append(s0)
    aliases = _carry_states(in_specs, args, carried)
    kern = functools.partial(_ret_kernel, L=MIX_ROWS, seq=seq, layer=layer, use_rope=rope is not None,
                             has_state=s0 is not None, want_state=want_state, carried=carried is not None)
    return pl.pallas_call(
        kern,
        grid=(N_PAIRS, rows // MIX_ROWS),
        in_specs=in_specs,
        out_specs=out_specs,
        out_shape=out_shape,
        input_output_aliases=aliases,
        scratch_shapes=[pltpu.VMEM((MIX_ROWS, LANES), F32)],
        compiler_params=pltpu.CompilerParams(
            dimension_semantics=("arbitrary", "arbitrary"), vmem_limit_bytes=VMEM_LIMIT),
        name="retention",
    )(*args)


def _hgrn_gate(x, llb, l1mlb, omlb):
    ls = jnp.minimum(x, 0.0) - jnp.log(1.0 + jnp.exp(-jnp.abs(x)))
    t = l1mlb + ls
    logf = jnp.maximum(llb, t) + jnp.log(1.0 + jnp.exp(-jnp.abs(llb - t)))
    kk = omlb / (1.0 + jnp.exp(x))
    return logf, kk


def _hgrn_kernel(*refs, L, seq, layer, has_state, want_state, carried):
    it = iter(refs)
    gp_ref = next(it)
    hq_ref, ff_ref, fb_ref, hi_ref, hg_ref = next(it), next(it), next(it), next(it), next(it)
    if has_state:
        s0_ref = next(it)
    if carried:
        next(it)
    o_ref = next(it)
    if want_state:
        st_view = _state_out_view(next(it), layer, carried)
    oacc, xi_ref, dstf_ref, dstb_ref, dec_ref, stcat_ref, lvl_ref = (next(it) for _ in range(7))
    q_s, kf_s, kb_s, bf_s, pb_s = (next(it) for _ in range(5))

    C = HG_CHUNK
    nc = L // C
    ncs = seq // C
    lane = lax.broadcasted_iota(jnp.int32, (1, LANES), 1)
    h0 = lane < HEAD_DIM
    bd = _bd_ones()
    bd_mask = bd > 0
    rowi = lax.broadcasted_iota(jnp.int32, (C, C), 0)
    coli = lax.broadcasted_iota(jnp.int32, (C, C), 1)
    tri = jnp.where(rowi >= coli, 1.0, 0.0).astype(BF16)
    sub = lax.broadcasted_iota(jnp.int32, (C // SUBLANES, SUBLANES, LANES), 1)
    row2 = lax.broadcasted_iota(jnp.int32, (2 * C, C), 0) % C
    col2 = lax.broadcasted_iota(jnp.int32, (2 * C, C), 1)
    lvl_ref[...] = 31 - lax.clz(row2 ^ col2)

    llb_f, llb_b = gp_ref[0:1, :], gp_ref[1:2, :]
    l1m_f, l1m_b = gp_ref[2:3, :], gp_ref[3:4, :]
    oml_f, oml_b = gp_ref[4:5, :], gp_ref[5:6, :]

    def to3(x):
        return x.reshape(C // SUBLANES, SUBLANES, LANES)

    def prep_gates(c):
        rows = pl.ds(c * C, C)
        q = _silu(hq_ref[rows, :])
        lff, kf = _hgrn_gate(ff_ref[rows, :], llb_f, l1m_f, oml_f)
        lfb, kb = _hgrn_gate(fb_ref[rows, :], llb_b, l1m_b, oml_b)
        lf2 = jnp.concatenate([lff, lfb], axis=1) * LOG2E
        return q, kf, kb, lf2, _split3(lf2)

    def prep_cumsum(vals):
        x1, x2, x3 = vals[4]
        return _dot(tri, x1) + _dot(tri, x2) + _dot(tri, x3)

    def prep_decay(c, gmax, vals, cs):
        rows = pl.ds(c * C, C)
        q, kf, kb, lf2, _ = vals
        bf = cs[:, :LANES]
        bb_incl = cs[:, LANES:]
        pb = bb_incl - lf2[:, LANES:]
        bf_last = bf[C - 1:C, :]
        tot_b = bb_incl[C - 1:C, :]
        q_s[rows, :] = q
        kf_s[rows, :] = kf
        kb_s[rows, :] = kb
        bf_s[rows, :] = bf
        pb_s[rows, :] = pb
        dec_ref[c, 0:1, :] = jnp.exp2(bf_last)
        dec_ref[c, 1:2, :] = jnp.exp2(tot_b)
        xi_ref[rows, 0:LANES] = (q * jnp.exp2(bf)).astype(BF16)
        xi_ref[rows, LANES:] = (q * jnp.exp2(tot_b - pb)).astype(BF16)
        for blk in range(C // HG_BLOCK):
            s, e = blk * HG_BLOCK, (blk + 1) * HG_BLOCK - 1
            gmax = jnp.maximum(gmax, bf[s:s + 1, :] - bf[e:e + 1, :])
            gmax = jnp.maximum(gmax, pb[s:s + 1, :] - pb[e:e + 1, :])
        kfs = (kf * jnp.exp2(bf_last - bf)).astype(BF16)
        kbs = (kb * jnp.exp2(pb)).astype(BF16)
        vt = hi_ref[rows, :].T.astype(BF16)
        return gmax, (vt, kfs, kbs)

    def prep_states(c, ops):
        vt, kfs, kbs = ops
        dstf_ref[c] = jnp.where(bd_mask, _dot(vt, kfs), 0.0)
        dstb_ref[c] = jnp.where(bd_mask, _dot(vt, kbs), 0.0)

    jm = (lax.broadcasted_iota(jnp.int32, (2 * HG_BLOCK, C), 1)
          - lax.broadcasted_iota(jnp.int32, (2 * HG_BLOCK, C), 0) % HG_BLOCK)

    def fast_pack(c):
        rows = pl.ds(c * C, C)
        q, kf, kb, bf, pb = q_s[rows, :], kf_s[rows, :], kb_s[rows, :], bf_s[rows, :], pb_s[rows, :]
        packed = []
        for blk in range(C // HG_BLOCK):
            s, e = blk * HG_BLOCK, (blk + 1) * HG_BLOCK
            rf = bf[e - 1:e, :]
            rb_ = pb[s:s + 1, :]
            qf = q[s:e, :] * jnp.exp2(bf[s:e, :] - rf)
            qb = q[s:e, :] * jnp.exp2(rb_ - pb[s:e, :])
            x = jnp.concatenate([jnp.where(h0, qf, 0.0), jnp.where(h0, 0.0, qf),
                                 jnp.where(h0, qb, 0.0), jnp.where(h0, 0.0, qb)], axis=0).astype(BF16)
            kfv = (kf[:e, :] * jnp.exp2(rf - bf[:e, :])).astype(BF16)
            kbv = (kb[s:, :] * jnp.exp2(pb[s:, :] - rb_)).astype(BF16)
            parts = [kfv]
            if e < C:
                parts.append(jnp.zeros((C - e, LANES), BF16))
            if s > 0:
                parts.append(jnp.zeros((s, LANES), BF16))
            parts.append(kbv)
            packed.append((x, jnp.concatenate(parts, axis=0)))
        return packed

    def fast_scores(packed):
        return [_dot_nt(x, w) for x, w in packed]

    def fast_mask(scores):
        a_rows = []
        for blk, sc in enumerate(scores):
            s = blk * HG_BLOCK
            af = sc[:2 * HG_BLOCK, :C]
            ab = sc[2 * HG_BLOCK:, C:]
            a_rows.append(jnp.where(jm < s, af, jnp.where(jm > s, ab, af + ab)))
        return jnp.concatenate(a_rows, axis=0).astype(BF16)

    def fast_store(c, o2):
        o_blocks = []
        for blk in range(C // HG_BLOCK):
            base = blk * 2 * HG_BLOCK
            o_blocks.append(jnp.where(h0, o2[base:base + HG_BLOCK], o2[base + HG_BLOCK:base + 2 * HG_BLOCK]))
        oacc[pl.ds(c * C, C), :] = jnp.concatenate(o_blocks, axis=0)

    def exact_body(c, carry):
        rows = pl.ds(pl.multiple_of(c * C, C), C)
        q, kf, kb, bf, pb = q_s[rows, :], kf_s[rows, :], kb_s[rows, :], bf_s[rows, :], pb_s[rows, :]
        v = hi_ref[rows, :]
        vb = v.astype(BF16)

        d0 = (q * (kf + kb)).astype(BF16)
        o = _dot(d0, bd) * v
        q3, v3, bf3, pb3, kf3, kb3 = to3(q), to3(v), to3(bf), to3(pb), to3(kf), to3(kb)
        nbf3, npb3 = -bf3, -pb3
        for dl in range(1, SUBLANES):
            src_fw = sub < (SUBLANES - dl)
            u = jnp.where(src_fw, nbf3, pb3)
            kmix = jnp.where(src_fw, kf3, kb3)
            w = jnp.where(sub >= dl, bf3, npb3)
            rel = w + pltpu.roll(u, dl, axis=1)
            d = (q3 * pltpu.roll(kmix, dl, axis=1) * jnp.exp2(rel)).reshape(C, LANES).astype(BF16)
            o = o + _dot(d, bd) * pltpu.roll(v3, dl, axis=1).reshape(C, LANES)

        lvl = lvl_ref[...]
        a_tot = jnp.zeros((2 * C, C), F32)
        m = SUBLANES
        while m < C:
            nblk = C // (2 * m)

            def halves(x):
                x4 = x.reshape(nblk, 2, m, LANES)
                return x4[:, 0], x4[:, 1]

            def join(lo, up):
                return jnp.stack([lo, up], axis=1).reshape(C, LANES)

            q_lo, q_up = halves(q)
            kf_lo, _ = halves(kf)
            _, kb_up = halves(kb)
            bf_lo, bf_up = halves(bf)
            pb_lo, pb_up = halves(pb)
            rf = bf_lo[:, m - 1:m, :]
            rb_ = pb_up[:, 0:1, :]
            xq = join(q_lo * jnp.exp2(rb_ - pb_lo), q_up * jnp.exp2(bf_up - rf))
            yk = join(kf_lo * jnp.exp2(rf - bf_lo), kb_up * jnp.exp2(pb_up - rb_))
            xs = jnp.concatenate([jnp.where(h0, xq, 0.0), jnp.where(h0, 0.0, xq)], axis=0).astype(BF16)
            a = _dot_nt(xs, yk.astype(BF16))
            a_tot = jnp.where(lvl == (m.bit_length() - 1), a, a_tot)
            m *= 2
        o2 = _dot(a_tot.astype(BF16), vb)
        o = o + jnp.where(h0, o2[:C], o2[C:])
        oacc[rows, :] = o
        return carry

    gmax = jnp.zeros((1, LANES), F32)
    for g0 in range(0, nc, HG_GROUP):
        grp = range(g0, g0 + HG_GROUP)
        vals = [prep_gates(c) for c in grp]
        sums = [prep_cumsum(vv) for vv in vals]
        ops = []
        for c, vv, cs in zip(grp, vals, sums):
            gmax, op = prep_decay(c, gmax, vv, cs)
            ops.append(op)
        for c, op in zip(grp, ops):
            prep_states(c, op)
    safe = jnp.max(gmax) < HG_SAFE_LOG2

    @pl.when(safe)
    def _():
        for g0 in range(0, nc, HG_GROUP):
            grp = range(g0, g0 + HG_GROUP)
            packed = [fast_pack(c) for c in grp]
            scores = [fast_scores(pk) for pk in packed]
            amats = [fast_mask(sc) for sc in scores]
            outs = [_dot(a, hi_ref[pl.ds(c * C, C), :].astype(BF16)) for a, c in zip(amats, grp)]
            for c, o2 in zip(grp, outs):
                fast_store(c, o2)

    @pl.when(jnp.logical_not(safe))
    def _():
        lax.fori_loop(0, nc, exact_body, 0)

    for sq in range(nc // ncs):
        chunks = range(sq * ncs, (sq + 1) * ncs)
        if has_state:
            stf, stb = _pair_state_in(s0_ref, 0), _pair_state_in(s0_ref, 1)
        else:
            stf = jnp.zeros((LANES, LANES), F32)
            stb = stf
        for c in chunks:
            stcat_ref[c, :, 0:LANES] = stf.astype(BF16)
            stf = stf * dec_ref[c, 0:1, :] + dstf_ref[c]
        for c in reversed(chunks):
            stcat_ref[c, :, LANES:] = stb.astype(BF16)
            stb = stb * dec_ref[c, 1:2, :] + dstb_ref[c]
        if want_state:
            _pair_state_out(stf, st_view(sq), 0)
            _pair_state_out(stb, st_view(sq), 1)

    for g0 in range(0, nc, HG_GROUP):
        grp = range(g0, g0 + HG_GROUP)
        rows = [slice(c * C, (c + 1) * C) for c in grp]
        inter = [_dot_nt(xi_ref[r, :], stcat_ref[c]) for c, r in zip(grp, rows)]
        outs = [oacc[r, :] + x for r, x in zip(rows, inter)]
        sums = [_head_sums_sq(o, bd) for o in outs]
        for r, o, ms in zip(rows, outs, sums):
            y = o * lax.rsqrt(ms * (1.0 / HEAD_DIM) + LN_EPS)
            o_ref[r, :] = (y * _silu(hg_ref[r, :])).astype(BF16)


def _hgrn(proj, gate_params, seq, layer, s0, want_state, carried=None):
    rows = proj.shape[0]
    L = MIX_ROWS
    nc = L // HG_CHUNK
    in_specs, state_spec, out_specs, out_shape = _mixer_specs(
        proj, (COL_HQ, COL_HFF, COL_HFB, COL_HI, COL_HG), seq, layer, s0, want_state, carried)
    in_specs = [pl.BlockSpec((None, None, SUBLANES, LANES), lambda p, b: (layer, p, 0, 0))] + in_specs
    args = [gate_params, proj, proj, proj, proj, proj]
    if s0 is not None:
        in_specs.append(state_spec)
        args.append(s0)
    aliases = _carry_states(in_specs, args, carried)
    kern = functools.partial(_hgrn_kernel, L=L, seq=seq, layer=layer, has_state=s0 is not None,
                             want_state=want_state, carried=carried is not None)
    return pl.pallas_call(
        kern,
        grid=(N_PAIRS, rows // L),
        in_specs=in_specs,
        out_specs=out_specs,
        out_shape=out_shape,
        input_output_aliases=aliases,
        scratch_shapes=[pltpu.VMEM((L, LANES), F32),
                        pltpu.VMEM((L, 2 * LANES), BF16),
                        pltpu.VMEM((nc, LANES, LANES), F32),
                        pltpu.VMEM((nc, LANES, LANES), F32),
                        pltpu.VMEM((nc, SUBLANES, LANES), F32),
                        pltpu.VMEM((nc, LANES, 2 * LANES), BF16),
                        pltpu.VMEM((2 * HG_CHUNK, HG_CHUNK), jnp.int32)]
                       + [pltpu.VMEM((L, LANES), F32)] * 5,
        compiler_params=pltpu.CompilerParams(
            dimension_semantics=("arbitrary", "arbitrary"), vmem_limit_bytes=VMEM_LIMIT),
        name="hgrn2",
    )(*args)


def _ffn_kernel(x_ref, of_ref, or_ref, oh_ref, mod_ref, wo_ref, lng_ref, lnb_ref,
                wg_ref, wu_ref, wd_ref, *rest, n_side):
    side_in, y_ref, side_out = rest[:n_side], rest[n_side], rest[n_side + 1:]
    g1 = mod_ref[:, 2 * D_MODEL:3 * D_MODEL]
    sh2 = mod_ref[:, 3 * D_MODEL:4 * D_MODEL]
    sc2 = mod_ref[:, 4 * D_MODEL:5 * D_MODEL]
    g2 = mod_ref[:, 5 * D_MODEL:6 * D_MODEL]
    half = ROW_TILE // 2
    halves = (slice(0, half), slice(half, ROW_TILE))
    mix = [_dot(jnp.concatenate([of_ref[r, :], or_ref[r, :], oh_ref[r, :]], axis=1), wo_ref[...])
           for r in halves]
    x1, gate, up, ffn = [], [], [], []
    for r, m in zip(halves, mix):
        x1.append(_ln(ALPHA * x_ref[r, :] + g1 * m) * lng_ref[0:1, :] + lnb_ref[0:1, :])
        h2 = (_ln(x1[-1]) * (1.0 + sc2) + sh2).astype(BF16)
        gate.append(_dot(h2, wg_ref[...]))
        up.append(_dot(h2, wu_ref[...]))
    for g, u in zip(gate, up):
        act = (_silu(g) * u).astype(BF16)
        ffn.append(_dot(act, wd_ref[...]))
    for r, x, f in zip(halves, x1, ffn):
        y_ref[r, :] = _ln(ALPHA * x + g2 * f) * lng_ref[1:2, :] + lnb_ref[1:2, :]
    _side_casts(side_in + side_out)


def _out_ffn(x2d, o_f, o_r, o_h, mod, w_out_bf, ln_g, ln_b, wg_bf, wu_bf, wd_bf, layer, mod_row, side=()):
    rows = x2d.shape[0]
    nsteps = rows // ROW_TILE
    s_in, s_out, s_shape, s_args = _side_cast_specs(side, nsteps)
    once = pl.Buffered(1)
    row_blk = lambda w: pl.BlockSpec((ROW_TILE, w), lambda i: (i, 0))
    return pl.pallas_call(
        functools.partial(_ffn_kernel, n_side=len(side)),
        grid=(nsteps,),
        in_specs=[row_blk(D_MODEL), row_blk(FNET_WIDTH), row_blk(HEADS_WIDTH), row_blk(HEADS_WIDTH),
                  pl.BlockSpec((None, None, 1, N_MOD * D_MODEL), lambda i: (layer, mod_row(i), 0, 0)),
                  pl.BlockSpec((D_MODEL, D_MODEL), lambda i: (0, 0), pipeline_mode=once),
                  pl.BlockSpec((None, 2, D_MODEL), lambda i: (layer, 0, 0)),
                  pl.BlockSpec((None, 2, D_MODEL), lambda i: (layer, 0, 0)),
                  pl.BlockSpec((D_MODEL, D_FF), lambda i: (0, 0), pipeline_mode=once),
                  pl.BlockSpec((D_MODEL, D_FF), lambda i: (0, 0), pipeline_mode=once),
                  pl.BlockSpec((D_FF, D_MODEL), lambda i: (0, 0), pipeline_mode=once)] + s_in,
        out_specs=[row_blk(D_MODEL)] + s_out,
        out_shape=[jax.ShapeDtypeStruct((rows, D_MODEL), F32)] + s_shape,
        compiler_params=pltpu.CompilerParams(
            dimension_semantics=("arbitrary",), vmem_limit_bytes=VMEM_LIMIT),
        name="out_ffn",
    )(x2d, o_f, o_r, o_h, mod, w_out_bf, ln_g, ln_b, wg_bf, wu_bf, wd_bf, *s_args)


def kernel(x_prompt, x_sample, c, state_ret, state_hgrn, c_ctx, w_mod, b_mod, w_in, w_out,
           ret_log_decay, hg_lower_bound, ln_g, ln_b, w_gate, w_up, w_down):
    B, S, _ = x_prompt.shape
    DB, DS, _ = x_sample.shape

    p = jax.nn.softmax(hg_lower_bound.astype(F32), axis=1)
    cum = jnp.cumsum(p, axis=1)
    lbs = cum - cum[:, :1]
    kinds = jnp.stack([jnp.log(lbs), jnp.log1p(-lbs), 1.0 - lbs], axis=0)
    gp = jnp.transpose(kinds.reshape(3, 2, DEPTH, N_PAIRS, LANES), (2, 3, 0, 1, 4))
    gp = gp.reshape(DEPTH, N_PAIRS, 6, LANES)
    gp = jnp.concatenate([gp, jnp.zeros((DEPTH, N_PAIRS, SUBLANES - 6, LANES), F32)], axis=2)
    lg_flat = (-jnp.exp(ret_log_decay.astype(F32))).reshape(-1)

    cv =jnp.concatenate([c_ctx[None, :], c, jnp.zeros((MOD_ROWS - 1 - DB, D_MODEL), F32)], axis=0)
    mod = _modulation(cv, w_mod, b_mod).reshape(DEPTH, MOD_ROWS, 1, N_MOD * D_MODEL)

    rope = _rope_tables(DS)

    ctx_row = lambda i: 0
    smp_row = lambda i: 1 + i // (DS // ROW_TILE)

    y = x_prompt.reshape(B * S, D_MODEL)
    z = x_sample.reshape(DB * DS, D_MODEL)
    new_state_ret, new_state_hgrn = None, None
    w_in_bf = w_in[0].astype(BF16)
    w_out_bf = wg_bf = wu_bf = wd_bf = None
    for l in range(DEPTH):
        side_c = [(w_out, l), (w_gate, l)] if l == 0 else []
        side_s = [(w_up, l), (w_down, l)] if l == 0 else []
        proj_c, *cast_c = _inproj(y, mod, w_in_bf, l, ctx_row, side_c)
        proj_s, *cast_s = _inproj(z, mod, w_in_bf, l, smp_row, side_s)
        if l == 0:
            (w_out_bf, wg_bf), (wu_bf, wd_bf) = cast_c, cast_s
        of_c = _fnet(proj_c, S)
        or_c, new_state_ret = _retention(proj_c, lg_flat, S, l, None, None, True, new_state_ret)
        oh_c, new_state_hgrn = _hgrn(proj_c, gp, S, l, None, True, new_state_hgrn)
        of_s = _fnet(proj_s, DS)
        (or_s,) = _retention(proj_s, lg_flat, DS, l, rope, state_ret, False)
        (oh_s,) = _hgrn(proj_s, gp, DS, l, state_hgrn, False)
        last = l + 1 == DEPTH
        side_c = [] if last else [(w_in, l + 1), (w_out, l + 1)]
        side_s = [] if last else [(w_gate, l + 1), (w_up, l + 1), (w_down, l + 1)]
        weights = (w_out_bf, ln_g, ln_b, wg_bf, wu_bf, wd_bf)
        y, *cast_c = _out_ffn(y, of_c, or_c, oh_c, mod, *weights, l, ctx_row, side_c)
        z, *cast_s = _out_ffn(z, of_s, or_s, oh_s, mod, *weights, l, smp_row, side_s)
        if not last:
            (w_in_bf, w_out_bf), (wg_bf, wu_bf, wd_bf) = cast_c, cast_s

    return (y.reshape(B, S, D_MODEL), z.reshape(DB, DS, D_MODEL), new_state_ret, new_state_hgrn)
```

```python
import functools

import numpy as np
import jax
import jax.numpy as jnp
from jax import lax
from jax.experimental import pallas as pl
from jax.experimental.pallas import tpu as pltpu

F32 = jnp.float32
BF16 = jnp.bfloat16

D_MODEL = 1024
DEPTH = 2
GRID_W = 64
FNET_WIDTH = 256
FNET_GDIM = 64
N_HEADS = 6
N_PAIRS = N_HEADS // 2
HEAD_DIM = 64
LANES = 128
SUBLANES = 8
HEADS_WIDTH = N_HEADS * HEAD_DIM
IN_WIDTH = 3712
D_FF = 2816
ROPE_BASE = 10000.0
LN_EPS = 1e-5
LOG2E = 1.4426950408889634
N_MOD = 6
ALPHA = (2 * DEPTH) ** 0.25
MOD_ROWS = 8
ROW_TILE = 512
RET_ROWS = 256
MIX_ROWS = 1024
HG_CHUNK = 128
HG_GROUP = 8
HG_BLOCK = 32
HG_SAFE_LOG2 = 96.0
VMEM_LIMIT = 56 * 1024 * 1024

COL_RQ, COL_RK, COL_RV, COL_RG = 2, 5, 8, 11
COL_HQ, COL_HFF, COL_HFB, COL_HI, COL_HG = 14, 17, 20, 23, 26


def _silu(x):
    return x / (1.0 + jnp.exp(-x))


def _ln(x):
    mu = jnp.mean(x, axis=-1, keepdims=True)
    xc = x - mu
    var = jnp.mean(xc * xc, axis=-1, keepdims=True)
    return xc * lax.rsqrt(var + LN_EPS)


def _dot(a, b):
    return jnp.dot(a, b, preferred_element_type=F32)


def _dot_nt(a, b):
    return lax.dot_general(a, b, (((1,), (1,)), ((), ())), preferred_element_type=F32)


def _split2(x):
    hi = x.astype(BF16)
    lo = (x - hi.astype(F32)).astype(BF16)
    return hi, lo


def _split3(x):
    x1 = x.astype(BF16)
    r1 = x - x1.astype(F32)
    x2 = r1.astype(BF16)
    x3 = (r1 - x2.astype(F32)).astype(BF16)
    return x1, x2, x3


def _head_sums(x, bd_ones):
    hi, lo = _split2(x)
    return _dot(hi, bd_ones) + _dot(lo, bd_ones)


def _head_sums_sq(x, bd_ones):
    return _dot((x * x).astype(BF16), bd_ones)


def _bd_ones():
    r = lax.broadcasted_iota(jnp.int32, (LANES, LANES), 0)
    c = lax.broadcasted_iota(jnp.int32, (LANES, LANES), 1)
    return jnp.where((r < HEAD_DIM) == (c < HEAD_DIM), 1.0, 0.0).astype(BF16)


def _dot3_nt(sel, x):
    return sum(_dot_nt(sel, part) for part in _split3(x))


def _pair_state_in(s_ref, direction):
    z = jnp.zeros((HEAD_DIM, HEAD_DIM), F32)
    r = lax.broadcasted_iota(jnp.int32, (LANES, HEAD_DIM), 0)
    c = lax.broadcasted_iota(jnp.int32, (LANES, HEAD_DIM), 1)
    out = None
    for hh in range(2):
        s = s_ref[direction, hh]
        rows = jnp.concatenate([s, z] if hh == 0 else [z, s], axis=0)
        place = jnp.where(r == c + hh * HEAD_DIM, 1.0, 0.0).astype(BF16)
        term = _dot3_nt(place, rows)
        out = term if out is None else out + term
    return out


def _state_out_view(st_ref, layer, carried):
    if carried:
        return lambda sq: st_ref.at[sq]
    for other in range(DEPTH):
        if other != layer:
            st_ref[:, other] = jnp.zeros(st_ref.shape[:1] + st_ref.shape[2:], F32)
    return lambda sq: st_ref.at[sq, layer]


def _pair_state_out(st, st_ref, direction):
    r = lax.broadcasted_iota(jnp.int32, (HEAD_DIM, LANES), 0)
    c = lax.broadcasted_iota(jnp.int32, (HEAD_DIM, LANES), 1)
    for hh in range(2):
        pick = jnp.where(c == r + hh * HEAD_DIM, 1.0, 0.0).astype(BF16)
        st_ref[direction, hh] = _dot3_nt(pick, st[hh * HEAD_DIM:(hh + 1) * HEAD_DIM, :])


def _mod_kernel(cv_ref, w_ref, b_ref, o_ref):
    s = _silu(cv_ref[...]).astype(BF16)
    o_ref[...] = _dot(s, w_ref[...].astype(BF16)) + b_ref[...]


def _modulation(cv, w_mod, b_mod):
    tn = 3072
    width = N_MOD * D_MODEL
    return pl.pallas_call(
        _mod_kernel,
        grid=(DEPTH, width // tn),
        in_specs=[pl.BlockSpec((MOD_ROWS, D_MODEL), lambda l, j: (0, 0)),
                  pl.BlockSpec((None, D_MODEL, tn), lambda l, j: (l, 0, j)),
                  pl.BlockSpec((None, 1, tn), lambda l, j: (l, 0, j))],
        out_specs=pl.BlockSpec((None, MOD_ROWS, tn), lambda l, j: (l, 0, j)),
        out_shape=jax.ShapeDtypeStruct((DEPTH, MOD_ROWS, width), F32),
        compiler_params=pltpu.CompilerParams(
            dimension_semantics=("arbitrary", "arbitrary"), vmem_limit_bytes=VMEM_LIMIT),
        name="modulation",
    )(cv, w_mod, b_mod.reshape(DEPTH, 1, width))


def _side_cast_specs(side, nsteps):
    in_specs, out_specs, out_shape, args = [], [], [], []
    for w, layer in side:
        _, r, c = w.shape
        in_specs.append(pl.BlockSpec((None, r // nsteps, c), lambda i, layer=layer: (layer, i, 0)))
        out_specs.append(pl.BlockSpec((r // nsteps, c), lambda i: (i, 0)))
        out_shape.append(jax.ShapeDtypeStruct((r, c), BF16))
        args.append(w)
    return in_specs, out_specs, out_shape, args


def _side_casts(side_refs):
    n = len(side_refs) // 2
    for src, dst in zip(side_refs[:n], side_refs[n:]):
        dst[...] = src[...].astype(BF16)


def _inproj_kernel(x_ref, mod_ref, w_ref, *rest, n_side):
    side_in, o_ref, side_out = rest[:n_side], rest[n_side], rest[n_side + 1:]
    sh = mod_ref[:, 0:D_MODEL]
    sc = mod_ref[:, D_MODEL:2 * D_MODEL]
    half = ROW_TILE // 2
    for r in (slice(0, half), slice(half, ROW_TILE)):
        h = _ln(x_ref[r, :]) * (1.0 + sc) + sh
        o_ref[r, :] = _dot(h.astype(BF16), w_ref[...])
    _side_casts(side_in + side_out)


def _inproj(x2d, mod, w_in_bf, layer, mod_row, side=()):
    rows = x2d.shape[0]
    nsteps = rows // ROW_TILE
    s_in, s_out, s_shape, s_args = _side_cast_specs(side, nsteps)
    return pl.pallas_call(
        functools.partial(_inproj_kernel, n_side=len(side)),
        grid=(nsteps,),
        in_specs=[pl.BlockSpec((ROW_TILE, D_MODEL), lambda i: (i, 0)),
                  pl.BlockSpec((None, None, 1, N_MOD * D_MODEL), lambda i: (layer, mod_row(i), 0, 0)),
                  pl.BlockSpec((D_MODEL, IN_WIDTH), lambda i: (0, 0))] + s_in,
        out_specs=[pl.BlockSpec((ROW_TILE, IN_WIDTH), lambda i: (i, 0))] + s_out,
        out_shape=[jax.ShapeDtypeStruct((rows, IN_WIDTH), F32)] + s_shape,
        compiler_params=pltpu.CompilerParams(
            dimension_semantics=("arbitrary",), vmem_limit_bytes=VMEM_LIMIT),
        name="inproj",
    )(x2d, mod, w_in_bf, *s_args)


def _fnet_kernel(u_ref, w1_ref, cl_ref, sl_ref, o_ref, *, seq):
    u = u_ref[...].astype(BF16)
    t = _dot(u, w1_ref[...])
    uc = t[:, :FNET_WIDTH].astype(BF16)
    us = t[:, FNET_WIDTH:].astype(BF16)
    for s in range(u.shape[0] // seq):
        rows = slice(s * seq, (s + 1) * seq)
        o = _dot(cl_ref[...], uc[rows, :]) - _dot(sl_ref[...], us[rows, :])
        o_ref[rows, :] = o.astype(BF16)


def _fnet_consts(L):
    scale = 1.0 / np.sqrt(L * FNET_GDIM)
    k = np.arange(FNET_GDIM)
    ang = 2.0 * np.pi * ((k[:, None] * k[None, :]) % FNET_GDIM) / FNET_GDIM
    eye = np.eye(FNET_WIDTH // FNET_GDIM)
    w1 = np.concatenate([np.kron(eye, np.cos(ang)), np.kron(eye, np.sin(ang))], axis=1) * scale
    n = np.arange(L)
    angl = 2.0 * np.pi * ((n[:, None] * n[None, :]) % L) / L
    tables = (w1, np.cos(angl), np.sin(angl))
    return tuple(jnp.asarray(t, F32).astype(BF16) for t in tables)


def _fnet(proj, L):
    rows = proj.shape[0]
    w1, cl, sl = _fnet_consts(L)
    return pl.pallas_call(
        functools.partial(_fnet_kernel, seq=L),
        grid=(rows // MIX_ROWS,),
        in_specs=[pl.BlockSpec((MIX_ROWS, FNET_WIDTH), lambda b: (b, 0)),
                  pl.BlockSpec((FNET_WIDTH, 2 * FNET_WIDTH), lambda b: (0, 0)),
                  pl.BlockSpec((L, L), lambda b: (0, 0)),
                  pl.BlockSpec((L, L), lambda b: (0, 0))],
        out_specs=pl.BlockSpec((MIX_ROWS, FNET_WIDTH), lambda b: (b, 0)),
        out_shape=jax.ShapeDtypeStruct((rows, FNET_WIDTH), BF16),
        compiler_params=pltpu.CompilerParams(
            dimension_semantics=("arbitrary",), vmem_limit_bytes=VMEM_LIMIT),
        name="fnet",
    )(proj, w1, cl, sl)


def _rope_tables(L):
    half = HEAD_DIM // 2
    inv = ROPE_BASE ** (-np.arange(0, half, 2, dtype=np.float64) / half)
    n = np.arange(L)
    rows, cols = n // GRID_W, n % GRID_W
    lane = np.arange(LANES)
    f = lane % (half // 2)
    use_col = (lane % HEAD_DIM) >= half
    pos = np.where(use_col[None, :], cols[:, None], rows[:, None]).astype(np.float64)
    ang = pos * inv[f][None, :]
    lo = (lane % half) < (half // 2)
    cos = np.cos(ang)
    sin = np.where(lo[None, :], -np.sin(ang), np.sin(ang))
    return jnp.asarray(cos, F32), jnp.asarray(sin, F32)


def _rope(x, cos, sin_signed, lo_mask):
    quarter = HEAD_DIM // 4
    swapped = jnp.where(lo_mask, pltpu.roll(x, LANES - quarter, axis=1), pltpu.roll(x, quarter, axis=1))
    return x * cos + swapped * sin_signed


def _ret_kernel(*refs, L, seq, layer, use_rope, has_state, want_state, carried):
    it = iter(refs)
    lg_ref = next(it)
    q_ref, k_ref, v_ref, g_ref = next(it), next(it), next(it), next(it)
    if use_rope:
        cos_ref, sin_ref = next(it), next(it)
    if has_state:
        s0_ref = next(it)
    if carried:
        next(it)
    o_ref = next(it)
    if want_state:
        st_view = _state_out_view(next(it), layer, carried)
    oacc = next(it)

    p = pl.program_id(0)
    T = RET_ROWS
    nc = L // T
    ncs = seq // T
    lane = lax.broadcasted_iota(jnp.int32, (1, LANES), 1)
    h0 = lane < HEAD_DIM
    bd = _bd_ones()
    bd_mask = bd > 0

    def lg(direction, head):
        return lg_ref[layer * 2 * N_HEADS + direction * N_HEADS + 2 * p + head]

    i = lax.broadcasted_iota(jnp.int32, (T, T), 0)
    j = lax.broadcasted_iota(jnp.int32, (T, T), 1)
    rel = (i - j).astype(F32)
    masks = []
    for hh in range(2):
        arg = jnp.where(rel > 0, rel * lg(0, hh), -rel * lg(1, hh))
        masks.append(jnp.exp(arg) + jnp.where(rel == 0, 1.0, 0.0))
    kscale = HEAD_DIM ** -0.5
    mcat = (jnp.concatenate(masks, axis=0) * kscale).astype(BF16)

    lgf_vec = jnp.where(h0, lg(0, 0), lg(0, 1))
    lgb_vec = jnp.where(h0, lg(1, 0), lg(1, 1))
    pos = lax.broadcasted_iota(jnp.int32, (T, LANES), 0).astype(F32)
    wf = jnp.exp((pos + 1.0) * lgf_vec)
    wb = jnp.exp((float(T) - pos) * lgb_vec)
    zf = jnp.exp((float(T) - 1.0 - pos) * lgf_vec) * kscale
    zb = jnp.exp(pos * lgb_vec) * kscale
    gf_chunk = jnp.exp(float(T) * lgf_vec)
    gb_chunk = jnp.exp(float(T) * lgb_vec)
    use_states = has_state or ncs > 1
    if use_rope:
        lo_mask = (lane % (HEAD_DIM // 2)) < (HEAD_DIM // 4)

    chunk_rows = [slice(c * T, (c + 1) * T) for c in range(nc)]
    qk = []
    for rows in chunk_rows:
        qc = q_ref[rows, :]
        kc = k_ref[rows, :]
        if use_rope:
            qc = _rope(qc, cos_ref[rows, :], sin_ref[rows, :], lo_mask)
            kc = _rope(kc, cos_ref[rows, :], sin_ref[rows, :], lo_mask)
        qk.append((qc, kc))
    scores = []
    for qc, kc in qk:
        qs = jnp.concatenate([jnp.where(h0, qc, 0.0), jnp.where(h0, 0.0, qc)], axis=0).astype(BF16)
        scores.append(_dot_nt(qs, kc.astype(BF16)))
    dstf, dstb, xis = [], [], []
    if use_states or want_state:
        for rows, (qc, kc) in zip(chunk_rows, qk):
            vt = v_ref[rows, :].T.astype(BF16)
            dstf.append(jnp.where(bd_mask, _dot(vt, (kc * zf).astype(BF16)), 0.0))
            dstb.append(jnp.where(bd_mask, _dot(vt, (kc * zb).astype(BF16)), 0.0))
    if use_states:
        for qc, kc in qk:
            xis.append(jnp.concatenate([qc * wf, qc * wb], axis=1).astype(BF16))
    probs = [s.astype(BF16) * mcat for s in scores]
    outs = [_dot(pm, v_ref[rows, :].astype(BF16)) for pm, rows in zip(probs, chunk_rows)]
    for rows, o2 in zip(chunk_rows, outs):
        oacc[rows, :] = jnp.where(h0, o2[:T], o2[T:])

    stf_in, stb_in = [None] * nc, [None] * nc
    if use_states or want_state:
        for sq in range(nc // ncs):
            chunks = range(sq * ncs, (sq + 1) * ncs)
            if has_state:
                stf, stb = _pair_state_in(s0_ref, 0), _pair_state_in(s0_ref, 1)
            else:
                stf = jnp.zeros((LANES, LANES), F32)
                stb = stf
            for c in chunks:
                stf_in[c] = stf
                stf = stf * gf_chunk + dstf[c]
            for c in reversed(chunks):
                stb_in[c] = stb
                stb = stb * gb_chunk + dstb[c]
            if want_state:
                _pair_state_out(stf, st_view(sq), 0)
                _pair_state_out(stb, st_view(sq), 1)

    outs = [oacc[rows, :] for rows in chunk_rows]
    if use_states:
        inter = [_dot_nt(xis[c], jnp.concatenate([stf_in[c], stb_in[c]], axis=1).astype(BF16))
                 for c in range(nc)]
        outs = [o + x for o, x in zip(outs, inter)]
    means = [_head_sums(o, bd) * (1.0 / HEAD_DIM) for o in outs]
    cent = [o - mu for o, mu in zip(outs, means)]
    variances = [_head_sums_sq(oc, bd) * (1.0 / HEAD_DIM) for oc in cent]
    for rows, oc, var in zip(chunk_rows, cent, variances):
        y = oc * lax.rsqrt(var + LN_EPS)
        o_ref[rows, :] = (y * _silu(g_ref[rows, :])).astype(BF16)


def _mixer_specs(proj, cols, seq, layer, s0, want_state, carried):
    rows = proj.shape[0]
    blk = lambda col: pl.BlockSpec((MIX_ROWS, LANES), lambda p, b, col=col: (b, col + p))
    in_specs = [blk(col) for col in cols]
    state_spec = None
    if s0 is not None:
        state_spec = pl.BlockSpec((None, None, 2, 2, HEAD_DIM, HEAD_DIM),
                                  lambda p, b: (b, layer, 0, p, 0, 0))
    out_specs = [pl.BlockSpec((MIX_ROWS, LANES), lambda p, b: (b, p))]
    out_shape = [jax.ShapeDtypeStruct((rows, HEADS_WIDTH), BF16)]
    if want_state:
        if carried is None:
            out_specs.append(pl.BlockSpec((MIX_ROWS // seq, DEPTH, 2, 2, HEAD_DIM, HEAD_DIM),
                                          lambda p, b: (b, 0, 0, p, 0, 0)))
        else:
            out_specs.append(pl.BlockSpec((MIX_ROWS // seq, None, 2, 2, HEAD_DIM, HEAD_DIM),
                                          lambda p, b: (b, layer, 0, p, 0, 0)))
        out_shape.append(jax.ShapeDtypeStruct((rows // seq, DEPTH, 2, N_HEADS, HEAD_DIM, HEAD_DIM), F32))
    return in_specs, state_spec, out_specs, out_shape


def _carry_states(in_specs, args, carried):
    if carried is None:
        return {}
    in_specs.append(pl.BlockSpec(memory_space=pl.ANY))
    args.append(carried)
    return {len(args) - 1: 1}


def _retention(proj, lg_flat, seq, layer, rope, s0, want_state, carried=None):
    rows = proj.shape[0]
    in_specs, state_spec, out_specs, out_shape = _mixer_specs(
        proj, (COL_RQ, COL_RK, COL_RV, COL_RG), seq, layer, s0, want_state, carried)
    in_specs = [pl.BlockSpec(memory_space=pltpu.SMEM)] + in_specs
    args = [lg_flat, proj, proj, proj, proj]
    if rope is not None:
        in_specs += [pl.BlockSpec((MIX_ROWS, LANES), lambda p, b: (0, 0))] * 2
        args += list(rope)
    if s0 is not None:
        in_specs.append(state_spec)
        args.append(s0)
    aliases = _carry_states(in_specs, args, carried)
    kern = functools.partial(_ret_kernel, L=MIX_ROWS, seq=seq, layer=layer, use_rope=rope is not None,
                             has_state=s0 is not None, want_state=want_state, carried=carried is not None)
    return pl.pallas_call(
        kern,
        grid=(N_PAIRS, rows // MIX_ROWS),
        in_specs=in_specs,
        out_specs=out_specs,
        out_shape=out_shape,
        input_output_aliases=aliases,
        scratch_shapes=[pltpu.VMEM((MIX_ROWS, LANES), F32)],
        compiler_params=pltpu.CompilerParams(
            dimension_semantics=("arbitrary", "arbitrary"), vmem_limit_bytes=VMEM_LIMIT),
        name="retention",
    )(*args)


def _hgrn_gate(x, llb, l1mlb, omlb):
    z = jnp.exp(-jnp.abs(x))
    zp1 = 1.0 + z
    ls = jnp.minimum(x, 0.0) - jnp.log(zp1)
    t = l1mlb + ls
    logf = jnp.maximum(llb, t) + jnp.log(1.0 + jnp.exp(-jnp.abs(llb - t)))
    kk = omlb * jnp.where(x > 0.0, z, 1.0) / zp1
    return logf, kk


def _hgrn_kernel(*refs, L, seq, layer, has_state, want_state, carried):
    it = iter(refs)
    gp_ref = next(it)
    hq_ref, ff_ref, fb_ref, hi_ref, hg_ref = next(it), next(it), next(it), next(it), next(it)
    if has_state:
        s0_ref = next(it)
    if carried:
        next(it)
    o_ref = next(it)
    if want_state:
        st_view = _state_out_view(next(it), layer, carried)
    oacc, xi_ref, dstf_ref, dstb_ref, dec_ref, stcat_ref, lvl_ref = (next(it) for _ in range(7))
    q_s, kf_s, kb_s, bf_s, pb_s = (next(it) for _ in range(5))
    kfs_s, kbs_s = next(it), next(it)

    C = HG_CHUNK
    nc = L // C
    ncs = seq // C
    lane = lax.broadcasted_iota(jnp.int32, (1, LANES), 1)
    h0 = lane < HEAD_DIM
    bd = _bd_ones()
    bd_mask = bd > 0
    rowi = lax.broadcasted_iota(jnp.int32, (C, C), 0)
    coli = lax.broadcasted_iota(jnp.int32, (C, C), 1)
    tri = jnp.where(rowi >= coli, 1.0, 0.0).astype(BF16)
    sub = lax.broadcasted_iota(jnp.int32, (C // SUBLANES, SUBLANES, LANES), 1)
    row2 = lax.broadcasted_iota(jnp.int32, (2 * C, C), 0) % C
    col2 = lax.broadcasted_iota(jnp.int32, (2 * C, C), 1)
    lvl_ref[...] = 31 - lax.clz(row2 ^ col2)

    llb_f, llb_b = gp_ref[0:1, :], gp_ref[1:2, :]
    l1m_f, l1m_b = gp_ref[2:3, :], gp_ref[3:4, :]
    oml_f, oml_b = gp_ref[4:5, :], gp_ref[5:6, :]

    def to3(x):
        return x.reshape(C // SUBLANES, SUBLANES, LANES)

    def prep_gates(c):
        rows = pl.ds(c * C, C)
        q = _silu(hq_ref[rows, :])
        lff, kf = _hgrn_gate(ff_ref[rows, :], llb_f, l1m_f, oml_f)
        lfb, kb = _hgrn_gate(fb_ref[rows, :], llb_b, l1m_b, oml_b)
        lf2 = jnp.concatenate([lff, lfb], axis=1) * LOG2E
        return q, kf, kb, lf2, _split3(lf2)

    def prep_cumsum(vals):
        x1, x2, x3 = vals[4]
        return _dot(tri, x1) + _dot(tri, x2) + _dot(tri, x3)

    def prep_decay(c, gmax, vals, cs):
        rows = pl.ds(c * C, C)
        q, kf, kb, lf2, _ = vals
        bf = cs[:, :LANES]
        bb_incl = cs[:, LANES:]
        pb = bb_incl - lf2[:, LANES:]
        bf_last = bf[C - 1:C, :]
        tot_b = bb_incl[C - 1:C, :]
        q_s[rows, :] = q
        kf_s[rows, :] = kf
        kb_s[rows, :] = kb
        bf_s[rows, :] = bf
        pb_s[rows, :] = pb
        dec_ref[c, 0:1, :] = jnp.exp2(bf_last)
        dec_ref[c, 1:2, :] = jnp.exp2(tot_b)
        xi_ref[rows, 0:LANES] = (q * jnp.exp2(bf)).astype(BF16)
        xi_ref[rows, LANES:] = (q * jnp.exp2(tot_b - pb)).astype(BF16)
        for blk in range(C // HG_BLOCK):
            s, e = blk * HG_BLOCK, (blk + 1) * HG_BLOCK - 1
            gmax = jnp.maximum(gmax, bf[s:s + 1, :] - bf[e:e + 1, :])
            gmax = jnp.maximum(gmax, pb[s:s + 1, :] - pb[e:e + 1, :])
        kfs = (kf * jnp.exp2(bf_last - bf)).astype(BF16)
        kbs = (kb * jnp.exp2(pb)).astype(BF16)
        kfs_s[rows, :] = kfs
        kbs_s[rows, :] = kbs
        vt = hi_ref[rows, :].T.astype(BF16)
        return gmax, (vt, kfs, kbs)

    def prep_states(c, ops):
        vt, kfs, kbs = ops
        dstf_ref[c] = jnp.where(bd_mask, _dot(vt, kfs), 0.0)
        dstb_ref[c] = jnp.where(bd_mask, _dot(vt, kbs), 0.0)

    jm = (lax.broadcasted_iota(jnp.int32, (2 * HG_BLOCK, C), 1)
          - lax.broadcasted_iota(jnp.int32, (2 * HG_BLOCK, C), 0) % HG_BLOCK)

    def fast_pack(c):
        rows = pl.ds(c * C, C)
        q, kf, kb, bf, pb = q_s[rows, :], kf_s[rows, :], kb_s[rows, :], bf_s[rows, :], pb_s[rows, :]
        packed = []
        for blk in range(C // HG_BLOCK):
            s, e = blk * HG_BLOCK, (blk + 1) * HG_BLOCK
            rf = bf[e - 1:e, :]
            rb_ = pb[s:s + 1, :]
            qf = q[s:e, :] * jnp.exp2(bf[s:e, :] - rf)
            qb = q[s:e, :] * jnp.exp2(rb_ - pb[s:e, :])
            x = jnp.concatenate([jnp.where(h0, qf, 0.0), jnp.where(h0, 0.0, qf),
                                 jnp.where(h0, qb, 0.0), jnp.where(h0, 0.0, qb)], axis=0).astype(BF16)
            if e == C:
                kfv = kfs_s[rows, :]
            else:
                kfv = (kf[:e, :] * jnp.exp2(rf - bf[:e, :])).astype(BF16)
            if s == 0:
                kbv = kbs_s[rows, :]
            else:
                kbv = (kb[s:, :] * jnp.exp2(pb[s:, :] - rb_)).astype(BF16)
            parts = [kfv]
            if e < C:
                parts.append(jnp.zeros((C - e, LANES), BF16))
            if s > 0:
                parts.append(jnp.zeros((s, LANES), BF16))
            parts.append(kbv)
            packed.append((x, jnp.concatenate(parts, axis=0)))
        return packed

    def fast_scores(packed):
        return [_dot_nt(x, w) for x, w in packed]

    def fast_mask(scores):
        a_rows = []
        for blk, sc in enumerate(scores):
            s = blk * HG_BLOCK
            af = sc[:2 * HG_BLOCK, :C]
            ab = sc[2 * HG_BLOCK:, C:]
            a_rows.append(jnp.where(jm < s, af, jnp.where(jm > s, ab, af + ab)))
        return jnp.concatenate(a_rows, axis=0).astype(BF16)

    def fast_store(c, o2):
        o_blocks = []
        for blk in range(C // HG_BLOCK):
            base = blk * 2 * HG_BLOCK
            o_blocks.append(jnp.where(h0, o2[base:base + HG_BLOCK], o2[base + HG_BLOCK:base + 2 * HG_BLOCK]))
        oacc[pl.ds(c * C, C), :] = jnp.concatenate(o_blocks, axis=0)

    def exact_body(c, carry):
        rows = pl.ds(pl.multiple_of(c * C, C), C)
        q, kf, kb, bf, pb = q_s[rows, :], kf_s[rows, :], kb_s[rows, :], bf_s[rows, :], pb_s[rows, :]
        v = hi_ref[rows, :]
        vb = v.astype(BF16)

        d0 = (q * (kf + kb)).astype(BF16)
        o = _dot(d0, bd) * v
        q3, v3, bf3, pb3, kf3, kb3 = to3(q), to3(v), to3(bf), to3(pb), to3(kf), to3(kb)
        nbf3, npb3 = -bf3, -pb3
        for dl in range(1, SUBLANES):
            src_fw = sub < (SUBLANES - dl)
            u = jnp.where(src_fw, nbf3, pb3)
            kmix = jnp.where(src_fw, kf3, kb3)
            w = jnp.where(sub >= dl, bf3, npb3)
            rel = w + pltpu.roll(u, dl, axis=1)
            d = (q3 * pltpu.roll(kmix, dl, axis=1) * jnp.exp2(rel)).reshape(C, LANES).astype(BF16)
            o = o + _dot(d, bd) * pltpu.roll(v3, dl, axis=1).reshape(C, LANES)

        lvl = lvl_ref[...]
        a_tot = jnp.zeros((2 * C, C), F32)
        m = SUBLANES
        while m < C:
            nblk = C // (2 * m)

            def halves(x):
                x4 = x.reshape(nblk, 2, m, LANES)
                return x4[:, 0], x4[:, 1]

            def join(lo, up):
                return jnp.stack([lo, up], axis=1).reshape(C, LANES)

            q_lo, q_up = halves(q)
            kf_lo, _ = halves(kf)
            _, kb_up = halves(kb)
            bf_lo, bf_up = halves(bf)
            pb_lo, pb_up = halves(pb)
            rf = bf_lo[:, m - 1:m, :]
            rb_ = pb_up[:, 0:1, :]
            xq = join(q_lo * jnp.exp2(rb_ - pb_lo), q_up * jnp.exp2(bf_up - rf))
            yk = join(kf_lo * jnp.exp2(rf - bf_lo), kb_up * jnp.exp2(pb_up - rb_))
            xs = jnp.concatenate([jnp.where(h0, xq, 0.0), jnp.where(h0, 0.0, xq)], axis=0).astype(BF16)
            a = _dot_nt(xs, yk.astype(BF16))
            a_tot = jnp.where(lvl == (m.bit_length() - 1), a, a_tot)
            m *= 2
        o2 = _dot(a_tot.astype(BF16), vb)
        o = o + jnp.where(h0, o2[:C], o2[C:])
        oacc[rows, :] = o
        return carry

    gmax = jnp.zeros((1, LANES), F32)
    for g0 in range(0, nc, HG_GROUP):
        grp = range(g0, g0 + HG_GROUP)
        vals = [prep_gates(c) for c in grp]
        sums = [prep_cumsum(vv) for vv in vals]
        ops = []
        for c, vv, cs in zip(grp, vals, sums):
            gmax, op = prep_decay(c, gmax, vv, cs)
            ops.append(op)
        for c, op in zip(grp, ops):
            prep_states(c, op)
    for g0 in range(0, nc, HG_GROUP):
        grp = range(g0, g0 + HG_GROUP)
        packed = [fast_pack(c) for c in grp]
        scores = [fast_scores(pk) for pk in packed]
        amats = [fast_mask(sc) for sc in scores]
        outs = [_dot(a, hi_ref[pl.ds(c * C, C), :].astype(BF16)) for a, c in zip(amats, grp)]
        for c, o2 in zip(grp, outs):
            fast_store(c, o2)

    for sq in range(nc // ncs):
        chunks = range(sq * ncs, (sq + 1) * ncs)
        if has_state:
            stf, stb = _pair_state_in(s0_ref, 0), _pair_state_in(s0_ref, 1)
        else:
            stf = jnp.zeros((LANES, LANES), F32)
            stb = stf
        for c in chunks:
            stcat_ref[c, :, 0:LANES] = stf.astype(BF16)
            stf = stf * dec_ref[c, 0:1, :] + dstf_ref[c]
        for c in reversed(chunks):
            stcat_ref[c, :, LANES:] = stb.astype(BF16)
            stb = stb * dec_ref[c, 1:2, :] + dstb_ref[c]
        if want_state:
            _pair_state_out(stf, st_view(sq), 0)
            _pair_state_out(stb, st_view(sq), 1)

    def finish():
        for g0 in range(0, nc, HG_GROUP):
            grp = range(g0, g0 + HG_GROUP)
            rows = [slice(c * C, (c + 1) * C) for c in grp]
            inter = [_dot_nt(xi_ref[r, :], stcat_ref[c]) for c, r in zip(grp, rows)]
            outs = [oacc[r, :] + x for r, x in zip(rows, inter)]
            sums = [_head_sums_sq(o, bd) for o in outs]
            for r, o, ms in zip(rows, outs, sums):
                y = o * lax.rsqrt(ms * (1.0 / HEAD_DIM) + LN_EPS)
                o_ref[r, :] = (y * _silu(hg_ref[r, :])).astype(BF16)

    finish()

    @pl.when(jnp.logical_not(jnp.max(gmax) < HG_SAFE_LOG2))
    def _():
        lax.fori_loop(0, nc, exact_body, 0)
        finish()


def _hgrn(proj, gate_params, seq, layer, s0, want_state, carried=None):
    rows = proj.shape[0]
    L = MIX_ROWS
    nc = L // HG_CHUNK
    in_specs, state_spec, out_specs, out_shape = _mixer_specs(
        proj, (COL_HQ, COL_HFF, COL_HFB, COL_HI, COL_HG), seq, layer, s0, want_state, carried)
    in_specs = [pl.BlockSpec((None, None, SUBLANES, LANES), lambda p, b: (layer, p, 0, 0))] + in_specs
    args = [gate_params, proj, proj, proj, proj, proj]
    if s0 is not None:
        in_specs.append(state_spec)
        args.append(s0)
    aliases = _carry_states(in_specs, args, carried)
    kern = functools.partial(_hgrn_kernel, L=L, seq=seq, layer=layer, has_state=s0 is not None,
                             want_state=want_state, carried=carried is not None)
    return pl.pallas_call(
        kern,
        grid=(N_PAIRS, rows // L),
        in_specs=in_specs,
        out_specs=out_specs,
        out_shape=out_shape,
        input_output_aliases=aliases,
        scratch_shapes=[pltpu.VMEM((L, LANES), F32),
                        pltpu.VMEM((L, 2 * LANES), BF16),
                        pltpu.VMEM((nc, LANES, LANES), F32),
                        pltpu.VMEM((nc, LANES, LANES), F32),
                        pltpu.VMEM((nc, SUBLANES, LANES), F32),
                        pltpu.VMEM((nc, LANES, 2 * LANES), BF16),
                        pltpu.VMEM((2 * HG_CHUNK, HG_CHUNK), jnp.int32)]
                       + [pltpu.VMEM((L, LANES), F32)] * 5
                       + [pltpu.VMEM((L, LANES), BF16)] * 2,
        compiler_params=pltpu.CompilerParams(
            dimension_semantics=("arbitrary", "arbitrary"), vmem_limit_bytes=VMEM_LIMIT),
        name="hgrn2",
    )(*args)


def _ffn_kernel(x_ref, of_ref, or_ref, oh_ref, mod_ref, wo_ref, lng_ref, lnb_ref,
                wg_ref, wu_ref, wd_ref, *rest, n_side):
    side_in, y_ref, side_out = rest[:n_side], rest[n_side], rest[n_side + 1:]
    g1 = mod_ref[:, 2 * D_MODEL:3 * D_MODEL]
    sh2 = mod_ref[:, 3 * D_MODEL:4 * D_MODEL]
    sc2 = mod_ref[:, 4 * D_MODEL:5 * D_MODEL]
    g2 = mod_ref[:, 5 * D_MODEL:6 * D_MODEL]
    half = ROW_TILE // 2
    halves = (slice(0, half), slice(half, ROW_TILE))
    mix = [_dot(jnp.concatenate([of_ref[r, :], or_ref[r, :], oh_ref[r, :]], axis=1), wo_ref[...])
           for r in halves]
    x1, gate, up, ffn = [], [], [], []
    for r, m in zip(halves, mix):
        x1.append(_ln(ALPHA * x_ref[r, :] + g1 * m) * lng_ref[0:1, :] + lnb_ref[0:1, :])
        h2 = (_ln(x1[-1]) * (1.0 + sc2) + sh2).astype(BF16)
        gate.append(_dot(h2, wg_ref[...]))
        up.append(_dot(h2, wu_ref[...]))
    for g, u in zip(gate, up):
        act = (_silu(g) * u).astype(BF16)
        ffn.append(_dot(act, wd_ref[...]))
    for r, x, f in zip(halves, x1, ffn):
        y_ref[r, :] = _ln(ALPHA * x + g2 * f) * lng_ref[1:2, :] + lnb_ref[1:2, :]
    _side_casts(side_in + side_out)


def _out_ffn(x2d, o_f, o_r, o_h, mod, w_out_bf, ln_g, ln_b, wg_bf, wu_bf, wd_bf, layer, mod_row, side=()):
    rows = x2d.shape[0]
    nsteps = rows // ROW_TILE
    s_in, s_out, s_shape, s_args = _side_cast_specs(side, nsteps)
    once = pl.Buffered(1)
    row_blk = lambda w: pl.BlockSpec((ROW_TILE, w), lambda i: (i, 0))
    return pl.pallas_call(
        functools.partial(_ffn_kernel, n_side=len(side)),
        grid=(nsteps,),
        in_specs=[row_blk(D_MODEL), row_blk(FNET_WIDTH), row_blk(HEADS_WIDTH), row_blk(HEADS_WIDTH),
                  pl.BlockSpec((None, None, 1, N_MOD * D_MODEL), lambda i: (layer, mod_row(i), 0, 0)),
                  pl.BlockSpec((D_MODEL, D_MODEL), lambda i: (0, 0), pipeline_mode=once),
                  pl.BlockSpec((None, 2, D_MODEL), lambda i: (layer, 0, 0)),
                  pl.BlockSpec((None, 2, D_MODEL), lambda i: (layer, 0, 0)),
                  pl.BlockSpec((D_MODEL, D_FF), lambda i: (0, 0), pipeline_mode=once),
                  pl.BlockSpec((D_MODEL, D_FF), lambda i: (0, 0), pipeline_mode=once),
                  pl.BlockSpec((D_FF, D_MODEL), lambda i: (0, 0), pipeline_mode=once)] + s_in,
        out_specs=[row_blk(D_MODEL)] + s_out,
        out_shape=[jax.ShapeDtypeStruct((rows, D_MODEL), F32)] + s_shape,
        compiler_params=pltpu.CompilerParams(
            dimension_semantics=("arbitrary",), vmem_limit_bytes=VMEM_LIMIT),
        name="out_ffn",
    )(x2d, o_f, o_r, o_h, mod, w_out_bf, ln_g, ln_b, wg_bf, wu_bf, wd_bf, *s_args)


def kernel(x_prompt, x_sample, c, state_ret, state_hgrn, c_ctx, w_mod, b_mod, w_in, w_out,
           ret_log_decay, hg_lower_bound, ln_g, ln_b, w_gate, w_up, w_down):
    B, S, _ = x_prompt.shape
    DB, DS, _ = x_sample.shape

    p = jax.nn.softmax(hg_lower_bound.astype(F32), axis=1)
    cum = jnp.cumsum(p, axis=1)
    lbs = cum - cum[:, :1]
    kinds = jnp.stack([jnp.log(lbs), jnp.log1p(-lbs), 1.0 - lbs], axis=0)
    gp = jnp.transpose(kinds.reshape(3, 2, DEPTH, N_PAIRS, LANES), (2, 3, 0, 1, 4))
    gp = gp.reshape(DEPTH, N_PAIRS, 6, LANES)
    gp = jnp.concatenate([gp, jnp.zeros((DEPTH, N_PAIRS, SUBLANES - 6, LANES), F32)], axis=2)
    lg_flat = (-jnp.exp(ret_log_decay.astype(F32))).reshape(-1)

    cv =jnp.concatenate([c_ctx[None, :], c, jnp.zeros((MOD_ROWS - 1 - DB, D_MODEL), F32)], axis=0)
    mod = _modulation(cv, w_mod, b_mod).reshape(DEPTH, MOD_ROWS, 1, N_MOD * D_MODEL)

    rope = _rope_tables(DS)

    ctx_row = lambda i: 0
    smp_row = lambda i: 1 + i // (DS // ROW_TILE)

    y = x_prompt.reshape(B * S, D_MODEL)
    z = x_sample.reshape(DB * DS, D_MODEL)
    new_state_ret, new_state_hgrn = None, None
    w_in_bf = w_in[0].astype(BF16)
    w_out_bf = wg_bf = wu_bf = wd_bf = None
    for l in range(DEPTH):
        side_c = [(w_out, l), (w_gate, l)] if l == 0 else []
        side_s = [(w_up, l), (w_down, l)] if l == 0 else []
        proj_c, *cast_c = _inproj(y, mod, w_in_bf, l, ctx_row, side_c)
        proj_s, *cast_s = _inproj(z, mod, w_in_bf, l, smp_row, side_s)
        if l == 0:
            (w_out_bf, wg_bf), (wu_bf, wd_bf) = cast_c, cast_s
        of_c = _fnet(proj_c, S)
        or_c, new_state_ret = _retention(proj_c, lg_flat, S, l, None, None, True, new_state_ret)
        oh_c, new_state_hgrn = _hgrn(proj_c, gp, S, l, None, True, new_state_hgrn)
        of_s = _fnet(proj_s, DS)
        (or_s,) = _retention(proj_s, lg_flat, DS, l, rope, state_ret, False)
        (oh_s,) = _hgrn(proj_s, gp, DS, l, state_hgrn, False)
        last = l + 1 == DEPTH
        side_c = [] if last else [(w_in, l + 1), (w_out, l + 1)]
        side_s = [] if last else [(w_gate, l + 1), (w_up, l + 1), (w_down, l + 1)]
        weights = (w_out_bf, ln_g, ln_b, wg_bf, wu_bf, wd_bf)
        y, *cast_c = _out_ffn(y, of_c, or_c, oh_c, mod, *weights, l, ctx_row, side_c)
        z, *cast_s = _out_ffn(z, of_s, or_s, oh_s, mod, *weights, l, smp_row, side_s)
        if not last:
            (w_in_bf, w_out_bf), (wg_bf, wu_bf, wd_bf) = cast_c, cast_s

    return (y.reshape(B, S, D_MODEL), z.reshape(DB, DS, D_MODEL), new_state_ret, new_state_hgrn)
```

```python
import functools

import numpy as np
import jax
import jax.numpy as jnp
from jax import lax
from jax.experimental import pallas as pl
from jax.experimental.pallas import tpu as pltpu

F32 = jnp.float32
BF16 = jnp.bfloat16

D_MODEL = 1024
DEPTH = 2
GRID_W = 64
FNET_WIDTH = 256
FNET_GDIM = 64
N_HEADS = 6
N_PAIRS = N_HEADS // 2
HEAD_DIM = 64
LANES = 128
SUBLANES = 8
HEADS_WIDTH = N_HEADS * HEAD_DIM
IN_WIDTH = 3712
D_FF = 2816
ROPE_BASE = 10000.0
LN_EPS = 1e-5
LOG2E = 1.4426950408889634
N_MOD = 6
ALPHA = (2 * DEPTH) ** 0.25
MOD_ROWS = 8
ROW_TILE = 512
RET_ROWS = 256
MIX_ROWS = 1024
HG_CHUNK = 128
HG_GROUP = 8
HG_BLOCK = 32
HG_SAFE_LOG2 = 96.0
VMEM_LIMIT = 56 * 1024 * 1024

COL_RQ, COL_RK, COL_RV, COL_RG = 2, 5, 8, 11
COL_HQ, COL_HFF, COL_HFB, COL_HI, COL_HG = 14, 17, 20, 23, 26


def _silu(x):
    return x / (1.0 + jnp.exp(-x))


def _ln(x):
    mu = jnp.mean(x, axis=-1, keepdims=True)
    xc = x - mu
    var = jnp.mean(xc * xc, axis=-1, keepdims=True)
    return xc * lax.rsqrt(var + LN_EPS)


def _dot(a, b):
    return jnp.dot(a, b, preferred_element_type=F32)


def _dot_nt(a, b):
    return lax.dot_general(a, b, (((1,), (1,)), ((), ())), preferred_element_type=F32)


def _split2(x):
    hi = x.astype(BF16)
    lo = (x - hi.astype(F32)).astype(BF16)
    return hi, lo


def _split3(x):
    x1 = x.astype(BF16)
    r1 = x - x1.astype(F32)
    x2 = r1.astype(BF16)
    x3 = (r1 - x2.astype(F32)).astype(BF16)
    return x1, x2, x3


def _head_sums(x, bd_ones):
    return _dot(x.astype(BF16), bd_ones)


def _head_sums_sq(x, bd_ones):
    return _head_sums(x * x, bd_ones)


def _bd_ones():
    r = lax.broadcasted_iota(jnp.int32, (LANES, LANES), 0)
    c = lax.broadcasted_iota(jnp.int32, (LANES, LANES), 1)
    return jnp.where((r < HEAD_DIM) == (c < HEAD_DIM), 1.0, 0.0).astype(BF16)


def _dot3_nt(sel, x):
    return sum(_dot_nt(sel, part) for part in _split3(x))


def _pair_state_in(s_ref, direction):
    z = jnp.zeros((HEAD_DIM, HEAD_DIM), F32)
    r = lax.broadcasted_iota(jnp.int32, (LANES, HEAD_DIM), 0)
    c = lax.broadcasted_iota(jnp.int32, (LANES, HEAD_DIM), 1)
    out = None
    for hh in range(2):
        s = s_ref[direction, hh]
        rows = jnp.concatenate([s, z] if hh == 0 else [z, s], axis=0)
        place = jnp.where(r == c + hh * HEAD_DIM, 1.0, 0.0).astype(BF16)
        term = _dot3_nt(place, rows)
        out = term if out is None else out + term
    return out


def _state_out_view(st_ref, layer, carried):
    if carried:
        return lambda sq: st_ref.at[sq]
    for other in range(DEPTH):
        if other != layer:
            st_ref[:, other] = jnp.zeros(st_ref.shape[:1] + st_ref.shape[2:], F32)
    return lambda sq: st_ref.at[sq, layer]


def _pair_state_out(st, st_ref, direction):
    r = lax.broadcasted_iota(jnp.int32, (HEAD_DIM, LANES), 0)
    c = lax.broadcasted_iota(jnp.int32, (HEAD_DIM, LANES), 1)
    for hh in range(2):
        pick = jnp.where(c == r + hh * HEAD_DIM, 1.0, 0.0).astype(BF16)
        st_ref[direction, hh] = _dot3_nt(pick, st[hh * HEAD_DIM:(hh + 1) * HEAD_DIM, :])


def _mod_kernel(cv_ref, w_ref, b_ref, o_ref):
    s = _silu(cv_ref[...]).astype(BF16)
    bias = b_ref[pl.ds(pl.program_id(0), 1), :]
    o_ref[...] = _dot(s, w_ref[...].astype(BF16)) + bias


def _modulation(cv, w_mod, b_mod):
    tn = 3072
    width = N_MOD * D_MODEL
    return pl.pallas_call(
        _mod_kernel,
        grid=(DEPTH, width // tn),
        in_specs=[pl.BlockSpec((MOD_ROWS, D_MODEL), lambda l, j: (0, 0)),
                  pl.BlockSpec((None, D_MODEL, tn), lambda l, j: (l, 0, j)),
                  pl.BlockSpec((DEPTH, tn), lambda l, j: (0, j))],
        out_specs=pl.BlockSpec((None, MOD_ROWS, tn), lambda l, j: (l, 0, j)),
        out_shape=jax.ShapeDtypeStruct((DEPTH, MOD_ROWS, width), F32),
        compiler_params=pltpu.CompilerParams(
            dimension_semantics=("arbitrary", "arbitrary"), vmem_limit_bytes=VMEM_LIMIT),
        name="modulation",
    )(cv, w_mod, b_mod)


def _side_cast_specs(side, nsteps):
    in_specs, out_specs, out_shape, args = [], [], [], []
    for w, layer in side:
        _, r, c = w.shape
        in_specs.append(pl.BlockSpec((None, r // nsteps, c), lambda i, layer=layer: (layer, i, 0)))
        out_specs.append(pl.BlockSpec((r // nsteps, c), lambda i: (i, 0)))
        out_shape.append(jax.ShapeDtypeStruct((r, c), BF16))
        args.append(w)
    return in_specs, out_specs, out_shape, args


def _side_casts(side_refs):
    n = len(side_refs) // 2
    for src, dst in zip(side_refs[:n], side_refs[n:]):
        dst[...] = src[...].astype(BF16)


def _mod_vectors(mod_ref, mod_row):
    m = mod_ref[pl.ds(mod_row(pl.program_id(0)), 1), :]
    return [m[:, k * D_MODEL:(k + 1) * D_MODEL] for k in range(N_MOD)]


def _inproj_kernel(x_ref, mod_ref, w_ref, *rest, n_side, mod_row):
    side_in, o_ref, side_out = rest[:n_side], rest[n_side], rest[n_side + 1:]
    sh, sc = _mod_vectors(mod_ref, mod_row)[0:2]
    half = ROW_TILE // 2
    for r in (slice(0, half), slice(half, ROW_TILE)):
        h = _ln(x_ref[r, :]) * (1.0 + sc) + sh
        o_ref[r, :] = _dot(h.astype(BF16), w_ref[...])
    _side_casts(side_in + side_out)


def _inproj(x2d, mod, w_in_bf, layer, mod_row, side=()):
    rows = x2d.shape[0]
    nsteps = rows // ROW_TILE
    s_in, s_out, s_shape, s_args = _side_cast_specs(side, nsteps)
    return pl.pallas_call(
        functools.partial(_inproj_kernel, n_side=len(side), mod_row=mod_row),
        grid=(nsteps,),
        in_specs=[pl.BlockSpec((ROW_TILE, D_MODEL), lambda i: (i, 0)),
                  pl.BlockSpec((None, MOD_ROWS, N_MOD * D_MODEL), lambda i: (layer, 0, 0)),
                  pl.BlockSpec((D_MODEL, IN_WIDTH), lambda i: (0, 0))] + s_in,
        out_specs=[pl.BlockSpec((ROW_TILE, IN_WIDTH), lambda i: (i, 0))] + s_out,
        out_shape=[jax.ShapeDtypeStruct((rows, IN_WIDTH), F32)] + s_shape,
        compiler_params=pltpu.CompilerParams(
            dimension_semantics=("arbitrary",), vmem_limit_bytes=VMEM_LIMIT),
        name="inproj",
    )(x2d, mod, w_in_bf, *s_args)


def _fnet_kernel(u_ref, w1_ref, cl_ref, sl_ref, o_ref, *, seq):
    u = u_ref[...].astype(BF16)
    t = _dot(u, w1_ref[...])
    uc = t[:, :FNET_WIDTH].astype(BF16)
    us = t[:, FNET_WIDTH:].astype(BF16)
    for s in range(u.shape[0] // seq):
        rows = slice(s * seq, (s + 1) * seq)
        o = _dot(cl_ref[...], uc[rows, :]) - _dot(sl_ref[...], us[rows, :])
        o_ref[rows, :] = o.astype(BF16)


def _fnet_consts(L):
    scale = 1.0 / np.sqrt(L * FNET_GDIM)
    k = np.arange(FNET_GDIM)
    ang = 2.0 * np.pi * ((k[:, None] * k[None, :]) % FNET_GDIM) / FNET_GDIM
    eye = np.eye(FNET_WIDTH // FNET_GDIM)
    w1 = np.concatenate([np.kron(eye, np.cos(ang)), np.kron(eye, np.sin(ang))], axis=1) * scale
    n = np.arange(L)
    angl = 2.0 * np.pi * ((n[:, None] * n[None, :]) % L) / L
    tables = (w1, np.cos(angl), np.sin(angl))
    return tuple(jnp.asarray(t, F32).astype(BF16) for t in tables)


def _fnet(proj, L):
    rows = proj.shape[0]
    w1, cl, sl = _fnet_consts(L)
    return pl.pallas_call(
        functools.partial(_fnet_kernel, seq=L),
        grid=(rows // MIX_ROWS,),
        in_specs=[pl.BlockSpec((MIX_ROWS, FNET_WIDTH), lambda b: (b, 0)),
                  pl.BlockSpec((FNET_WIDTH, 2 * FNET_WIDTH), lambda b: (0, 0)),
                  pl.BlockSpec((L, L), lambda b: (0, 0)),
                  pl.BlockSpec((L, L), lambda b: (0, 0))],
        out_specs=pl.BlockSpec((MIX_ROWS, FNET_WIDTH), lambda b: (b, 0)),
        out_shape=jax.ShapeDtypeStruct((rows, FNET_WIDTH), BF16),
        compiler_params=pltpu.CompilerParams(
            dimension_semantics=("arbitrary",), vmem_limit_bytes=VMEM_LIMIT),
        name="fnet",
    )(proj, w1, cl, sl)


def _rope_tables(L):
    half = HEAD_DIM // 2
    inv = ROPE_BASE ** (-np.arange(0, half, 2, dtype=np.float64) / half)
    n = np.arange(L)
    rows, cols = n // GRID_W, n % GRID_W
    lane = np.arange(LANES)
    f = lane % (half // 2)
    use_col = (lane % HEAD_DIM) >= half
    pos = np.where(use_col[None, :], cols[:, None], rows[:, None]).astype(np.float64)
    ang = pos * inv[f][None, :]
    lo = (lane % half) < (half // 2)
    cos = np.cos(ang)
    sin = np.where(lo[None, :], -np.sin(ang), np.sin(ang))
    return jnp.asarray(cos, F32), jnp.asarray(sin, F32)


def _rope(x, cos, sin_signed, lo_mask):
    quarter = HEAD_DIM // 4
    swapped = jnp.where(lo_mask, pltpu.roll(x, LANES - quarter, axis=1), pltpu.roll(x, quarter, axis=1))
    return x * cos + swapped * sin_signed


def _ret_kernel(*refs, L, seq, layer, use_rope, has_state, want_state, carried):
    it = iter(refs)
    lg_ref = next(it)
    q_ref, k_ref, v_ref, g_ref = next(it), next(it), next(it), next(it)
    if use_rope:
        cos_ref, sin_ref = next(it), next(it)
    if has_state:
        s0_ref = next(it)
    if carried:
        next(it)
    o_ref = next(it)
    if want_state:
        st_view = _state_out_view(next(it), layer, carried)
    oacc = next(it)

    p = pl.program_id(0)
    T = RET_ROWS
    nc = L // T
    ncs = seq // T
    lane = lax.broadcasted_iota(jnp.int32, (1, LANES), 1)
    h0 = lane < HEAD_DIM
    bd = _bd_ones()
    bd_mask = bd > 0

    def lg(direction, head):
        return lg_ref[layer * 2 * N_HEADS + direction * N_HEADS + 2 * p + head]

    i = lax.broadcasted_iota(jnp.int32, (T, T), 0)
    j = lax.broadcasted_iota(jnp.int32, (T, T), 1)
    rel = (i - j).astype(F32)
    masks = []
    for hh in range(2):
        arg = jnp.where(rel > 0, rel * lg(0, hh), -rel * lg(1, hh))
        masks.append(jnp.exp(arg) + jnp.where(rel == 0, 1.0, 0.0))
    kscale = HEAD_DIM ** -0.5
    mcat = (jnp.concatenate(masks, axis=0) * kscale).astype(BF16)

    lgf_vec = jnp.where(h0, lg(0, 0), lg(0, 1))
    lgb_vec = jnp.where(h0, lg(1, 0), lg(1, 1))
    pos = lax.broadcasted_iota(jnp.int32, (T, LANES), 0).astype(F32)
    wf = jnp.exp((pos + 1.0) * lgf_vec)
    wb = jnp.exp((float(T) - pos) * lgb_vec)
    zf = jnp.exp((float(T) - 1.0 - pos) * lgf_vec) * kscale
    zb = jnp.exp(pos * lgb_vec) * kscale
    gf_chunk = jnp.exp(float(T) * lgf_vec)
    gb_chunk = jnp.exp(float(T) * lgb_vec)
    use_states = has_state or ncs > 1
    if use_rope:
        lo_mask = (lane % (HEAD_DIM // 2)) < (HEAD_DIM // 4)

    chunk_rows = [slice(c * T, (c + 1) * T) for c in range(nc)]
    qk = []
    for rows in chunk_rows:
        qc = q_ref[rows, :]
        kc = k_ref[rows, :]
        if use_rope:
            qc = _rope(qc, cos_ref[rows, :], sin_ref[rows, :], lo_mask)
            kc = _rope(kc, cos_ref[rows, :], sin_ref[rows, :], lo_mask)
        qk.append((qc, kc))
    scores = []
    for qc, kc in qk:
        qs = jnp.concatenate([jnp.where(h0, qc, 0.0), jnp.where(h0, 0.0, qc)], axis=0).astype(BF16)
        scores.append(_dot_nt(qs, kc.astype(BF16)))
    dstf, dstb, xis = [], [], []
    if use_states or want_state:
        for rows, (qc, kc) in zip(chunk_rows, qk):
            vt = v_ref[rows, :].T.astype(BF16)
            dstf.append(jnp.where(bd_mask, _dot(vt, (kc * zf).astype(BF16)), 0.0))
            dstb.append(jnp.where(bd_mask, _dot(vt, (kc * zb).astype(BF16)), 0.0))
    if use_states:
        for qc, kc in qk:
            xis.append(jnp.concatenate([qc * wf, qc * wb], axis=1).astype(BF16))
    probs = [s.astype(BF16) * mcat for s in scores]
    outs = [_dot(pm, v_ref[rows, :].astype(BF16)) for pm, rows in zip(probs, chunk_rows)]
    for rows, o2 in zip(chunk_rows, outs):
        oacc[rows, :] = jnp.where(h0, o2[:T], o2[T:])

    stf_in, stb_in = [None] * nc, [None] * nc
    if use_states or want_state:
        for sq in range(nc // ncs):
            chunks = range(sq * ncs, (sq + 1) * ncs)
            if has_state:
                stf, stb = _pair_state_in(s0_ref, 0), _pair_state_in(s0_ref, 1)
            else:
                stf = jnp.zeros((LANES, LANES), F32)
                stb = stf
            for c in chunks:
                stf_in[c] = stf
                stf = stf * gf_chunk + dstf[c]
            for c in reversed(chunks):
                stb_in[c] = stb
                stb = stb * gb_chunk + dstb[c]
            if want_state:
                _pair_state_out(stf, st_view(sq), 0)
                _pair_state_out(stb, st_view(sq), 1)

    outs = [oacc[rows, :] for rows in chunk_rows]
    if use_states:
        inter = [_dot_nt(xis[c], jnp.concatenate([stf_in[c], stb_in[c]], axis=1).astype(BF16))
                 for c in range(nc)]
        outs = [o + x for o, x in zip(outs, inter)]
    means = [_head_sums(o, bd) * (1.0 / HEAD_DIM) for o in outs]
    cent = [o - mu for o, mu in zip(outs, means)]
    variances = [_head_sums_sq(oc, bd) * (1.0 / HEAD_DIM) for oc in cent]
    for rows, oc, var in zip(chunk_rows, cent, variances):
        y = oc * lax.rsqrt(var + LN_EPS)
        o_ref[rows, :] = (y * _silu(g_ref[rows, :])).astype(BF16)


def _mixer_specs(proj, cols, seq, layer, s0, want_state, carried):
    rows = proj.shape[0]
    blk = lambda col: pl.BlockSpec((MIX_ROWS, LANES), lambda p, b, col=col: (b, col + p))
    in_specs = [blk(col) for col in cols]
    state_spec = None
    if s0 is not None:
        state_spec = pl.BlockSpec((None, None, 2, 2, HEAD_DIM, HEAD_DIM),
                                  lambda p, b: (b, layer, 0, p, 0, 0))
    out_specs = [pl.BlockSpec((MIX_ROWS, LANES), lambda p, b: (b, p))]
    out_shape = [jax.ShapeDtypeStruct((rows, HEADS_WIDTH), BF16)]
    if want_state:
        if carried is None:
            out_specs.append(pl.BlockSpec((MIX_ROWS // seq, DEPTH, 2, 2, HEAD_DIM, HEAD_DIM),
                                          lambda p, b: (b, 0, 0, p, 0, 0)))
        else:
            out_specs.append(pl.BlockSpec((MIX_ROWS // seq, None, 2, 2, HEAD_DIM, HEAD_DIM),
                                          lambda p, b: (b, layer, 0, p, 0, 0)))
        out_shape.append(jax.ShapeDtypeStruct((rows // seq, DEPTH, 2, N_HEADS, HEAD_DIM, HEAD_DIM), F32))
    return in_specs, state_spec, out_specs, out_shape


def _carry_states(in_specs, args, carried):
    if carried is None:
        return {}
    in_specs.append(pl.BlockSpec(memory_space=pl.ANY))
    args.append(carried)
    return {len(args) - 1: 1}


def _retention(proj, lg_flat, seq, layer, rope, s0, want_state, carried=None):
    rows = proj.shape[0]
    in_specs, state_spec, out_specs, out_shape = _mixer_specs(
        proj, (COL_RQ, COL_RK, COL_RV, COL_RG), seq, layer, s0, want_state, carried)
    in_specs = [pl.BlockSpec(memory_space=pltpu.SMEM)] + in_specs
    args = [lg_flat, proj, proj, proj, proj]
    if rope is not None:
        in_specs += [pl.BlockSpec((MIX_ROWS, LANES), lambda p, b: (0, 0))] * 2
        args += list(rope)
    if s0 is not None:
        in_specs.append(state_spec)
        args.append(s0)
    aliases = _carry_states(in_specs, args, carried)
    kern = functools.partial(_ret_kernel, L=MIX_ROWS, seq=seq, layer=layer, use_rope=rope is not None,
                             has_state=s0 is not None, want_state=want_state, carried=carried is not None)
    return pl.pallas_call(
        kern,
        grid=(N_PAIRS, rows // MIX_ROWS),
        in_specs=in_specs,
        out_specs=out_specs,
        out_shape=out_shape,
        input_output_aliases=aliases,
        scratch_shapes=[pltpu.VMEM((MIX_ROWS, LANES), F32)],
        compiler_params=pltpu.CompilerParams(
            dimension_semantics=("arbitrary", "arbitrary"), vmem_limit_bytes=VMEM_LIMIT),
        name="retention",
    )(*args)


def _hgrn_gate(x, llb, l1mlb, omlb, zero_bound):
    z = jnp.exp(-jnp.abs(x))
    zp1 = 1.0 + z
    ls = jnp.minimum(x, 0.0) - jnp.log(zp1)
    sig_neg = jnp.where(x > 0.0, z, 1.0) / zp1
    if zero_bound:
        return ls, sig_neg
    t = l1mlb + ls
    logf = jnp.maximum(llb, t) + jnp.log(1.0 + jnp.exp(-jnp.abs(llb - t)))
    return logf, omlb * sig_neg


def _hgrn_kernel(*refs, L, seq, layer, has_state, want_state, carried):
    it = iter(refs)
    gp_ref = next(it)
    hq_ref, ff_ref, fb_ref, hi_ref, hg_ref = next(it), next(it), next(it), next(it), next(it)
    if has_state:
        s0_ref = next(it)
    if carried:
        next(it)
    o_ref = next(it)
    if want_state:
        st_view = _state_out_view(next(it), layer, carried)
    oacc, xi_ref, dstf_ref, dstb_ref, dec_ref, stcat_ref, lvl_ref = (next(it) for _ in range(7))
    q_s, kf_s, kb_s, bf_s, pb_s = (next(it) for _ in range(5))
    kfs_s, kbs_s = next(it), next(it)

    C = HG_CHUNK
    nc = L // C
    ncs = seq // C
    lane = lax.broadcasted_iota(jnp.int32, (1, LANES), 1)
    h0 = lane < HEAD_DIM
    bd = _bd_ones()
    bd_mask = bd > 0
    rowi = lax.broadcasted_iota(jnp.int32, (C, C), 0)
    coli = lax.broadcasted_iota(jnp.int32, (C, C), 1)
    tri = jnp.where(rowi >= coli, 1.0, 0.0).astype(BF16)
    sub = lax.broadcasted_iota(jnp.int32, (C // SUBLANES, SUBLANES, LANES), 1)
    row2 = lax.broadcasted_iota(jnp.int32, (2 * C, C), 0) % C
    col2 = lax.broadcasted_iota(jnp.int32, (2 * C, C), 1)
    lvl_ref[...] = 31 - lax.clz(row2 ^ col2)

    llb_f, llb_b = gp_ref[0:1, :], gp_ref[1:2, :]
    l1m_f, l1m_b = gp_ref[2:3, :], gp_ref[3:4, :]
    oml_f, oml_b = gp_ref[4:5, :], gp_ref[5:6, :]

    def to3(x):
        return x.reshape(C // SUBLANES, SUBLANES, LANES)

    def prep_gates(c):
        rows = pl.ds(c * C, C)
        q = _silu(hq_ref[rows, :])
        lff, kf = _hgrn_gate(ff_ref[rows, :], llb_f, l1m_f, oml_f, layer == 0)
        lfb, kb = _hgrn_gate(fb_ref[rows, :], llb_b, l1m_b, oml_b, layer == 0)
        lf2 = jnp.concatenate([lff, lfb], axis=1) * LOG2E
        return q, kf, kb, lf2, _split2(lf2)

    def prep_cumsum(vals):
        hi, lo = vals[4]
        return _dot(tri, hi) + _dot(tri, lo)

    def prep_decay(c, gmax, vals, cs):
        rows = pl.ds(c * C, C)
        q, kf, kb, lf2, _ = vals
        bf = cs[:, :LANES]
        bb_incl = cs[:, LANES:]
        pb = bb_incl - lf2[:, LANES:]
        bf_last = bf[C - 1:C, :]
        tot_b = bb_incl[C - 1:C, :]
        q_s[rows, :] = q
        kf_s[rows, :] = kf
        kb_s[rows, :] = kb
        bf_s[rows, :] = bf
        pb_s[rows, :] = pb
        dec_ref[c, 0:1, :] = jnp.exp2(bf_last)
        dec_ref[c, 1:2, :] = jnp.exp2(tot_b)
        xi_ref[rows, 0:LANES] = (q * jnp.exp2(bf)).astype(BF16)
        xi_ref[rows, LANES:] = (q * jnp.exp2(tot_b - pb)).astype(BF16)
        for blk in range(C // HG_BLOCK):
            s, e = blk * HG_BLOCK, (blk + 1) * HG_BLOCK - 1
            gmax = jnp.maximum(gmax, bf[s:s + 1, :] - bf[e:e + 1, :])
            gmax = jnp.maximum(gmax, pb[s:s + 1, :] - pb[e:e + 1, :])
        kfs = (kf * jnp.exp2(bf_last - bf)).astype(BF16)
        kbs = (kb * jnp.exp2(pb)).astype(BF16)
        kfs_s[rows, :] = kfs
        kbs_s[rows, :] = kbs
        vt = hi_ref[rows, :].T.astype(BF16)
        return gmax, (vt, kfs, kbs)

    def prep_states(c, ops):
        vt, kfs, kbs = ops
        dstf_ref[c] = jnp.where(bd_mask, _dot(vt, kfs), 0.0)
        dstb_ref[c] = jnp.where(bd_mask, _dot(vt, kbs), 0.0)

    jm = (lax.broadcasted_iota(jnp.int32, (2 * HG_BLOCK, C), 1)
          - lax.broadcasted_iota(jnp.int32, (2 * HG_BLOCK, C), 0) % HG_BLOCK)

    def fast_pack(c):
        rows = pl.ds(c * C, C)
        q, kf, kb, bf, pb = q_s[rows, :], kf_s[rows, :], kb_s[rows, :], bf_s[rows, :], pb_s[rows, :]
        packed = []
        for blk in range(C // HG_BLOCK):
            s, e = blk * HG_BLOCK, (blk + 1) * HG_BLOCK
            rf = bf[e - 1:e, :]
            rb_ = pb[s:s + 1, :]
            qf = q[s:e, :] * jnp.exp2(bf[s:e, :] - rf)
            qb = q[s:e, :] * jnp.exp2(rb_ - pb[s:e, :])
            x = jnp.concatenate([jnp.where(h0, qf, 0.0), jnp.where(h0, 0.0, qf),
                                 jnp.where(h0, qb, 0.0), jnp.where(h0, 0.0, qb)], axis=0).astype(BF16)
            if e == C:
                kfv = kfs_s[rows, :]
            else:
                kfv = (kf[:e, :] * jnp.exp2(rf - bf[:e, :])).astype(BF16)
            if s == 0:
                kbv = kbs_s[rows, :]
            else:
                kbv = (kb[s:, :] * jnp.exp2(pb[s:, :] - rb_)).astype(BF16)
            parts = [kfv]
            if e < C:
                parts.append(jnp.zeros((C - e, LANES), BF16))
            if s > 0:
                parts.append(jnp.zeros((s, LANES), BF16))
            parts.append(kbv)
            packed.append((x, jnp.concatenate(parts, axis=0)))
        return packed

    def fast_scores(packed):
        return [_dot_nt(x, w) for x, w in packed]

    def fast_mask(scores):
        a_rows = []
        for blk, sc in enumerate(scores):
            s = blk * HG_BLOCK
            af = sc[:2 * HG_BLOCK, :C]
            ab = sc[2 * HG_BLOCK:, C:]
            a_rows.append(jnp.where(jm < s, af, jnp.where(jm > s, ab, af + ab)))
        return jnp.concatenate(a_rows, axis=0).astype(BF16)

    def fast_store(c, o2):
        o_blocks = []
        for blk in range(C // HG_BLOCK):
            base = blk * 2 * HG_BLOCK
            o_blocks.append(jnp.where(h0, o2[base:base + HG_BLOCK], o2[base + HG_BLOCK:base + 2 * HG_BLOCK]))
        oacc[pl.ds(c * C, C), :] = jnp.concatenate(o_blocks, axis=0)

    def exact_body(c, carry):
        rows = pl.ds(pl.multiple_of(c * C, C), C)
        q, kf, kb, bf, pb = q_s[rows, :], kf_s[rows, :], kb_s[rows, :], bf_s[rows, :], pb_s[rows, :]
        v = hi_ref[rows, :]
        vb = v.astype(BF16)

        d0 = (q * (kf + kb)).astype(BF16)
        o = _dot(d0, bd) * v
        q3, v3, bf3, pb3, kf3, kb3 = to3(q), to3(v), to3(bf), to3(pb), to3(kf), to3(kb)
        nbf3, npb3 = -bf3, -pb3
        for dl in range(1, SUBLANES):
            src_fw = sub < (SUBLANES - dl)
            u = jnp.where(src_fw, nbf3, pb3)
            kmix = jnp.where(src_fw, kf3, kb3)
            w = jnp.where(sub >= dl, bf3, npb3)
            rel = w + pltpu.roll(u, dl, axis=1)
            d = (q3 * pltpu.roll(kmix, dl, axis=1) * jnp.exp2(rel)).reshape(C, LANES).astype(BF16)
            o = o + _dot(d, bd) * pltpu.roll(v3, dl, axis=1).reshape(C, LANES)

        lvl = lvl_ref[...]
        a_tot = jnp.zeros((2 * C, C), F32)
        m = SUBLANES
        while m < C:
            nblk = C // (2 * m)

            def halves(x):
                x4 = x.reshape(nblk, 2, m, LANES)
                return x4[:, 0], x4[:, 1]

            def join(lo, up):
                return jnp.stack([lo, up], axis=1).reshape(C, LANES)

            q_lo, q_up = halves(q)
            kf_lo, _ = halves(kf)
            _, kb_up = halves(kb)
            bf_lo, bf_up = halves(bf)
            pb_lo, pb_up = halves(pb)
            rf = bf_lo[:, m - 1:m, :]
            rb_ = pb_up[:, 0:1, :]
            xq = join(q_lo * jnp.exp2(rb_ - pb_lo), q_up * jnp.exp2(bf_up - rf))
            yk = join(kf_lo * jnp.exp2(rf - bf_lo), kb_up * jnp.exp2(pb_up - rb_))
            xs = jnp.concatenate([jnp.where(h0, xq, 0.0), jnp.where(h0, 0.0, xq)], axis=0).astype(BF16)
            a = _dot_nt(xs, yk.astype(BF16))
            a_tot = jnp.where(lvl == (m.bit_length() - 1), a, a_tot)
            m *= 2
        o2 = _dot(a_tot.astype(BF16), vb)
        o = o + jnp.where(h0, o2[:C], o2[C:])
        oacc[rows, :] = o
        return carry

    gmax = jnp.zeros((1, LANES), F32)
    for g0 in range(0, nc, HG_GROUP):
        grp = range(g0, g0 + HG_GROUP)
        vals = [prep_gates(c) for c in grp]
        sums = [prep_cumsum(vv) for vv in vals]
        ops = []
        for c, vv, cs in zip(grp, vals, sums):
            gmax, op = prep_decay(c, gmax, vv, cs)
            ops.append(op)
        for c, op in zip(grp, ops):
            prep_states(c, op)
    for g0 in range(0, nc, HG_GROUP):
        grp = range(g0, g0 + HG_GROUP)
        packed = [fast_pack(c) for c in grp]
        scores = [fast_scores(pk) for pk in packed]
        amats = [fast_mask(sc) for sc in scores]
        outs = [_dot(a, hi_ref[pl.ds(c * C, C), :].astype(BF16)) for a, c in zip(amats, grp)]
        for c, o2 in zip(grp, outs):
            fast_store(c, o2)

    for sq in range(nc // ncs):
        chunks = range(sq * ncs, (sq + 1) * ncs)
        if has_state:
            stf, stb = _pair_state_in(s0_ref, 0), _pair_state_in(s0_ref, 1)
        else:
            stf = jnp.zeros((LANES, LANES), F32)
            stb = stf
        for c in chunks:
            stcat_ref[c, :, 0:LANES] = stf.astype(BF16)
            stf = stf * dec_ref[c, 0:1, :] + dstf_ref[c]
        for c in reversed(chunks):
            stcat_ref[c, :, LANES:] = stb.astype(BF16)
            stb = stb * dec_ref[c, 1:2, :] + dstb_ref[c]
        if want_state:
            _pair_state_out(stf, st_view(sq), 0)
            _pair_state_out(stb, st_view(sq), 1)

    def finish():
        for g0 in range(0, nc, HG_GROUP):
            grp = range(g0, g0 + HG_GROUP)
            rows = [slice(c * C, (c + 1) * C) for c in grp]
            inter = [_dot_nt(xi_ref[r, :], stcat_ref[c]) for c, r in zip(grp, rows)]
            outs = [oacc[r, :] + x for r, x in zip(rows, inter)]
            sums = [_head_sums_sq(o, bd) for o in outs]
            for r, o, ms in zip(rows, outs, sums):
                y = o * lax.rsqrt(ms * (1.0 / HEAD_DIM) + LN_EPS)
                o_ref[r, :] = (y * _silu(hg_ref[r, :])).astype(BF16)

    finish()

    @pl.when(jnp.logical_not(jnp.max(gmax) < HG_SAFE_LOG2))
    def _():
        lax.fori_loop(0, nc, exact_body, 0)
        finish()


def _hgrn(proj, gate_params, seq, layer, s0, want_state, carried=None):
    rows = proj.shape[0]
    L = MIX_ROWS
    nc = L // HG_CHUNK
    in_specs, state_spec, out_specs, out_shape = _mixer_specs(
        proj, (COL_HQ, COL_HFF, COL_HFB, COL_HI, COL_HG), seq, layer, s0, want_state, carried)
    in_specs = [pl.BlockSpec((None, None, SUBLANES, LANES), lambda p, b: (layer, p, 0, 0))] + in_specs
    args = [gate_params, proj, proj, proj, proj, proj]
    if s0 is not None:
        in_specs.append(state_spec)
        args.append(s0)
    aliases = _carry_states(in_specs, args, carried)
    kern = functools.partial(_hgrn_kernel, L=L, seq=seq, layer=layer, has_state=s0 is not None,
                             want_state=want_state, carried=carried is not None)
    return pl.pallas_call(
        kern,
        grid=(N_PAIRS, rows // L),
        in_specs=in_specs,
        out_specs=out_specs,
        out_shape=out_shape,
        input_output_aliases=aliases,
        scratch_shapes=[pltpu.VMEM((L, LANES), F32),
                        pltpu.VMEM((L, 2 * LANES), BF16),
                        pltpu.VMEM((nc, LANES, LANES), F32),
                        pltpu.VMEM((nc, LANES, LANES), F32),
                        pltpu.VMEM((nc, SUBLANES, LANES), F32),
                        pltpu.VMEM((nc, LANES, 2 * LANES), BF16),
                        pltpu.VMEM((2 * HG_CHUNK, HG_CHUNK), jnp.int32)]
                       + [pltpu.VMEM((L, LANES), F32)] * 5
                       + [pltpu.VMEM((L, LANES), BF16)] * 2,
        compiler_params=pltpu.CompilerParams(
            dimension_semantics=("arbitrary", "arbitrary"), vmem_limit_bytes=VMEM_LIMIT),
        name="hgrn2",
    )(*args)


def _ffn_kernel(x_ref, of_ref, or_ref, oh_ref, mod_ref, wo_ref, lng_ref, lnb_ref,
                wg_ref, wu_ref, wd_ref, *rest, n_side, mod_row):
    side_in, y_ref, side_out = rest[:n_side], rest[n_side], rest[n_side + 1:]
    g1, sh2, sc2, g2 = _mod_vectors(mod_ref, mod_row)[2:6]
    half = ROW_TILE // 2
    halves = (slice(0, half), slice(half, ROW_TILE))
    mix = [_dot(jnp.concatenate([of_ref[r, :], or_ref[r, :], oh_ref[r, :]], axis=1), wo_ref[...])
           for r in halves]
    x1, gate, up, ffn = [], [], [], []
    for r, m in zip(halves, mix):
        x1.append(_ln(ALPHA * x_ref[r, :] + g1 * m) * lng_ref[0:1, :] + lnb_ref[0:1, :])
        h2 = (_ln(x1[-1]) * (1.0 + sc2) + sh2).astype(BF16)
        gate.append(_dot(h2, wg_ref[...]))
        up.append(_dot(h2, wu_ref[...]))
    for g, u in zip(gate, up):
        act = (_silu(g) * u).astype(BF16)
        ffn.append(_dot(act, wd_ref[...]))
    for r, x, f in zip(halves, x1, ffn):
        y_ref[r, :] = _ln(ALPHA * x + g2 * f) * lng_ref[1:2, :] + lnb_ref[1:2, :]
    _side_casts(side_in + side_out)


def _out_ffn(x2d, o_f, o_r, o_h, mod, w_out_bf, ln_g, ln_b, wg_bf, wu_bf, wd_bf, layer, mod_row, side=()):
    rows = x2d.shape[0]
    nsteps = rows // ROW_TILE
    s_in, s_out, s_shape, s_args = _side_cast_specs(side, nsteps)
    once = pl.Buffered(1)
    row_blk = lambda w: pl.BlockSpec((ROW_TILE, w), lambda i: (i, 0))
    return pl.pallas_call(
        functools.partial(_ffn_kernel, n_side=len(side), mod_row=mod_row),
        grid=(nsteps,),
        in_specs=[row_blk(D_MODEL), row_blk(FNET_WIDTH), row_blk(HEADS_WIDTH), row_blk(HEADS_WIDTH),
                  pl.BlockSpec((None, MOD_ROWS, N_MOD * D_MODEL), lambda i: (layer, 0, 0)),
                  pl.BlockSpec((D_MODEL, D_MODEL), lambda i: (0, 0), pipeline_mode=once),
                  pl.BlockSpec((None, 2, D_MODEL), lambda i: (layer, 0, 0)),
                  pl.BlockSpec((None, 2, D_MODEL), lambda i: (layer, 0, 0)),
                  pl.BlockSpec((D_MODEL, D_FF), lambda i: (0, 0), pipeline_mode=once),
                  pl.BlockSpec((D_MODEL, D_FF), lambda i: (0, 0), pipeline_mode=once),
                  pl.BlockSpec((D_FF, D_MODEL), lambda i: (0, 0), pipeline_mode=once)] + s_in,
        out_specs=[row_blk(D_MODEL)] + s_out,
        out_shape=[jax.ShapeDtypeStruct((rows, D_MODEL), F32)] + s_shape,
        compiler_params=pltpu.CompilerParams(
            dimension_semantics=("arbitrary",), vmem_limit_bytes=VMEM_LIMIT),
        name="out_ffn",
    )(x2d, o_f, o_r, o_h, mod, w_out_bf, ln_g, ln_b, wg_bf, wu_bf, wd_bf, *s_args)


def kernel(x_prompt, x_sample, c, state_ret, state_hgrn, c_ctx, w_mod, b_mod, w_in, w_out,
           ret_log_decay, hg_lower_bound, ln_g, ln_b, w_gate, w_up, w_down):
    B, S, _ = x_prompt.shape
    DB, DS, _ = x_sample.shape

    p = jax.nn.softmax(hg_lower_bound.astype(F32), axis=1)
    cum = jnp.cumsum(p, axis=1)
    lbs = cum - cum[:, :1]
    kinds = jnp.stack([jnp.log(lbs), jnp.log1p(-lbs), 1.0 - lbs], axis=0)
    gp = jnp.transpose(kinds.reshape(3, 2, DEPTH, N_PAIRS, LANES), (2, 3, 0, 1, 4))
    gp = gp.reshape(DEPTH, N_PAIRS, 6, LANES)
    gp = jnp.concatenate([gp, jnp.zeros((DEPTH, N_PAIRS, SUBLANES - 6, LANES), F32)], axis=2)
    lg_flat = (-jnp.exp(ret_log_decay.astype(F32))).reshape(-1)

    cv =jnp.concatenate([c_ctx[None, :], c, jnp.zeros((MOD_ROWS - 1 - DB, D_MODEL), F32)], axis=0)
    mod = _modulation(cv, w_mod, b_mod)

    rope = _rope_tables(DS)

    ctx_row = lambda i: 0
    smp_row = lambda i: 1 + i // (DS // ROW_TILE)

    y = x_prompt.reshape(B * S, D_MODEL)
    z = x_sample.reshape(DB * DS, D_MODEL)
    new_state_ret, new_state_hgrn = None, None
    w_in_bf = w_in[0].astype(BF16)
    w_out_bf = wg_bf = wu_bf = wd_bf = None
    for l in range(DEPTH):
        side_c = [(w_out, l), (w_gate, l)] if l == 0 else []
        side_s = [(w_up, l), (w_down, l)] if l == 0 else []
        proj_c, *cast_c = _inproj(y, mod, w_in_bf, l, ctx_row, side_c)
        proj_s, *cast_s = _inproj(z, mod, w_in_bf, l, smp_row, side_s)
        if l == 0:
            (w_out_bf, wg_bf), (wu_bf, wd_bf) = cast_c, cast_s
        of_c = _fnet(proj_c, S)
        or_c, new_state_ret = _retention(proj_c, lg_flat, S, l, None, None, True, new_state_ret)
        oh_c, new_state_hgrn = _hgrn(proj_c, gp, S, l, None, True, new_state_hgrn)
        of_s = _fnet(proj_s, DS)
        (or_s,) = _retention(proj_s, lg_flat, DS, l, rope, state_ret, False)
        (oh_s,) = _hgrn(proj_s, gp, DS, l, state_hgrn, False)
        last = l + 1 == DEPTH
        side_c = [] if last else [(w_in, l + 1), (w_out, l + 1)]
        side_s = [] if last else [(w_gate, l + 1), (w_up, l + 1), (w_down, l + 1)]
        weights = (w_out_bf, ln_g, ln_b, wg_bf, wu_bf, wd_bf)
        y, *cast_c = _out_ffn(y, of_c, or_c, oh_c, mod, *weights, l, ctx_row, side_c)
        z, *cast_s = _out_ffn(z, of_s, or_s, oh_s, mod, *weights, l, smp_row, side_s)
        if not last:
            (w_in_bf, w_out_bf), (wg_bf, wu_bf, wd_bf) = cast_c, cast_s

    return (y.reshape(B, S, D_MODEL), z.reshape(DB, DS, D_MODEL), new_state_ret, new_state_hgrn)
```

```python
import functools

import numpy as np
import jax
import jax.numpy as jnp
from jax import lax
from jax.experimental import pallas as pl
from jax.experimental.pallas import tpu as pltpu

F32 = jnp.float32
BF16 = jnp.bfloat16

D_MODEL = 1024
DEPTH = 2
GRID_W = 64
FNET_WIDTH = 256
FNET_GDIM = 64
N_HEADS = 6
N_PAIRS = N_HEADS // 2
HEAD_DIM = 64
LANES = 128
SUBLANES = 8
HEADS_WIDTH = N_HEADS * HEAD_DIM
IN_WIDTH = 3712
D_FF = 2816
ROPE_BASE = 10000.0
LN_EPS = 1e-5
LOG2E = 1.4426950408889634
N_MOD = 6
ALPHA = (2 * DEPTH) ** 0.25
MOD_ROWS = 8
ROW_TILE = 512
RET_ROWS = 256
MIX_ROWS = 2048
HG_CHUNK = 128
HG_GROUP = 16
HG_BLOCK = 32
HG_SAFE_LOG2 = 96.0
VMEM_LIMIT = 56 * 1024 * 1024

COL_RQ, COL_RK, COL_RV, COL_RG = 2, 5, 8, 11
COL_HQ, COL_HFF, COL_HFB, COL_HI, COL_HG = 14, 17, 20, 23, 26


def _silu(x):
    return x / (1.0 + jnp.exp(-x))


def _ln(x):
    mu = jnp.mean(x, axis=-1, keepdims=True)
    xc = x - mu
    var = jnp.mean(xc * xc, axis=-1, keepdims=True)
    return xc * lax.rsqrt(var + LN_EPS)


def _dot(a, b):
    return jnp.dot(a, b, preferred_element_type=F32)


def _dot_nt(a, b):
    return lax.dot_general(a, b, (((1,), (1,)), ((), ())), preferred_element_type=F32)


def _split2(x):
    hi = x.astype(BF16)
    lo = (x - hi.astype(F32)).astype(BF16)
    return hi, lo


def _split3(x):
    x1 = x.astype(BF16)
    r1 = x - x1.astype(F32)
    x2 = r1.astype(BF16)
    x3 = (r1 - x2.astype(F32)).astype(BF16)
    return x1, x2, x3


def _head_sums(x, bd_ones):
    return _dot(x.astype(BF16), bd_ones)


def _head_sums_sq(x, bd_ones):
    return _head_sums(x * x, bd_ones)


def _bd_ones():
    r = lax.broadcasted_iota(jnp.int32, (LANES, LANES), 0)
    c = lax.broadcasted_iota(jnp.int32, (LANES, LANES), 1)
    return jnp.where((r < HEAD_DIM) == (c < HEAD_DIM), 1.0, 0.0).astype(BF16)


def _dot3_nt(sel, x):
    return sum(_dot_nt(sel, part) for part in _split3(x))


def _pair_state_in(s_ref, direction):
    z = jnp.zeros((HEAD_DIM, HEAD_DIM), F32)
    r = lax.broadcasted_iota(jnp.int32, (LANES, HEAD_DIM), 0)
    c = lax.broadcasted_iota(jnp.int32, (LANES, HEAD_DIM), 1)
    out = None
    for hh in range(2):
        s = s_ref[direction, hh]
        rows = jnp.concatenate([s, z] if hh == 0 else [z, s], axis=0)
        place = jnp.where(r == c + hh * HEAD_DIM, 1.0, 0.0).astype(BF16)
        term = _dot3_nt(place, rows)
        out = term if out is None else out + term
    return out


def _state_out_view(st_ref, layer, carried):
    if carried:
        return lambda sq: st_ref.at[sq]
    for other in range(DEPTH):
        if other != layer:
            st_ref[:, other] = jnp.zeros(st_ref.shape[:1] + st_ref.shape[2:], F32)
    return lambda sq: st_ref.at[sq, layer]


def _pair_state_out(st, st_ref, direction):
    r = lax.broadcasted_iota(jnp.int32, (HEAD_DIM, LANES), 0)
    c = lax.broadcasted_iota(jnp.int32, (HEAD_DIM, LANES), 1)
    for hh in range(2):
        pick = jnp.where(c == r + hh * HEAD_DIM, 1.0, 0.0).astype(BF16)
        st_ref[direction, hh] = _dot3_nt(pick, st[hh * HEAD_DIM:(hh + 1) * HEAD_DIM, :])


def _mod_kernel(cv_ref, w_ref, b_ref, o_ref):
    s = _silu(cv_ref[...]).astype(BF16)
    bias = b_ref[pl.ds(pl.program_id(0), 1), :]
    o_ref[...] = _dot(s, w_ref[...].astype(BF16)) + bias


def _modulation(cv, w_mod, b_mod):
    tn = 3072
    width = N_MOD * D_MODEL
    return pl.pallas_call(
        _mod_kernel,
        grid=(DEPTH, width // tn),
        in_specs=[pl.BlockSpec((MOD_ROWS, D_MODEL), lambda l, j: (0, 0)),
                  pl.BlockSpec((None, D_MODEL, tn), lambda l, j: (l, 0, j)),
                  pl.BlockSpec((DEPTH, tn), lambda l, j: (0, j))],
        out_specs=pl.BlockSpec((None, MOD_ROWS, tn), lambda l, j: (l, 0, j)),
        out_shape=jax.ShapeDtypeStruct((DEPTH, MOD_ROWS, width), F32),
        compiler_params=pltpu.CompilerParams(
            dimension_semantics=("arbitrary", "arbitrary"), vmem_limit_bytes=VMEM_LIMIT),
        name="modulation",
    )(cv, w_mod, b_mod)


def _side_cast_specs(side, nsteps):
    in_specs, out_specs, out_shape, args = [], [], [], []
    for w, layer in side:
        _, r, c = w.shape
        in_specs.append(pl.BlockSpec((None, r // nsteps, c), lambda i, layer=layer: (layer, i, 0)))
        out_specs.append(pl.BlockSpec((r // nsteps, c), lambda i: (i, 0)))
        out_shape.append(jax.ShapeDtypeStruct((r, c), BF16))
        args.append(w)
    return in_specs, out_specs, out_shape, args


def _side_casts(side_refs):
    n = len(side_refs) // 2
    for src, dst in zip(side_refs[:n], side_refs[n:]):
        dst[...] = src[...].astype(BF16)


def _mod_vectors(mod_ref, mod_row):
    m = mod_ref[pl.ds(mod_row(pl.program_id(0)), 1), :]
    return [m[:, k * D_MODEL:(k + 1) * D_MODEL] for k in range(N_MOD)]


def _inproj_kernel(x_ref, mod_ref, w_ref, *rest, n_side, mod_row):
    side_in, o_ref, side_out = rest[:n_side], rest[n_side], rest[n_side + 1:]
    sh, sc = _mod_vectors(mod_ref, mod_row)[0:2]
    half = ROW_TILE // 2
    for r in (slice(0, half), slice(half, ROW_TILE)):
        h = _ln(x_ref[r, :]) * (1.0 + sc) + sh
        o_ref[r, :] = _dot(h.astype(BF16), w_ref[...])
    _side_casts(side_in + side_out)


def _inproj(x2d, mod, w_in_bf, layer, mod_row, side=()):
    rows = x2d.shape[0]
    nsteps = rows // ROW_TILE
    s_in, s_out, s_shape, s_args = _side_cast_specs(side, nsteps)
    return pl.pallas_call(
        functools.partial(_inproj_kernel, n_side=len(side), mod_row=mod_row),
        grid=(nsteps,),
        in_specs=[pl.BlockSpec((ROW_TILE, D_MODEL), lambda i: (i, 0)),
                  pl.BlockSpec((None, MOD_ROWS, N_MOD * D_MODEL), lambda i: (layer, 0, 0)),
                  pl.BlockSpec((D_MODEL, IN_WIDTH), lambda i: (0, 0))] + s_in,
        out_specs=[pl.BlockSpec((ROW_TILE, IN_WIDTH), lambda i: (i, 0))] + s_out,
        out_shape=[jax.ShapeDtypeStruct((rows, IN_WIDTH), F32)] + s_shape,
        compiler_params=pltpu.CompilerParams(
            dimension_semantics=("arbitrary",), vmem_limit_bytes=VMEM_LIMIT),
        name="inproj",
    )(x2d, mod, w_in_bf, *s_args)


def _fnet_kernel(u_ref, w1_ref, cl_ref, sl_ref, o_ref, *, seq):
    u = u_ref[...].astype(BF16)
    t = _dot(u, w1_ref[...])
    uc = t[:, :FNET_WIDTH].astype(BF16)
    us = t[:, FNET_WIDTH:].astype(BF16)
    for s in range(u.shape[0] // seq):
        rows = slice(s * seq, (s + 1) * seq)
        o = _dot(cl_ref[...], uc[rows, :]) - _dot(sl_ref[...], us[rows, :])
        o_ref[rows, :] = o.astype(BF16)


def _fnet_consts(L):
    scale = 1.0 / np.sqrt(L * FNET_GDIM)
    k = np.arange(FNET_GDIM)
    ang = 2.0 * np.pi * ((k[:, None] * k[None, :]) % FNET_GDIM) / FNET_GDIM
    eye = np.eye(FNET_WIDTH // FNET_GDIM)
    w1 = np.concatenate([np.kron(eye, np.cos(ang)), np.kron(eye, np.sin(ang))], axis=1) * scale
    n = np.arange(L)
    angl = 2.0 * np.pi * ((n[:, None] * n[None, :]) % L) / L
    tables = (w1, np.cos(angl), np.sin(angl))
    return tuple(jnp.asarray(t, F32).astype(BF16) for t in tables)


def _fnet(proj, L):
    rows = proj.shape[0]
    w1, cl, sl = _fnet_consts(L)
    return pl.pallas_call(
        functools.partial(_fnet_kernel, seq=L),
        grid=(rows // MIX_ROWS,),
        in_specs=[pl.BlockSpec((MIX_ROWS, FNET_WIDTH), lambda b: (b, 0)),
                  pl.BlockSpec((FNET_WIDTH, 2 * FNET_WIDTH), lambda b: (0, 0)),
                  pl.BlockSpec((L, L), lambda b: (0, 0)),
                  pl.BlockSpec((L, L), lambda b: (0, 0))],
        out_specs=pl.BlockSpec((MIX_ROWS, FNET_WIDTH), lambda b: (b, 0)),
        out_shape=jax.ShapeDtypeStruct((rows, FNET_WIDTH), BF16),
        compiler_params=pltpu.CompilerParams(
            dimension_semantics=("arbitrary",), vmem_limit_bytes=VMEM_LIMIT),
        name="fnet",
    )(proj, w1, cl, sl)


def _rope_tables(L):
    half = HEAD_DIM // 2
    inv = ROPE_BASE ** (-np.arange(0, half, 2, dtype=np.float64) / half)
    n = np.arange(L)
    rows, cols = n // GRID_W, n % GRID_W
    lane = np.arange(LANES)
    f = lane % (half // 2)
    use_col = (lane % HEAD_DIM) >= half
    pos = np.where(use_col[None, :], cols[:, None], rows[:, None]).astype(np.float64)
    ang = pos * inv[f][None, :]
    lo = (lane % half) < (half // 2)
    cos = np.cos(ang)
    sin = np.where(lo[None, :], -np.sin(ang), np.sin(ang))
    reps = (MIX_ROWS // L, 1)
    return jnp.asarray(np.tile(cos, reps), F32), jnp.asarray(np.tile(sin, reps), F32)


def _rope(x, cos, sin_signed, lo_mask):
    quarter = HEAD_DIM // 4
    swapped = jnp.where(lo_mask, pltpu.roll(x, LANES - quarter, axis=1), pltpu.roll(x, quarter, axis=1))
    return x * cos + swapped * sin_signed


def _ret_kernel(*refs, L, seq, layer, use_rope, has_state, want_state, carried):
    it = iter(refs)
    lg_ref = next(it)
    q_ref, k_ref, v_ref, g_ref = next(it), next(it), next(it), next(it)
    if use_rope:
        cos_ref, sin_ref = next(it), next(it)
    if has_state:
        s0_ref = next(it)
    if carried:
        next(it)
    o_ref = next(it)
    if want_state:
        st_view = _state_out_view(next(it), layer, carried)
    oacc = next(it)

    p = pl.program_id(0)
    T = RET_ROWS
    nc = L // T
    ncs = seq // T
    lane = lax.broadcasted_iota(jnp.int32, (1, LANES), 1)
    h0 = lane < HEAD_DIM
    bd = _bd_ones()
    bd_mask = bd > 0

    def lg(direction, head):
        return lg_ref[layer * 2 * N_HEADS + direction * N_HEADS + 2 * p + head]

    i = lax.broadcasted_iota(jnp.int32, (T, T), 0)
    j = lax.broadcasted_iota(jnp.int32, (T, T), 1)
    rel = (i - j).astype(F32)
    masks = []
    for hh in range(2):
        arg = jnp.where(rel > 0, rel * lg(0, hh), -rel * lg(1, hh))
        masks.append(jnp.exp(arg) + jnp.where(rel == 0, 1.0, 0.0))
    kscale = HEAD_DIM ** -0.5
    mcat = (jnp.concatenate(masks, axis=0) * kscale).astype(BF16)

    lgf_vec = jnp.where(h0, lg(0, 0), lg(0, 1))
    lgb_vec = jnp.where(h0, lg(1, 0), lg(1, 1))
    pos = lax.broadcasted_iota(jnp.int32, (T, LANES), 0).astype(F32)
    wf = jnp.exp((pos + 1.0) * lgf_vec)
    wb = jnp.exp((float(T) - pos) * lgb_vec)
    zf = jnp.exp((float(T) - 1.0 - pos) * lgf_vec) * kscale
    zb = jnp.exp(pos * lgb_vec) * kscale
    gf_chunk = jnp.exp(float(T) * lgf_vec)
    gb_chunk = jnp.exp(float(T) * lgb_vec)
    use_states = has_state or ncs > 1
    if use_rope:
        lo_mask = (lane % (HEAD_DIM // 2)) < (HEAD_DIM // 4)

    chunk_rows = [slice(c * T, (c + 1) * T) for c in range(nc)]
    qk = []
    for rows in chunk_rows:
        qc = q_ref[rows, :]
        kc = k_ref[rows, :]
        if use_rope:
            qc = _rope(qc, cos_ref[rows, :], sin_ref[rows, :], lo_mask)
            kc = _rope(kc, cos_ref[rows, :], sin_ref[rows, :], lo_mask)
        qk.append((qc, kc))
    scores = []
    for qc, kc in qk:
        qs = jnp.concatenate([jnp.where(h0, qc, 0.0), jnp.where(h0, 0.0, qc)], axis=0).astype(BF16)
        scores.append(_dot_nt(qs, kc.astype(BF16)))
    dstf, dstb, xis = [], [], []
    if use_states or want_state:
        for rows, (qc, kc) in zip(chunk_rows, qk):
            vt = v_ref[rows, :].T.astype(BF16)
            dstf.append(jnp.where(bd_mask, _dot(vt, (kc * zf).astype(BF16)), 0.0))
            dstb.append(jnp.where(bd_mask, _dot(vt, (kc * zb).astype(BF16)), 0.0))
    if use_states:
        for qc, kc in qk:
            xis.append(jnp.concatenate([qc * wf, qc * wb], axis=1).astype(BF16))
    probs = [s.astype(BF16) * mcat for s in scores]
    outs = [_dot(pm, v_ref[rows, :].astype(BF16)) for pm, rows in zip(probs, chunk_rows)]
    for rows, o2 in zip(chunk_rows, outs):
        oacc[rows, :] = jnp.where(h0, o2[:T], o2[T:])

    stf_in, stb_in = [None] * nc, [None] * nc
    if use_states or want_state:
        for sq in range(nc // ncs):
            chunks = range(sq * ncs, (sq + 1) * ncs)
            if has_state:
                stf, stb = _pair_state_in(s0_ref.at[sq], 0), _pair_state_in(s0_ref.at[sq], 1)
            else:
                stf = jnp.zeros((LANES, LANES), F32)
                stb = stf
            for c in chunks:
                stf_in[c] = stf
                stf = stf * gf_chunk + dstf[c]
            for c in reversed(chunks):
                stb_in[c] = stb
                stb = stb * gb_chunk + dstb[c]
            if want_state:
                _pair_state_out(stf, st_view(sq), 0)
                _pair_state_out(stb, st_view(sq), 1)

    outs = [oacc[rows, :] for rows in chunk_rows]
    if use_states:
        inter = [_dot_nt(xis[c], jnp.concatenate([stf_in[c], stb_in[c]], axis=1).astype(BF16))
                 for c in range(nc)]
        outs = [o + x for o, x in zip(outs, inter)]
    means = [_head_sums(o, bd) * (1.0 / HEAD_DIM) for o in outs]
    cent = [o - mu for o, mu in zip(outs, means)]
    variances = [_head_sums_sq(oc, bd) * (1.0 / HEAD_DIM) for oc in cent]
    for rows, oc, var in zip(chunk_rows, cent, variances):
        y = oc * lax.rsqrt(var + LN_EPS)
        o_ref[rows, :] = (y * _silu(g_ref[rows, :])).astype(BF16)


def _mixer_specs(proj, cols, seq, layer, s0, want_state, carried):
    rows = proj.shape[0]
    blk = lambda col: pl.BlockSpec((MIX_ROWS, LANES), lambda p, b, col=col: (b, col + p))
    in_specs = [blk(col) for col in cols]
    state_spec = None
    if s0 is not None:
        state_spec = pl.BlockSpec((MIX_ROWS // seq, None, 2, 2, HEAD_DIM, HEAD_DIM),
                                  lambda p, b: (b, layer, 0, p, 0, 0))
    out_specs = [pl.BlockSpec((MIX_ROWS, LANES), lambda p, b: (b, p))]
    out_shape = [jax.ShapeDtypeStruct((rows, HEADS_WIDTH), BF16)]
    if want_state:
        if carried is None:
            out_specs.append(pl.BlockSpec((MIX_ROWS // seq, DEPTH, 2, 2, HEAD_DIM, HEAD_DIM),
                                          lambda p, b: (b, 0, 0, p, 0, 0)))
        else:
            out_specs.append(pl.BlockSpec((MIX_ROWS // seq, None, 2, 2, HEAD_DIM, HEAD_DIM),
                                          lambda p, b: (b, layer, 0, p, 0, 0)))
        out_shape.append(jax.ShapeDtypeStruct((rows // seq, DEPTH, 2, N_HEADS, HEAD_DIM, HEAD_DIM), F32))
    return in_specs, state_spec, out_specs, out_shape


def _carry_states(in_specs, args, carried):
    if carried is None:
        return {}
    in_specs.append(pl.BlockSpec(memory_space=pl.ANY))
    args.append(carried)
    return {len(args) - 1: 1}


def _retention(proj, lg_flat, seq, layer, rope, s0, want_state, carried=None):
    rows = proj.shape[0]
    in_specs, state_spec, out_specs, out_shape = _mixer_specs(
        proj, (COL_RQ, COL_RK, COL_RV, COL_RG), seq, layer, s0, want_state, carried)
    in_specs = [pl.BlockSpec(memory_space=pltpu.SMEM)] + in_specs
    args = [lg_flat, proj, proj, proj, proj]
    if rope is not None:
        in_specs += [pl.BlockSpec((MIX_ROWS, LANES), lambda p, b: (0, 0))] * 2
        args += list(rope)
    if s0 is not None:
        in_specs.append(state_spec)
        args.append(s0)
    aliases = _carry_states(in_specs, args, carried)
    kern = functools.partial(_ret_kernel, L=MIX_ROWS, seq=seq, layer=layer, use_rope=rope is not None,
                             has_state=s0 is not None, want_state=want_state, carried=carried is not None)
    return pl.pallas_call(
        kern,
        grid=(N_PAIRS, rows // MIX_ROWS),
        in_specs=in_specs,
        out_specs=out_specs,
        out_shape=out_shape,
        input_output_aliases=aliases,
        scratch_shapes=[pltpu.VMEM((MIX_ROWS, LANES), F32)],
        compiler_params=pltpu.CompilerParams(
            dimension_semantics=("arbitrary", "arbitrary"), vmem_limit_bytes=VMEM_LIMIT),
        name="retention",
    )(*args)


def _hgrn_gate(x, llb, l1mlb, omlb, zero_bound):
    z = jnp.exp(-jnp.abs(x))
    zp1 = 1.0 + z
    ls = jnp.minimum(x, 0.0) - jnp.log(zp1)
    sig_neg = jnp.where(x > 0.0, z, 1.0) / zp1
    if zero_bound:
        return ls, sig_neg
    t = l1mlb + ls
    logf = jnp.maximum(llb, t) + jnp.log(1.0 + jnp.exp(-jnp.abs(llb - t)))
    return logf, omlb * sig_neg


def _hgrn_kernel(*refs, L, seq, layer, has_state, want_state, carried):
    it = iter(refs)
    gp_ref = next(it)
    hq_ref, ff_ref, fb_ref, hi_ref, hg_ref = next(it), next(it), next(it), next(it), next(it)
    if has_state:
        s0_ref = next(it)
    if carried:
        next(it)
    o_ref = next(it)
    if want_state:
        st_view = _state_out_view(next(it), layer, carried)
    oacc, xi_ref, dstf_ref, dstb_ref, dec_ref, stcat_ref, lvl_ref = (next(it) for _ in range(7))
    q_s, kf_s, kb_s, bf_s, pb_s = (next(it) for _ in range(5))
    kfs_s, kbs_s = next(it), next(it)

    C = HG_CHUNK
    nc = L // C
    ncs = seq // C
    lane = lax.broadcasted_iota(jnp.int32, (1, LANES), 1)
    h0 = lane < HEAD_DIM
    bd = _bd_ones()
    bd_mask = bd > 0
    rowi = lax.broadcasted_iota(jnp.int32, (C, C), 0)
    coli = lax.broadcasted_iota(jnp.int32, (C, C), 1)
    tri = jnp.where(rowi >= coli, 1.0, 0.0).astype(BF16)
    sub = lax.broadcasted_iota(jnp.int32, (C // SUBLANES, SUBLANES, LANES), 1)
    row2 = lax.broadcasted_iota(jnp.int32, (2 * C, C), 0) % C
    col2 = lax.broadcasted_iota(jnp.int32, (2 * C, C), 1)
    lvl_ref[...] = 31 - lax.clz(row2 ^ col2)

    llb_f, llb_b = gp_ref[0:1, :], gp_ref[1:2, :]
    l1m_f, l1m_b = gp_ref[2:3, :], gp_ref[3:4, :]
    oml_f, oml_b = gp_ref[4:5, :], gp_ref[5:6, :]

    def to3(x):
        return x.reshape(C // SUBLANES, SUBLANES, LANES)

    def prep_gates(c):
        rows = pl.ds(c * C, C)
        q = _silu(hq_ref[rows, :])
        lff, kf = _hgrn_gate(ff_ref[rows, :], llb_f, l1m_f, oml_f, layer == 0)
        lfb, kb = _hgrn_gate(fb_ref[rows, :], llb_b, l1m_b, oml_b, layer == 0)
        lf2 = jnp.concatenate([lff, lfb], axis=1) * LOG2E
        return q, kf, kb, lf2, _split2(lf2)

    def prep_cumsum(vals):
        hi, lo = vals[4]
        return _dot(tri, hi) + _dot(tri, lo)

    def prep_decay(c, gmax, vals, cs):
        rows = pl.ds(c * C, C)
        q, kf, kb, lf2, _ = vals
        bf = cs[:, :LANES]
        bb_incl = cs[:, LANES:]
        pb = bb_incl - lf2[:, LANES:]
        bf_last = bf[C - 1:C, :]
        tot_b = bb_incl[C - 1:C, :]
        q_s[rows, :] = q
        kf_s[rows, :] = kf
        kb_s[rows, :] = kb
        bf_s[rows, :] = bf
        pb_s[rows, :] = pb
        dec_ref[c, 0:1, :] = jnp.exp2(bf_last)
        dec_ref[c, 1:2, :] = jnp.exp2(tot_b)
        xi_ref[rows, 0:LANES] = (q * jnp.exp2(bf)).astype(BF16)
        xi_ref[rows, LANES:] = (q * jnp.exp2(tot_b - pb)).astype(BF16)
        for blk in range(C // HG_BLOCK):
            s, e = blk * HG_BLOCK, (blk + 1) * HG_BLOCK - 1
            gmax = jnp.maximum(gmax, bf[s:s + 1, :] - bf[e:e + 1, :])
            gmax = jnp.maximum(gmax, pb[s:s + 1, :] - pb[e:e + 1, :])
        kfs = (kf * jnp.exp2(bf_last - bf)).astype(BF16)
        kbs = (kb * jnp.exp2(pb)).astype(BF16)
        kfs_s[rows, :] = kfs
        kbs_s[rows, :] = kbs
        vt = hi_ref[rows, :].T.astype(BF16)
        return gmax, (vt, kfs, kbs)

    def prep_states(c, ops):
        vt, kfs, kbs = ops
        dstf_ref[c] = jnp.where(bd_mask, _dot(vt, kfs), 0.0)
        dstb_ref[c] = jnp.where(bd_mask, _dot(vt, kbs), 0.0)

    jm = (lax.broadcasted_iota(jnp.int32, (2 * HG_BLOCK, C), 1)
          - lax.broadcasted_iota(jnp.int32, (2 * HG_BLOCK, C), 0) % HG_BLOCK)

    def fast_pack(c):
        rows = pl.ds(c * C, C)
        q, kf, kb, bf, pb = q_s[rows, :], kf_s[rows, :], kb_s[rows, :], bf_s[rows, :], pb_s[rows, :]
        packed = []
        for blk in range(C // HG_BLOCK):
            s, e = blk * HG_BLOCK, (blk + 1) * HG_BLOCK
            rf = bf[e - 1:e, :]
            rb_ = pb[s:s + 1, :]
            qf = q[s:e, :] * jnp.exp2(bf[s:e, :] - rf)
            qb = q[s:e, :] * jnp.exp2(rb_ - pb[s:e, :])
            x = jnp.concatenate([jnp.where(h0, qf, 0.0), jnp.where(h0, 0.0, qf),
                                 jnp.where(h0, qb, 0.0), jnp.where(h0, 0.0, qb)], axis=0).astype(BF16)
            if e == C:
                kfv = kfs_s[rows, :]
            else:
                kfv = (kf[:e, :] * jnp.exp2(rf - bf[:e, :])).astype(BF16)
            if s == 0:
                kbv = kbs_s[rows, :]
            else:
                kbv = (kb[s:, :] * jnp.exp2(pb[s:, :] - rb_)).astype(BF16)
            parts = [kfv]
            if e < C:
                parts.append(jnp.zeros((C - e, LANES), BF16))
            if s > 0:
                parts.append(jnp.zeros((s, LANES), BF16))
            parts.append(kbv)
            packed.append((x, jnp.concatenate(parts, axis=0)))
        return packed

    def fast_scores(packed):
        return [_dot_nt(x, w) for x, w in packed]

    def fast_mask(scores):
        a_rows = []
        for blk, sc in enumerate(scores):
            s = blk * HG_BLOCK
            af = sc[:2 * HG_BLOCK, :C]
            ab = sc[2 * HG_BLOCK:, C:]
            a_rows.append(jnp.where(jm < s, af, jnp.where(jm > s, ab, af + ab)))
        return jnp.concatenate(a_rows, axis=0).astype(BF16)

    def fast_store(c, o2):
        o_blocks = []
        for blk in range(C // HG_BLOCK):
            base = blk * 2 * HG_BLOCK
            o_blocks.append(jnp.where(h0, o2[base:base + HG_BLOCK], o2[base + HG_BLOCK:base + 2 * HG_BLOCK]))
        oacc[pl.ds(c * C, C), :] = jnp.concatenate(o_blocks, axis=0)

    def exact_body(c, carry):
        rows = pl.ds(pl.multiple_of(c * C, C), C)
        q, kf, kb, bf, pb = q_s[rows, :], kf_s[rows, :], kb_s[rows, :], bf_s[rows, :], pb_s[rows, :]
        v = hi_ref[rows, :]
        vb = v.astype(BF16)

        d0 = (q * (kf + kb)).astype(BF16)
        o = _dot(d0, bd) * v
        q3, v3, bf3, pb3, kf3, kb3 = to3(q), to3(v), to3(bf), to3(pb), to3(kf), to3(kb)
        nbf3, npb3 = -bf3, -pb3
        for dl in range(1, SUBLANES):
            src_fw = sub < (SUBLANES - dl)
            u = jnp.where(src_fw, nbf3, pb3)
            kmix = jnp.where(src_fw, kf3, kb3)
            w = jnp.where(sub >= dl, bf3, npb3)
            rel = w + pltpu.roll(u, dl, axis=1)
            d = (q3 * pltpu.roll(kmix, dl, axis=1) * jnp.exp2(rel)).reshape(C, LANES).astype(BF16)
            o = o + _dot(d, bd) * pltpu.roll(v3, dl, axis=1).reshape(C, LANES)

        lvl = lvl_ref[...]
        a_tot = jnp.zeros((2 * C, C), F32)
        m = SUBLANES
        while m < C:
            nblk = C // (2 * m)

            def halves(x):
                x4 = x.reshape(nblk, 2, m, LANES)
                return x4[:, 0], x4[:, 1]

            def join(lo, up):
                return jnp.stack([lo, up], axis=1).reshape(C, LANES)

            q_lo, q_up = halves(q)
            kf_lo, _ = halves(kf)
            _, kb_up = halves(kb)
            bf_lo, bf_up = halves(bf)
            pb_lo, pb_up = halves(pb)
            rf = bf_lo[:, m - 1:m, :]
            rb_ = pb_up[:, 0:1, :]
            xq = join(q_lo * jnp.exp2(rb_ - pb_lo), q_up * jnp.exp2(bf_up - rf))
            yk = join(kf_lo * jnp.exp2(rf - bf_lo), kb_up * jnp.exp2(pb_up - rb_))
            xs = jnp.concatenate([jnp.where(h0, xq, 0.0), jnp.where(h0, 0.0, xq)], axis=0).astype(BF16)
            a = _dot_nt(xs, yk.astype(BF16))
            a_tot = jnp.where(lvl == (m.bit_length() - 1), a, a_tot)
            m *= 2
        o2 = _dot(a_tot.astype(BF16), vb)
        o = o + jnp.where(h0, o2[:C], o2[C:])
        oacc[rows, :] = o
        return carry

    gmax = jnp.zeros((1, LANES), F32)
    for g0 in range(0, nc, HG_GROUP):
        grp = range(g0, g0 + HG_GROUP)
        vals = [prep_gates(c) for c in grp]
        sums = [prep_cumsum(vv) for vv in vals]
        ops = []
        for c, vv, cs in zip(grp, vals, sums):
            gmax, op = prep_decay(c, gmax, vv, cs)
            ops.append(op)
        for c, op in zip(grp, ops):
            prep_states(c, op)
    for g0 in range(0, nc, HG_GROUP):
        grp = range(g0, g0 + HG_GROUP)
        packed = [fast_pack(c) for c in grp]
        scores = [fast_scores(pk) for pk in packed]
        amats = [fast_mask(sc) for sc in scores]
        outs = [_dot(a, hi_ref[pl.ds(c * C, C), :].astype(BF16)) for a, c in zip(amats, grp)]
        for c, o2 in zip(grp, outs):
            fast_store(c, o2)

    for sq in range(nc // ncs):
        chunks = range(sq * ncs, (sq + 1) * ncs)
        if has_state:
            stf, stb = _pair_state_in(s0_ref.at[sq], 0), _pair_state_in(s0_ref.at[sq], 1)
        else:
            stf = jnp.zeros((LANES, LANES), F32)
            stb = stf
        for c in chunks:
            stcat_ref[c, :, 0:LANES] = stf.astype(BF16)
            stf = stf * dec_ref[c, 0:1, :] + dstf_ref[c]
        for c in reversed(chunks):
            stcat_ref[c, :, LANES:] = stb.astype(BF16)
            stb = stb * dec_ref[c, 1:2, :] + dstb_ref[c]
        if want_state:
            _pair_state_out(stf, st_view(sq), 0)
            _pair_state_out(stb, st_view(sq), 1)

    def finish():
        for g0 in range(0, nc, HG_GROUP):
            grp = range(g0, g0 + HG_GROUP)
            rows = [slice(c * C, (c + 1) * C) for c in grp]
            inter = [_dot_nt(xi_ref[r, :], stcat_ref[c]) for c, r in zip(grp, rows)]
            outs = [oacc[r, :] + x for r, x in zip(rows, inter)]
            sums = [_head_sums_sq(o, bd) for o in outs]
            for r, o, ms in zip(rows, outs, sums):
                y = o * lax.rsqrt(ms * (1.0 / HEAD_DIM) + LN_EPS)
                o_ref[r, :] = (y * _silu(hg_ref[r, :])).astype(BF16)

    finish()

    @pl.when(jnp.logical_not(jnp.max(gmax) < HG_SAFE_LOG2))
    def _():
        lax.fori_loop(0, nc, exact_body, 0)
        finish()


def _hgrn(proj, gate_params, seq, layer, s0, want_state, carried=None):
    rows = proj.shape[0]
    L = MIX_ROWS
    nc = L // HG_CHUNK
    in_specs, state_spec, out_specs, out_shape = _mixer_specs(
        proj, (COL_HQ, COL_HFF, COL_HFB, COL_HI, COL_HG), seq, layer, s0, want_state, carried)
    in_specs = [pl.BlockSpec((None, None, SUBLANES, LANES), lambda p, b: (layer, p, 0, 0))] + in_specs
    args = [gate_params, proj, proj, proj, proj, proj]
    if s0 is not None:
        in_specs.append(state_spec)
        args.append(s0)
    aliases = _carry_states(in_specs, args, carried)
    kern = functools.partial(_hgrn_kernel, L=L, seq=seq, layer=layer, has_state=s0 is not None,
                             want_state=want_state, carried=carried is not None)
    return pl.pallas_call(
        kern,
        grid=(N_PAIRS, rows // L),
        in_specs=in_specs,
        out_specs=out_specs,
        out_shape=out_shape,
        input_output_aliases=aliases,
        scratch_shapes=[pltpu.VMEM((L, LANES), F32),
                        pltpu.VMEM((L, 2 * LANES), BF16),
                        pltpu.VMEM((nc, LANES, LANES), F32),
                        pltpu.VMEM((nc, LANES, LANES), F32),
                        pltpu.VMEM((nc, SUBLANES, LANES), F32),
                        pltpu.VMEM((nc, LANES, 2 * LANES), BF16),
                        pltpu.VMEM((2 * HG_CHUNK, HG_CHUNK), jnp.int32)]
                       + [pltpu.VMEM((L, LANES), F32)] * 5
                       + [pltpu.VMEM((L, LANES), BF16)] * 2,
        compiler_params=pltpu.CompilerParams(
            dimension_semantics=("arbitrary", "arbitrary"), vmem_limit_bytes=VMEM_LIMIT),
        name="hgrn2",
    )(*args)


def _ffn_kernel(x_ref, of_ref, or_ref, oh_ref, mod_ref, wo_ref, lng_ref, lnb_ref,
                wg_ref, wu_ref, wd_ref, *rest, n_side, mod_row):
    side_in, y_ref, side_out = rest[:n_side], rest[n_side], rest[n_side + 1:]
    g1, sh2, sc2, g2 = _mod_vectors(mod_ref, mod_row)[2:6]
    half = ROW_TILE // 2
    halves = (slice(0, half), slice(half, ROW_TILE))
    mix = [_dot(jnp.concatenate([of_ref[r, :], or_ref[r, :], oh_ref[r, :]], axis=1), wo_ref[...])
           for r in halves]
    x1, gate, up, ffn = [], [], [], []
    for r, m in zip(halves, mix):
        x1.append(_ln(ALPHA * x_ref[r, :] + g1 * m) * lng_ref[0:1, :] + lnb_ref[0:1, :])
        h2 = (_ln(x1[-1]) * (1.0 + sc2) + sh2).astype(BF16)
        gate.append(_dot(h2, wg_ref[...]))
        up.append(_dot(h2, wu_ref[...]))
    for g, u in zip(gate, up):
        act = (_silu(g) * u).astype(BF16)
        ffn.append(_dot(act, wd_ref[...]))
    for r, x, f in zip(halves, x1, ffn):
        y_ref[r, :] = _ln(ALPHA * x + g2 * f) * lng_ref[1:2, :] + lnb_ref[1:2, :]
    _side_casts(side_in + side_out)


def _out_ffn(x2d, o_f, o_r, o_h, mod, w_out_bf, ln_g, ln_b, wg_bf, wu_bf, wd_bf, layer, mod_row, side=()):
    rows = x2d.shape[0]
    nsteps = rows // ROW_TILE
    s_in, s_out, s_shape, s_args = _side_cast_specs(side, nsteps)
    once = pl.Buffered(1)
    row_blk = lambda w: pl.BlockSpec((ROW_TILE, w), lambda i: (i, 0))
    return pl.pallas_call(
        functools.partial(_ffn_kernel, n_side=len(side), mod_row=mod_row),
        grid=(nsteps,),
        in_specs=[row_blk(D_MODEL), row_blk(FNET_WIDTH), row_blk(HEADS_WIDTH), row_blk(HEADS_WIDTH),
                  pl.BlockSpec((None, MOD_ROWS, N_MOD * D_MODEL), lambda i: (layer, 0, 0)),
                  pl.BlockSpec((D_MODEL, D_MODEL), lambda i: (0, 0), pipeline_mode=once),
                  pl.BlockSpec((None, 2, D_MODEL), lambda i: (layer, 0, 0)),
                  pl.BlockSpec((None, 2, D_MODEL), lambda i: (layer, 0, 0)),
                  pl.BlockSpec((D_MODEL, D_FF), lambda i: (0, 0), pipeline_mode=once),
                  pl.BlockSpec((D_MODEL, D_FF), lambda i: (0, 0), pipeline_mode=once),
                  pl.BlockSpec((D_FF, D_MODEL), lambda i: (0, 0), pipeline_mode=once)] + s_in,
        out_specs=[row_blk(D_MODEL)] + s_out,
        out_shape=[jax.ShapeDtypeStruct((rows, D_MODEL), F32)] + s_shape,
        compiler_params=pltpu.CompilerParams(
            dimension_semantics=("arbitrary",), vmem_limit_bytes=VMEM_LIMIT),
        name="out_ffn",
    )(x2d, o_f, o_r, o_h, mod, w_out_bf, ln_g, ln_b, wg_bf, wu_bf, wd_bf, *s_args)


def kernel(x_prompt, x_sample, c, state_ret, state_hgrn, c_ctx, w_mod, b_mod, w_in, w_out,
           ret_log_decay, hg_lower_bound, ln_g, ln_b, w_gate, w_up, w_down):
    B, S, _ = x_prompt.shape
    DB, DS, _ = x_sample.shape

    p = jax.nn.softmax(hg_lower_bound.astype(F32), axis=1)
    cum = jnp.cumsum(p, axis=1)
    lbs = cum - cum[:, :1]
    kinds = jnp.stack([jnp.log(lbs), jnp.log1p(-lbs), 1.0 - lbs], axis=0)
    gp = jnp.transpose(kinds.reshape(3, 2, DEPTH, N_PAIRS, LANES), (2, 3, 0, 1, 4))
    gp = gp.reshape(DEPTH, N_PAIRS, 6, LANES)
    gp = jnp.concatenate([gp, jnp.zeros((DEPTH, N_PAIRS, SUBLANES - 6, LANES), F32)], axis=2)
    lg_flat = (-jnp.exp(ret_log_decay.astype(F32))).reshape(-1)

    cv =jnp.concatenate([c_ctx[None, :], c, jnp.zeros((MOD_ROWS - 1 - DB, D_MODEL), F32)], axis=0)
    mod = _modulation(cv, w_mod, b_mod)

    rope = _rope_tables(DS)

    ctx_row = lambda i: 0
    smp_row = lambda i: 1 + i // (DS // ROW_TILE)

    y = x_prompt.reshape(B * S, D_MODEL)
    z = x_sample.reshape(DB * DS, D_MODEL)
    new_state_ret, new_state_hgrn = None, None
    w_in_bf = w_in[0].astype(BF16)
    w_out_bf = wg_bf = wu_bf = wd_bf = None
    for l in range(DEPTH):
        side_c = [(w_out, l), (w_gate, l)] if l == 0 else []
        side_s = [(w_up, l), (w_down, l)] if l == 0 else []
        proj_c, *cast_c = _inproj(y, mod, w_in_bf, l, ctx_row, side_c)
        proj_s, *cast_s = _inproj(z, mod, w_in_bf, l, smp_row, side_s)
        if l == 0:
            (w_out_bf, wg_bf), (wu_bf, wd_bf) = cast_c, cast_s
        of_c = _fnet(proj_c, S)
        or_c, new_state_ret = _retention(proj_c, lg_flat, S, l, None, None, True, new_state_ret)
        oh_c, new_state_hgrn = _hgrn(proj_c, gp, S, l, None, True, new_state_hgrn)
        of_s = _fnet(proj_s, DS)
        (or_s,) = _retention(proj_s, lg_flat, DS, l, rope, state_ret, False)
        (oh_s,) = _hgrn(proj_s, gp, DS, l, state_hgrn, False)
        last = l + 1 == DEPTH
        side_c = [] if last else [(w_in, l + 1), (w_out, l + 1)]
        side_s = [] if last else [(w_gate, l + 1), (w_up, l + 1), (w_down, l + 1)]
        weights = (w_out_bf, ln_g, ln_b, wg_bf, wu_bf, wd_bf)
        y, *cast_c = _out_ffn(y, of_c, or_c, oh_c, mod, *weights, l, ctx_row, side_c)
        z, *cast_s = _out_ffn(z, of_s, or_s, oh_s, mod, *weights, l, smp_row, side_s)
        if not last:
            (w_in_bf, w_out_bf), (wg_bf, wu_bf, wd_bf) = cast_c, cast_s

    return (y.reshape(B, S, D_MODEL), z.reshape(DB, DS, D_MODEL), new_state_ret, new_state_hgrn)
```

```python
import functools

import numpy as np
import jax
import jax.numpy as jnp
from jax import lax
from jax.experimental import pallas as pl
from jax.experimental.pallas import tpu as pltpu

F32 = jnp.float32
BF16 = jnp.bfloat16

D_MODEL = 1024
DEPTH = 2
GRID_W = 64
FNET_WIDTH = 256
FNET_GDIM = 64
N_HEADS = 6
N_PAIRS = N_HEADS // 2
HEAD_DIM = 64
LANES = 128
SUBLANES = 8
HEADS_WIDTH = N_HEADS * HEAD_DIM
IN_WIDTH = 3712
D_FF = 2816
ROPE_BASE = 10000.0
LN_EPS = 1e-5
LOG2E = 1.4426950408889634
N_MOD = 6
ALPHA = (2 * DEPTH) ** 0.25
MOD_ROWS = 8
ROW_TILE = 512
RET_ROWS = 256
MIX_ROWS = 2048
HG_CHUNK = 128
HG_GROUP = 16
HG_BLOCK = 32
HG_SAFE_LOG2 = 96.0
VMEM_LIMIT = 56 * 1024 * 1024

COL_RQ, COL_RK, COL_RV, COL_RG = 2, 5, 8, 11
COL_HQ, COL_HFF, COL_HFB, COL_HI, COL_HG = 14, 17, 20, 23, 26


def _silu(x):
    return x / (1.0 + jnp.exp(-x))


def _ln(x):
    mu = jnp.mean(x, axis=-1, keepdims=True)
    xc = x - mu
    var = jnp.mean(xc * xc, axis=-1, keepdims=True)
    return xc * lax.rsqrt(var + LN_EPS)


def _dot(a, b):
    return jnp.dot(a, b, preferred_element_type=F32)


def _dot_nt(a, b):
    return lax.dot_general(a, b, (((1,), (1,)), ((), ())), preferred_element_type=F32)


def _split2(x):
    hi = x.astype(BF16)
    lo = (x - hi.astype(F32)).astype(BF16)
    return hi, lo


def _head_sums(x, bd_ones):
    return _dot(x.astype(BF16), bd_ones)


def _head_sums_sq(x, bd_ones):
    return _head_sums(x * x, bd_ones)


def _bd_ones():
    r = lax.broadcasted_iota(jnp.int32, (LANES, LANES), 0)
    c = lax.broadcasted_iota(jnp.int32, (LANES, LANES), 1)
    return jnp.where((r < HEAD_DIM) == (c < HEAD_DIM), 1.0, 0.0).astype(BF16)


def _pair_state_in(s_ref, direction):
    z = jnp.zeros((HEAD_DIM, HEAD_DIM), F32)
    top = jnp.concatenate([s_ref[direction, 0], z], axis=1)
    bot = pltpu.roll(jnp.concatenate([s_ref[direction, 1], z], axis=1), HEAD_DIM, axis=1)
    return jnp.concatenate([top, bot], axis=0).T


def _state_out_view(st_ref, layer, carried):
    if carried:
        return lambda sq: st_ref.at[sq]
    for other in range(DEPTH):
        if other != layer:
            st_ref[:, other] = jnp.zeros(st_ref.shape[:1] + st_ref.shape[2:], F32)
    return lambda sq: st_ref.at[sq, layer]


def _pair_state_out(st, st_ref, direction):
    nat = st.T
    st_ref[direction, 0] = nat[:HEAD_DIM, :HEAD_DIM]
    st_ref[direction, 1] = pltpu.roll(nat[HEAD_DIM:, :], HEAD_DIM, axis=1)[:, :HEAD_DIM]


def _mod_kernel(cv_ref, w_ref, b_ref, o_ref):
    s = _silu(cv_ref[...]).astype(BF16)
    bias = b_ref[pl.ds(pl.program_id(0), 1), :]
    o_ref[...] = _dot(s, w_ref[...].astype(BF16)) + bias


def _modulation(cv, w_mod, b_mod):
    tn = 3072
    width = N_MOD * D_MODEL
    return pl.pallas_call(
        _mod_kernel,
        grid=(DEPTH, width // tn),
        in_specs=[pl.BlockSpec((MOD_ROWS, D_MODEL), lambda l, j: (0, 0)),
                  pl.BlockSpec((None, D_MODEL, tn), lambda l, j: (l, 0, j)),
                  pl.BlockSpec((DEPTH, tn), lambda l, j: (0, j))],
        out_specs=pl.BlockSpec((None, MOD_ROWS, tn), lambda l, j: (l, 0, j)),
        out_shape=jax.ShapeDtypeStruct((DEPTH, MOD_ROWS, width), F32),
        compiler_params=pltpu.CompilerParams(
            dimension_semantics=("arbitrary", "arbitrary"), vmem_limit_bytes=VMEM_LIMIT),
        name="modulation",
    )(cv, w_mod, b_mod)


def _side_cast_specs(side, nsteps):
    in_specs, out_specs, out_shape, args = [], [], [], []
    for w, layer in side:
        _, r, c = w.shape
        in_specs.append(pl.BlockSpec((None, r // nsteps, c), lambda i, layer=layer: (layer, i, 0)))
        out_specs.append(pl.BlockSpec((r // nsteps, c), lambda i: (i, 0)))
        out_shape.append(jax.ShapeDtypeStruct((r, c), BF16))
        args.append(w)
    return in_specs, out_specs, out_shape, args


def _side_casts(side_refs):
    n = len(side_refs) // 2
    for src, dst in zip(side_refs[:n], side_refs[n:]):
        dst[...] = src[...].astype(BF16)


def _mod_vectors(mod_ref, mod_row):
    m = mod_ref[pl.ds(mod_row(pl.program_id(0)), 1), :]
    return [m[:, k * D_MODEL:(k + 1) * D_MODEL] for k in range(N_MOD)]


def _inproj_kernel(x_ref, mod_ref, w_ref, *rest, n_side, mod_row):
    side_in, o_ref, side_out = rest[:n_side], rest[n_side], rest[n_side + 1:]
    sh, sc = _mod_vectors(mod_ref, mod_row)[0:2]
    half = ROW_TILE // 2
    for r in (slice(0, half), slice(half, ROW_TILE)):
        h = _ln(x_ref[r, :]) * (1.0 + sc) + sh
        o_ref[r, :] = _dot(h.astype(BF16), w_ref[...])
    _side_casts(side_in + side_out)


def _inproj(x2d, mod, w_in_bf, layer, mod_row, side=()):
    rows = x2d.shape[0]
    nsteps = rows // ROW_TILE
    s_in, s_out, s_shape, s_args = _side_cast_specs(side, nsteps)
    return pl.pallas_call(
        functools.partial(_inproj_kernel, n_side=len(side), mod_row=mod_row),
        grid=(nsteps,),
        in_specs=[pl.BlockSpec((ROW_TILE, D_MODEL), lambda i: (i, 0)),
                  pl.BlockSpec((None, MOD_ROWS, N_MOD * D_MODEL), lambda i: (layer, 0, 0)),
                  pl.BlockSpec((D_MODEL, IN_WIDTH), lambda i: (0, 0))] + s_in,
        out_specs=[pl.BlockSpec((ROW_TILE, IN_WIDTH), lambda i: (i, 0))] + s_out,
        out_shape=[jax.ShapeDtypeStruct((rows, IN_WIDTH), F32)] + s_shape,
        compiler_params=pltpu.CompilerParams(
            dimension_semantics=("arbitrary",), vmem_limit_bytes=VMEM_LIMIT),
        name="inproj",
    )(x2d, mod, w_in_bf, *s_args)


def _fnet_kernel(u_ref, w1_ref, cl_ref, sl_ref, o_ref, *, seq):
    u = u_ref[...].astype(BF16)
    t = _dot(u, w1_ref[...])
    uc = t[:, :FNET_WIDTH].astype(BF16)
    us = t[:, FNET_WIDTH:].astype(BF16)
    for s in range(u.shape[0] // seq):
        rows = slice(s * seq, (s + 1) * seq)
        o = _dot(cl_ref[...], uc[rows, :]) - _dot(sl_ref[...], us[rows, :])
        o_ref[rows, :] = o.astype(BF16)


def _fnet_consts(L):
    scale = 1.0 / np.sqrt(L * FNET_GDIM)
    k = np.arange(FNET_GDIM)
    ang = 2.0 * np.pi * ((k[:, None] * k[None, :]) % FNET_GDIM) / FNET_GDIM
    eye = np.eye(FNET_WIDTH // FNET_GDIM)
    w1 = np.concatenate([np.kron(eye, np.cos(ang)), np.kron(eye, np.sin(ang))], axis=1) * scale
    n = np.arange(L)
    angl = 2.0 * np.pi * ((n[:, None] * n[None, :]) % L) / L
    tables = (w1, np.cos(angl), np.sin(angl))
    return tuple(jnp.asarray(t, F32).astype(BF16) for t in tables)


def _fnet(proj, L):
    rows = proj.shape[0]
    w1, cl, sl = _fnet_consts(L)
    return pl.pallas_call(
        functools.partial(_fnet_kernel, seq=L),
        grid=(rows // MIX_ROWS,),
        in_specs=[pl.BlockSpec((MIX_ROWS, FNET_WIDTH), lambda b: (b, 0)),
                  pl.BlockSpec((FNET_WIDTH, 2 * FNET_WIDTH), lambda b: (0, 0)),
                  pl.BlockSpec((L, L), lambda b: (0, 0)),
                  pl.BlockSpec((L, L), lambda b: (0, 0))],
        out_specs=pl.BlockSpec((MIX_ROWS, FNET_WIDTH), lambda b: (b, 0)),
        out_shape=jax.ShapeDtypeStruct((rows, FNET_WIDTH), BF16),
        compiler_params=pltpu.CompilerParams(
            dimension_semantics=("arbitrary",), vmem_limit_bytes=VMEM_LIMIT),
        name="fnet",
    )(proj, w1, cl, sl)


def _rope_tables(L):
    half = HEAD_DIM // 2
    inv = ROPE_BASE ** (-np.arange(0, half, 2, dtype=np.float64) / half)
    n = np.arange(L)
    rows, cols = n // GRID_W, n % GRID_W
    lane = np.arange(LANES)
    f = lane % (half // 2)
    use_col = (lane % HEAD_DIM) >= half
    pos = np.where(use_col[None, :], cols[:, None], rows[:, None]).astype(np.float64)
    ang = pos * inv[f][None, :]
    lo = (lane % half) < (half // 2)
    cos = np.cos(ang)
    sin = np.where(lo[None, :], -np.sin(ang), np.sin(ang))
    reps = (MIX_ROWS // L, 1)
    return jnp.asarray(np.tile(cos, reps), F32), jnp.asarray(np.tile(sin, reps), F32)


def _rope(x, cos, sin_signed, lo_mask):
    quarter = HEAD_DIM // 4
    swapped = jnp.where(lo_mask, pltpu.roll(x, LANES - quarter, axis=1), pltpu.roll(x, quarter, axis=1))
    return x * cos + swapped * sin_signed


def _ret_kernel(*refs, L, seq, layer, use_rope, has_state, want_state, carried):
    it = iter(refs)
    lg_ref = next(it)
    q_ref, k_ref, v_ref, g_ref = next(it), next(it), next(it), next(it)
    if use_rope:
        cos_ref, sin_ref = next(it), next(it)
    if has_state:
        s0_ref = next(it)
    if carried:
        next(it)
    o_ref = next(it)
    if want_state:
        st_view = _state_out_view(next(it), layer, carried)
    oacc = next(it)

    p = pl.program_id(0)
    T = RET_ROWS
    nc = L // T
    ncs = seq // T
    lane = lax.broadcasted_iota(jnp.int32, (1, LANES), 1)
    h0 = lane < HEAD_DIM
    bd = _bd_ones()
    bd_mask = bd > 0

    def lg(direction, head):
        return lg_ref[layer * 2 * N_HEADS + direction * N_HEADS + 2 * p + head]

    i = lax.broadcasted_iota(jnp.int32, (T, T), 0)
    j = lax.broadcasted_iota(jnp.int32, (T, T), 1)
    rel = (i - j).astype(F32)
    masks = []
    for hh in range(2):
        arg = jnp.where(rel > 0, rel * lg(0, hh), -rel * lg(1, hh))
        masks.append(jnp.exp(arg) + jnp.where(rel == 0, 1.0, 0.0))
    kscale = HEAD_DIM ** -0.5
    mcat = (jnp.concatenate(masks, axis=0) * kscale).astype(BF16)

    lgf_vec = jnp.where(h0, lg(0, 0), lg(0, 1))
    lgb_vec = jnp.where(h0, lg(1, 0), lg(1, 1))
    pos = lax.broadcasted_iota(jnp.int32, (T, LANES), 0).astype(F32)
    wf = jnp.exp((pos + 1.0) * lgf_vec)
    wb = jnp.exp((float(T) - pos) * lgb_vec)
    zf = jnp.exp((float(T) - 1.0 - pos) * lgf_vec) * kscale
    zb = jnp.exp(pos * lgb_vec) * kscale
    gf_chunk = jnp.exp(float(T) * lgf_vec)
    gb_chunk = jnp.exp(float(T) * lgb_vec)
    use_states = has_state or ncs > 1
    if use_rope:
        lo_mask = (lane % (HEAD_DIM // 2)) < (HEAD_DIM // 4)

    chunk_rows = [slice(c * T, (c + 1) * T) for c in range(nc)]
    qk = []
    for rows in chunk_rows:
        qc = q_ref[rows, :]
        kc = k_ref[rows, :]
        if use_rope:
            qc = _rope(qc, cos_ref[rows, :], sin_ref[rows, :], lo_mask)
            kc = _rope(kc, cos_ref[rows, :], sin_ref[rows, :], lo_mask)
        qk.append((qc, kc))
    scores = []
    for qc, kc in qk:
        qs = jnp.concatenate([jnp.where(h0, qc, 0.0), jnp.where(h0, 0.0, qc)], axis=0).astype(BF16)
        scores.append(_dot_nt(qs, kc.astype(BF16)))
    dstf, dstb, xis = [], [], []
    if use_states or want_state:
        for rows, (qc, kc) in zip(chunk_rows, qk):
            vt = v_ref[rows, :].T.astype(BF16)
            dstf.append(jnp.where(bd_mask, _dot(vt, (kc * zf).astype(BF16)), 0.0))
            dstb.append(jnp.where(bd_mask, _dot(vt, (kc * zb).astype(BF16)), 0.0))
    if use_states:
        for qc, kc in qk:
            xis.append(jnp.concatenate([qc * wf, qc * wb], axis=1).astype(BF16))
    probs = [s.astype(BF16) * mcat for s in scores]
    outs = [_dot(pm, v_ref[rows, :].astype(BF16)) for pm, rows in zip(probs, chunk_rows)]
    for rows, o2 in zip(chunk_rows, outs):
        oacc[rows, :] = jnp.where(h0, o2[:T], o2[T:])

    stf_in, stb_in = [None] * nc, [None] * nc
    if use_states or want_state:
        for sq in range(nc // ncs):
            chunks = range(sq * ncs, (sq + 1) * ncs)
            if has_state:
                stf, stb = _pair_state_in(s0_ref.at[sq], 0), _pair_state_in(s0_ref.at[sq], 1)
            else:
                stf = jnp.zeros((LANES, LANES), F32)
                stb = stf
            for c in chunks:
                stf_in[c] = stf
                stf = stf * gf_chunk + dstf[c]
            for c in reversed(chunks):
                stb_in[c] = stb
                stb = stb * gb_chunk + dstb[c]
            if want_state:
                _pair_state_out(stf, st_view(sq), 0)
                _pair_state_out(stb, st_view(sq), 1)

    outs = [oacc[rows, :] for rows in chunk_rows]
    if use_states:
        inter = [_dot_nt(xis[c], jnp.concatenate([stf_in[c], stb_in[c]], axis=1).astype(BF16))
                 for c in range(nc)]
        outs = [o + x for o, x in zip(outs, inter)]
    means = [_head_sums(o, bd) * (1.0 / HEAD_DIM) for o in outs]
    cent = [o - mu for o, mu in zip(outs, means)]
    variances = [_head_sums_sq(oc, bd) * (1.0 / HEAD_DIM) for oc in cent]
    for rows, oc, var in zip(chunk_rows, cent, variances):
        y = oc * lax.rsqrt(var + LN_EPS)
        o_ref[rows, :] = (y * _silu(g_ref[rows, :])).astype(BF16)


def _mixer_specs(proj, cols, seq, layer, s0, want_state, carried):
    rows = proj.shape[0]
    blk = lambda col: pl.BlockSpec((MIX_ROWS, LANES), lambda p, b, col=col: (b, col + p))
    in_specs = [blk(col) for col in cols]
    state_spec = None
    if s0 is not None:
        state_spec = pl.BlockSpec((MIX_ROWS // seq, None, 2, 2, HEAD_DIM, HEAD_DIM),
                                  lambda p, b: (b, layer, 0, p, 0, 0))
    out_specs = [pl.BlockSpec((MIX_ROWS, LANES), lambda p, b: (b, p))]
    out_shape = [jax.ShapeDtypeStruct((rows, HEADS_WIDTH), BF16)]
    if want_state:
        if carried is None:
            out_specs.append(pl.BlockSpec((MIX_ROWS // seq, DEPTH, 2, 2, HEAD_DIM, HEAD_DIM),
                                          lambda p, b: (b, 0, 0, p, 0, 0)))
        else:
            out_specs.append(pl.BlockSpec((MIX_ROWS // seq, None, 2, 2, HEAD_DIM, HEAD_DIM),
                                          lambda p, b: (b, layer, 0, p, 0, 0)))
        out_shape.append(jax.ShapeDtypeStruct((rows // seq, DEPTH, 2, N_HEADS, HEAD_DIM, HEAD_DIM), F32))
    return in_specs, state_spec, out_specs, out_shape


def _carry_states(in_specs, args, carried):
    if carried is None:
        return {}
    in_specs.append(pl.BlockSpec(memory_space=pl.ANY))
    args.append(carried)
    return {len(args) - 1: 1}


def _retention(proj, lg_flat, seq, layer, rope, s0, want_state, carried=None):
    rows = proj.shape[0]
    in_specs, state_spec, out_specs, out_shape = _mixer_specs(
        proj, (COL_RQ, COL_RK, COL_RV, COL_RG), seq, layer, s0, want_state, carried)
    in_specs = [pl.BlockSpec(memory_space=pltpu.SMEM)] + in_specs
    args = [lg_flat, proj, proj, proj, proj]
    if rope is not None:
        in_specs += [pl.BlockSpec((MIX_ROWS, LANES), lambda p, b: (0, 0))] * 2
        args += list(rope)
    if s0 is not None:
        in_specs.append(state_spec)
        args.append(s0)
    aliases = _carry_states(in_specs, args, carried)
    kern = functools.partial(_ret_kernel, L=MIX_ROWS, seq=seq, layer=layer, use_rope=rope is not None,
                             has_state=s0 is not None, want_state=want_state, carried=carried is not None)
    return pl.pallas_call(
        kern,
        grid=(N_PAIRS, rows // MIX_ROWS),
        in_specs=in_specs,
        out_specs=out_specs,
        out_shape=out_shape,
        input_output_aliases=aliases,
        scratch_shapes=[pltpu.VMEM((MIX_ROWS, LANES), F32)],
        compiler_params=pltpu.CompilerParams(
            dimension_semantics=("arbitrary", "arbitrary"), vmem_limit_bytes=VMEM_LIMIT),
        name="retention",
    )(*args)


def _hgrn_gate(x, llb, l1mlb, omlb, zero_bound):
    z = jnp.exp(-jnp.abs(x))
    zp1 = 1.0 + z
    ls = jnp.minimum(x, 0.0) - jnp.log(zp1)
    sig_neg = jnp.where(x > 0.0, z, 1.0) / zp1
    if zero_bound:
        return ls, sig_neg
    t = l1mlb + ls
    logf = jnp.maximum(llb, t) + jnp.log(1.0 + jnp.exp(-jnp.abs(llb - t)))
    return logf, omlb * sig_neg


def _hgrn_kernel(*refs, L, seq, layer, has_state, want_state, carried):
    it = iter(refs)
    gp_ref = next(it)
    hq_ref, ff_ref, fb_ref, hi_ref, hg_ref = next(it), next(it), next(it), next(it), next(it)
    if has_state:
        s0_ref = next(it)
    if carried:
        next(it)
    o_ref = next(it)
    if want_state:
        st_view = _state_out_view(next(it), layer, carried)
    oacc, xi_ref, dstf_ref, dstb_ref, dec_ref, stcat_ref, lvl_ref = (next(it) for _ in range(7))
    q_s, kf_s, kb_s, bf_s, pb_s = (next(it) for _ in range(5))
    kfs_s, kbs_s = next(it), next(it)

    C = HG_CHUNK
    nc = L // C
    ncs = seq // C
    lane = lax.broadcasted_iota(jnp.int32, (1, LANES), 1)
    h0 = lane < HEAD_DIM
    bd = _bd_ones()
    bd_mask = bd > 0
    rowi = lax.broadcasted_iota(jnp.int32, (C, C), 0)
    coli = lax.broadcasted_iota(jnp.int32, (C, C), 1)
    tri = jnp.where(rowi >= coli, 1.0, 0.0).astype(BF16)
    sub = lax.broadcasted_iota(jnp.int32, (C // SUBLANES, SUBLANES, LANES), 1)
    row2 = lax.broadcasted_iota(jnp.int32, (2 * C, C), 0) % C
    col2 = lax.broadcasted_iota(jnp.int32, (2 * C, C), 1)
    lvl_ref[...] = 31 - lax.clz(row2 ^ col2)

    llb_f, llb_b = gp_ref[0:1, :], gp_ref[1:2, :]
    l1m_f, l1m_b = gp_ref[2:3, :], gp_ref[3:4, :]
    oml_f, oml_b = gp_ref[4:5, :], gp_ref[5:6, :]

    def to3(x):
        return x.reshape(C // SUBLANES, SUBLANES, LANES)

    def prep_gates(c):
        rows = pl.ds(c * C, C)
        q = _silu(hq_ref[rows, :])
        lff, kf = _hgrn_gate(ff_ref[rows, :], llb_f, l1m_f, oml_f, layer == 0)
        lfb, kb = _hgrn_gate(fb_ref[rows, :], llb_b, l1m_b, oml_b, layer == 0)
        lf2 = jnp.concatenate([lff, lfb], axis=1) * LOG2E
        return q, kf, kb, lf2, _split2(lf2)

    def prep_cumsum(vals):
        hi, lo = vals[4]
        return _dot(tri, hi) + _dot(tri, lo)

    def prep_decay(c, gmax, vals, cs):
        rows = pl.ds(c * C, C)
        q, kf, kb, lf2, _ = vals
        bf = cs[:, :LANES]
        bb_incl = cs[:, LANES:]
        pb = bb_incl - lf2[:, LANES:]
        bf_last = bf[C - 1:C, :]
        tot_b = bb_incl[C - 1:C, :]
        q_s[rows, :] = q
        kf_s[rows, :] = kf
        kb_s[rows, :] = kb
        bf_s[rows, :] = bf
        pb_s[rows, :] = pb
        dec_ref[c, 0:1, :] = jnp.exp2(bf_last)
        dec_ref[c, 1:2, :] = jnp.exp2(tot_b)
        xi_ref[rows, 0:LANES] = (q * jnp.exp2(bf)).astype(BF16)
        xi_ref[rows, LANES:] = (q * jnp.exp2(tot_b - pb)).astype(BF16)
        for blk in range(C // HG_BLOCK):
            s, e = blk * HG_BLOCK, (blk + 1) * HG_BLOCK - 1
            gmax = jnp.maximum(gmax, bf[s:s + 1, :] - bf[e:e + 1, :])
            gmax = jnp.maximum(gmax, pb[s:s + 1, :] - pb[e:e + 1, :])
        kfs = (kf * jnp.exp2(bf_last - bf)).astype(BF16)
        kbs = (kb * jnp.exp2(pb)).astype(BF16)
        kfs_s[rows, :] = kfs
        kbs_s[rows, :] = kbs
        vt = hi_ref[rows, :].T.astype(BF16)
        return gmax, (vt, kfs, kbs)

    def prep_states(c, ops):
        vt, kfs, kbs = ops
        dstf_ref[c] = jnp.where(bd_mask, _dot(vt, kfs), 0.0)
        dstb_ref[c] = jnp.where(bd_mask, _dot(vt, kbs), 0.0)

    jm = (lax.broadcasted_iota(jnp.int32, (2 * HG_BLOCK, C), 1)
          - lax.broadcasted_iota(jnp.int32, (2 * HG_BLOCK, C), 0) % HG_BLOCK)

    def fast_pack(c):
        rows = pl.ds(c * C, C)
        q, kf, kb, bf, pb = q_s[rows, :], kf_s[rows, :], kb_s[rows, :], bf_s[rows, :], pb_s[rows, :]
        packed = []
        for blk in range(C // HG_BLOCK):
            s, e = blk * HG_BLOCK, (blk + 1) * HG_BLOCK
            rf = bf[e - 1:e, :]
            rb_ = pb[s:s + 1, :]
            qf = q[s:e, :] * jnp.exp2(bf[s:e, :] - rf)
            qb = q[s:e, :] * jnp.exp2(rb_ - pb[s:e, :])
            x = jnp.concatenate([jnp.where(h0, qf, 0.0), jnp.where(h0, 0.0, qf),
                                 jnp.where(h0, qb, 0.0), jnp.where(h0, 0.0, qb)], axis=0).astype(BF16)
            if e == C:
                kfv = kfs_s[rows, :]
            else:
                kfv = (kf[:e, :] * jnp.exp2(rf - bf[:e, :])).astype(BF16)
            if s == 0:
                kbv = kbs_s[rows, :]
            else:
                kbv = (kb[s:, :] * jnp.exp2(pb[s:, :] - rb_)).astype(BF16)
            parts = [kfv]
            if e < C:
                parts.append(jnp.zeros((C - e, LANES), BF16))
            if s > 0:
                parts.append(jnp.zeros((s, LANES), BF16))
            parts.append(kbv)
            packed.append((x, jnp.concatenate(parts, axis=0)))
        return packed

    def fast_scores(packed):
        return [_dot_nt(x, w) for x, w in packed]

    def fast_mask(scores):
        a_rows = []
        for blk, sc in enumerate(scores):
            s = blk * HG_BLOCK
            af = sc[:2 * HG_BLOCK, :C]
            ab = sc[2 * HG_BLOCK:, C:]
            a_rows.append(jnp.where(jm < s, af, jnp.where(jm > s, ab, af + ab)))
        return jnp.concatenate(a_rows, axis=0).astype(BF16)

    def fast_store(c, o2):
        o_blocks = []
        for blk in range(C // HG_BLOCK):
            base = blk * 2 * HG_BLOCK
            o_blocks.append(jnp.where(h0, o2[base:base + HG_BLOCK], o2[base + HG_BLOCK:base + 2 * HG_BLOCK]))
        oacc[pl.ds(c * C, C), :] = jnp.concatenate(o_blocks, axis=0)

    def exact_body(c, carry):
        rows = pl.ds(pl.multiple_of(c * C, C), C)
        q, kf, kb, bf, pb = q_s[rows, :], kf_s[rows, :], kb_s[rows, :], bf_s[rows, :], pb_s[rows, :]
        v = hi_ref[rows, :]
        vb = v.astype(BF16)

        d0 = (q * (kf + kb)).astype(BF16)
        o = _dot(d0, bd) * v
        q3, v3, bf3, pb3, kf3, kb3 = to3(q), to3(v), to3(bf), to3(pb), to3(kf), to3(kb)
        nbf3, npb3 = -bf3, -pb3
        for dl in range(1, SUBLANES):
            src_fw = sub < (SUBLANES - dl)
            u = jnp.where(src_fw, nbf3, pb3)
            kmix = jnp.where(src_fw, kf3, kb3)
            w = jnp.where(sub >= dl, bf3, npb3)
            rel = w + pltpu.roll(u, dl, axis=1)
            d = (q3 * pltpu.roll(kmix, dl, axis=1) * jnp.exp2(rel)).reshape(C, LANES).astype(BF16)
            o = o + _dot(d, bd) * pltpu.roll(v3, dl, axis=1).reshape(C, LANES)

        lvl = lvl_ref[...]
        a_tot = jnp.zeros((2 * C, C), F32)
        m = SUBLANES
        while m < C:
            nblk = C // (2 * m)

            def halves(x):
                x4 = x.reshape(nblk, 2, m, LANES)
                return x4[:, 0], x4[:, 1]

            def join(lo, up):
                return jnp.stack([lo, up], axis=1).reshape(C, LANES)

            q_lo, q_up = halves(q)
            kf_lo, _ = halves(kf)
            _, kb_up = halves(kb)
            bf_lo, bf_up = halves(bf)
            pb_lo, pb_up = halves(pb)
            rf = bf_lo[:, m - 1:m, :]
            rb_ = pb_up[:, 0:1, :]
            xq = join(q_lo * jnp.exp2(rb_ - pb_lo), q_up * jnp.exp2(bf_up - rf))
            yk = join(kf_lo * jnp.exp2(rf - bf_lo), kb_up * jnp.exp2(pb_up - rb_))
            xs = jnp.concatenate([jnp.where(h0, xq, 0.0), jnp.where(h0, 0.0, xq)], axis=0).astype(BF16)
            a = _dot_nt(xs, yk.astype(BF16))
            a_tot = jnp.where(lvl == (m.bit_length() - 1), a, a_tot)
            m *= 2
        o2 = _dot(a_tot.astype(BF16), vb)
        o = o + jnp.where(h0, o2[:C], o2[C:])
        oacc[rows, :] = o
        return carry

    gmax = jnp.zeros((1, LANES), F32)
    for g0 in range(0, nc, HG_GROUP):
        grp = range(g0, g0 + HG_GROUP)
        vals = [prep_gates(c) for c in grp]
        sums = [prep_cumsum(vv) for vv in vals]
        ops = []
        for c, vv, cs in zip(grp, vals, sums):
            gmax, op = prep_decay(c, gmax, vv, cs)
            ops.append(op)
        for c, op in zip(grp, ops):
            prep_states(c, op)
    for g0 in range(0, nc, HG_GROUP):
        grp = range(g0, g0 + HG_GROUP)
        packed = [fast_pack(c) for c in grp]
        scores = [fast_scores(pk) for pk in packed]
        amats = [fast_mask(sc) for sc in scores]
        outs = [_dot(a, hi_ref[pl.ds(c * C, C), :].astype(BF16)) for a, c in zip(amats, grp)]
        for c, o2 in zip(grp, outs):
            fast_store(c, o2)

    for sq in range(nc // ncs):
        chunks = range(sq * ncs, (sq + 1) * ncs)
        if has_state:
            stf, stb = _pair_state_in(s0_ref.at[sq], 0), _pair_state_in(s0_ref.at[sq], 1)
        else:
            stf = jnp.zeros((LANES, LANES), F32)
            stb = stf
        for c in chunks:
            stcat_ref[c, :, 0:LANES] = stf.astype(BF16)
            stf = stf * dec_ref[c, 0:1, :] + dstf_ref[c]
        for c in reversed(chunks):
            stcat_ref[c, :, LANES:] = stb.astype(BF16)
            stb = stb * dec_ref[c, 1:2, :] + dstb_ref[c]
        if want_state:
            _pair_state_out(stf, st_view(sq), 0)
            _pair_state_out(stb, st_view(sq), 1)

    def finish():
        for g0 in range(0, nc, HG_GROUP):
            grp = range(g0, g0 + HG_GROUP)
            rows = [slice(c * C, (c + 1) * C) for c in grp]
            inter = [_dot_nt(xi_ref[r, :], stcat_ref[c]) for c, r in zip(grp, rows)]
            outs = [oacc[r, :] + x for r, x in zip(rows, inter)]
            sums = [_head_sums_sq(o, bd) for o in outs]
            for r, o, ms in zip(rows, outs, sums):
                y = o * lax.rsqrt(ms * (1.0 / HEAD_DIM) + LN_EPS)
                o_ref[r, :] = (y * _silu(hg_ref[r, :])).astype(BF16)

    finish()

    @pl.when(jnp.logical_not(jnp.max(gmax) < HG_SAFE_LOG2))
    def _():
        lax.fori_loop(0, nc, exact_body, 0)
        finish()


def _hgrn(proj, gate_params, seq, layer, s0, want_state, carried=None):
    rows = proj.shape[0]
    L = MIX_ROWS
    nc = L // HG_CHUNK
    in_specs, state_spec, out_specs, out_shape = _mixer_specs(
        proj, (COL_HQ, COL_HFF, COL_HFB, COL_HI, COL_HG), seq, layer, s0, want_state, carried)
    in_specs = [pl.BlockSpec((None, None, SUBLANES, LANES), lambda p, b: (layer, p, 0, 0))] + in_specs
    args = [gate_params, proj, proj, proj, proj, proj]
    if s0 is not None:
        in_specs.append(state_spec)
        args.append(s0)
    aliases = _carry_states(in_specs, args, carried)
    kern = functools.partial(_hgrn_kernel, L=L, seq=seq, layer=layer, has_state=s0 is not None,
                             want_state=want_state, carried=carried is not None)
    return pl.pallas_call(
        kern,
        grid=(N_PAIRS, rows // L),
        in_specs=in_specs,
        out_specs=out_specs,
        out_shape=out_shape,
        input_output_aliases=aliases,
        scratch_shapes=[pltpu.VMEM((L, LANES), F32),
                        pltpu.VMEM((L, 2 * LANES), BF16),
                        pltpu.VMEM((nc, LANES, LANES), F32),
                        pltpu.VMEM((nc, LANES, LANES), F32),
                        pltpu.VMEM((nc, SUBLANES, LANES), F32),
                        pltpu.VMEM((nc, LANES, 2 * LANES), BF16),
                        pltpu.VMEM((2 * HG_CHUNK, HG_CHUNK), jnp.int32)]
                       + [pltpu.VMEM((L, LANES), F32)] * 5
                       + [pltpu.VMEM((L, LANES), BF16)] * 2,
        compiler_params=pltpu.CompilerParams(
            dimension_semantics=("arbitrary", "arbitrary"), vmem_limit_bytes=VMEM_LIMIT),
        name="hgrn2",
    )(*args)


def _ffn_kernel(x_ref, of_ref, or_ref, oh_ref, mod_ref, wo_ref, lng_ref, lnb_ref,
                wg_ref, wu_ref, wd_ref, *rest, n_side, mod_row):
    side_in, y_ref, side_out = rest[:n_side], rest[n_side], rest[n_side + 1:]
    g1, sh2, sc2, g2 = _mod_vectors(mod_ref, mod_row)[2:6]
    half = ROW_TILE // 2
    halves = (slice(0, half), slice(half, ROW_TILE))
    mix = [_dot(jnp.concatenate([of_ref[r, :], or_ref[r, :], oh_ref[r, :]], axis=1), wo_ref[...])
           for r in halves]
    x1, gate, up, ffn = [], [], [], []
    for r, m in zip(halves, mix):
        x1.append(_ln(ALPHA * x_ref[r, :] + g1 * m) * lng_ref[0:1, :] + lnb_ref[0:1, :])
        h2 = (_ln(x1[-1]) * (1.0 + sc2) + sh2).astype(BF16)
        gate.append(_dot(h2, wg_ref[...]))
        up.append(_dot(h2, wu_ref[...]))
    for g, u in zip(gate, up):
        act = (_silu(g) * u).astype(BF16)
        ffn.append(_dot(act, wd_ref[...]))
    for r, x, f in zip(halves, x1, ffn):
        y_ref[r, :] = _ln(ALPHA * x + g2 * f) * lng_ref[1:2, :] + lnb_ref[1:2, :]
    _side_casts(side_in + side_out)


def _out_ffn(x2d, o_f, o_r, o_h, mod, w_out_bf, ln_g, ln_b, wg_bf, wu_bf, wd_bf, layer, mod_row, side=()):
    rows = x2d.shape[0]
    nsteps = rows // ROW_TILE
    s_in, s_out, s_shape, s_args = _side_cast_specs(side, nsteps)
    once = pl.Buffered(1)
    row_blk = lambda w: pl.BlockSpec((ROW_TILE, w), lambda i: (i, 0))
    return pl.pallas_call(
        functools.partial(_ffn_kernel, n_side=len(side), mod_row=mod_row),
        grid=(nsteps,),
        in_specs=[row_blk(D_MODEL), row_blk(FNET_WIDTH), row_blk(HEADS_WIDTH), row_blk(HEADS_WIDTH),
                  pl.BlockSpec((None, MOD_ROWS, N_MOD * D_MODEL), lambda i: (layer, 0, 0)),
                  pl.BlockSpec((D_MODEL, D_MODEL), lambda i: (0, 0), pipeline_mode=once),
                  pl.BlockSpec((None, 2, D_MODEL), lambda i: (layer, 0, 0)),
                  pl.BlockSpec((None, 2, D_MODEL), lambda i: (layer, 0, 0)),
                  pl.BlockSpec((D_MODEL, D_FF), lambda i: (0, 0), pipeline_mode=once),
                  pl.BlockSpec((D_MODEL, D_FF), lambda i: (0, 0), pipeline_mode=once),
                  pl.BlockSpec((D_FF, D_MODEL), lambda i: (0, 0), pipeline_mode=once)] + s_in,
        out_specs=[row_blk(D_MODEL)] + s_out,
        out_shape=[jax.ShapeDtypeStruct((rows, D_MODEL), F32)] + s_shape,
        compiler_params=pltpu.CompilerParams(
            dimension_semantics=("arbitrary",), vmem_limit_bytes=VMEM_LIMIT),
        name="out_ffn",
    )(x2d, o_f, o_r, o_h, mod, w_out_bf, ln_g, ln_b, wg_bf, wu_bf, wd_bf, *s_args)


def kernel(x_prompt, x_sample, c, state_ret, state_hgrn, c_ctx, w_mod, b_mod, w_in, w_out,
           ret_log_decay, hg_lower_bound, ln_g, ln_b, w_gate, w_up, w_down):
    B, S, _ = x_prompt.shape
    DB, DS, _ = x_sample.shape

    p = jax.nn.softmax(hg_lower_bound.astype(F32), axis=1)
    cum = jnp.cumsum(p, axis=1)
    lbs = cum - cum[:, :1]
    kinds = jnp.stack([jnp.log(lbs), jnp.log1p(-lbs), 1.0 - lbs], axis=0)
    gp = jnp.transpose(kinds.reshape(3, 2, DEPTH, N_PAIRS, LANES), (2, 3, 0, 1, 4))
    gp = gp.reshape(DEPTH, N_PAIRS, 6, LANES)
    gp = jnp.concatenate([gp, jnp.zeros((DEPTH, N_PAIRS, SUBLANES - 6, LANES), F32)], axis=2)
    lg_flat = (-jnp.exp(ret_log_decay.astype(F32))).reshape(-1)

    cv =jnp.concatenate([c_ctx[None, :], c, jnp.zeros((MOD_ROWS - 1 - DB, D_MODEL), F32)], axis=0)
    mod = _modulation(cv, w_mod, b_mod)

    rope = _rope_tables(DS)

    ctx_row = lambda i: 0
    smp_row = lambda i: 1 + i // (DS // ROW_TILE)

    y = x_prompt.reshape(B * S, D_MODEL)
    z = x_sample.reshape(DB * DS, D_MODEL)
    new_state_ret, new_state_hgrn = None, None
    w_in_bf = w_in[0].astype(BF16)
    w_out_bf = wg_bf = wu_bf = wd_bf = None
    for l in range(DEPTH):
        side_c = [(w_out, l), (w_gate, l)] if l == 0 else []
        side_s = [(w_up, l), (w_down, l)] if l == 0 else []
        proj_c, *cast_c = _inproj(y, mod, w_in_bf, l, ctx_row, side_c)
        proj_s, *cast_s = _inproj(z, mod, w_in_bf, l, smp_row, side_s)
        if l == 0:
            (w_out_bf, wg_bf), (wu_bf, wd_bf) = cast_c, cast_s
        of_c = _fnet(proj_c, S)
        or_c, new_state_ret = _retention(proj_c, lg_flat, S, l, None, None, True, new_state_ret)
        oh_c, new_state_hgrn = _hgrn(proj_c, gp, S, l, None, True, new_state_hgrn)
        of_s = _fnet(proj_s, DS)
        (or_s,) = _retention(proj_s, lg_flat, DS, l, rope, state_ret, False)
        (oh_s,) = _hgrn(proj_s, gp, DS, l, state_hgrn, False)
        last = l + 1 == DEPTH
        side_c = [] if last else [(w_in, l + 1), (w_out, l + 1)]
        side_s = [] if last else [(w_gate, l + 1), (w_up, l + 1), (w_down, l + 1)]
        weights = (w_out_bf, ln_g, ln_b, wg_bf, wu_bf, wd_bf)
        y, *cast_c = _out_ffn(y, of_c, or_c, oh_c, mod, *weights, l, ctx_row, side_c)
        z, *cast_s = _out_ffn(z, of_s, or_s, oh_s, mod, *weights, l, smp_row, side_s)
        if not last:
            (w_in_bf, w_out_bf), (wg_bf, wu_bf, wd_bf) = cast_c, cast_s

    return (y.reshape(B, S, D_MODEL), z.reshape(DB, DS, D_MODEL), new_state_ret, new_state_hgrn)
```

```python
import functools

import numpy as np
import jax
import jax.numpy as jnp
from jax import lax
from jax.experimental import pallas as pl
from jax.experimental.pallas import tpu as pltpu

F32 = jnp.float32
BF16 = jnp.bfloat16

D_MODEL = 1024
DEPTH = 2
GRID_W = 64
FNET_WIDTH = 256
FNET_GDIM = 64
N_HEADS = 6
N_PAIRS = N_HEADS // 2
HEAD_DIM = 64
LANES = 128
SUBLANES = 8
HEADS_WIDTH = N_HEADS * HEAD_DIM
IN_WIDTH = 3712
D_FF = 2816
ROPE_BASE = 10000.0
LN_EPS = 1e-5
LOG2E = 1.4426950408889634
N_MOD = 6
ALPHA = (2 * DEPTH) ** 0.25
MOD_ROWS = 8
ROW_TILE = 512
RET_ROWS = 256
MIX_ROWS = 2048
HG_CHUNK = 128
HG_GROUP = 16
HG_BLOCK = 32
HG_SAFE_LOG2 = 96.0
VMEM_LIMIT = 56 * 1024 * 1024

COL_RQ, COL_RK, COL_RV, COL_RG = 2, 5, 8, 11
COL_HQ, COL_HFF, COL_HFB, COL_HI, COL_HG = 14, 17, 20, 23, 26


def _silu(x):
    return x / (1.0 + jnp.exp(-x))


def _ln(x):
    mu = jnp.mean(x, axis=-1, keepdims=True)
    xc = x - mu
    var = jnp.mean(xc * xc, axis=-1, keepdims=True)
    return xc * lax.rsqrt(var + LN_EPS)


def _dot(a, b):
    return jnp.dot(a, b, preferred_element_type=F32)


def _dot_nt(a, b):
    return lax.dot_general(a, b, (((1,), (1,)), ((), ())), preferred_element_type=F32)


def _split2(x):
    hi = x.astype(BF16)
    lo = (x - hi.astype(F32)).astype(BF16)
    return hi, lo


def _head_sums(x, bd_ones):
    return _dot(x.astype(BF16), bd_ones)


def _head_sums_sq(x, bd_ones):
    return _head_sums(x * x, bd_ones)


def _bd_ones():
    r = lax.broadcasted_iota(jnp.int32, (LANES, LANES), 0)
    c = lax.broadcasted_iota(jnp.int32, (LANES, LANES), 1)
    return jnp.where((r < HEAD_DIM) == (c < HEAD_DIM), 1.0, 0.0).astype(BF16)


def _pair_state_in(s_ref, direction):
    z = jnp.zeros((HEAD_DIM, HEAD_DIM), F32)
    top = jnp.concatenate([s_ref[direction, 0], z], axis=1)
    bot = pltpu.roll(jnp.concatenate([s_ref[direction, 1], z], axis=1), HEAD_DIM, axis=1)
    return jnp.concatenate([top, bot], axis=0).T


def _state_out_view(st_ref, layer, carried):
    if carried:
        return lambda sq: st_ref.at[sq]
    for other in range(DEPTH):
        if other != layer:
            st_ref[:, other] = jnp.zeros(st_ref.shape[:1] + st_ref.shape[2:], F32)
    return lambda sq: st_ref.at[sq, layer]


def _pair_state_out(st, st_ref, direction):
    nat = st.T
    st_ref[direction, 0] = nat[:HEAD_DIM, :HEAD_DIM]
    st_ref[direction, 1] = pltpu.roll(nat[HEAD_DIM:, :], HEAD_DIM, axis=1)[:, :HEAD_DIM]


def _mod_block(cv_ref, w_ref, b_ref, o_ref, layer):
    s = _silu(cv_ref[...]).astype(BF16)
    o_ref[...] = _dot(s, w_ref[...].astype(BF16)) + b_ref[layer:layer + 1, :]


def _mod_specs(layer, tn, col):
    return ([pl.BlockSpec((MOD_ROWS, D_MODEL), lambda *g: (0, 0)),
             pl.BlockSpec((None, D_MODEL, tn), lambda *g: (layer, 0, col(*g))),
             pl.BlockSpec((DEPTH, tn), lambda *g: (0, col(*g)))],
            pl.BlockSpec((MOD_ROWS, tn), lambda *g: (0, col(*g))))


def _modulation(cv, w_mod, b_mod, layer):
    tn = 3072
    width = N_MOD * D_MODEL
    in_specs, out_spec = _mod_specs(layer, tn, lambda j: j)
    return pl.pallas_call(
        functools.partial(_mod_block, layer=layer),
        grid=(width // tn,),
        in_specs=in_specs,
        out_specs=out_spec,
        out_shape=jax.ShapeDtypeStruct((MOD_ROWS, width), F32),
        compiler_params=pltpu.CompilerParams(
            dimension_semantics=("arbitrary",), vmem_limit_bytes=VMEM_LIMIT),
        name="modulation",
    )(cv, w_mod, b_mod)


def _side_cast_specs(side, nsteps):
    in_specs, out_specs, out_shape, args = [], [], [], []
    for w, layer in side:
        _, r, c = w.shape
        in_specs.append(pl.BlockSpec((None, r // nsteps, c), lambda i, layer=layer: (layer, i, 0)))
        out_specs.append(pl.BlockSpec((r // nsteps, c), lambda i: (i, 0)))
        out_shape.append(jax.ShapeDtypeStruct((r, c), BF16))
        args.append(w)
    return in_specs, out_specs, out_shape, args


def _side_casts(side_refs):
    n = len(side_refs) // 2
    for src, dst in zip(side_refs[:n], side_refs[n:]):
        dst[...] = src[...].astype(BF16)


def _mod_vectors(mod_ref, mod_row):
    m = mod_ref[pl.ds(mod_row(pl.program_id(0)), 1), :]
    return [m[:, k * D_MODEL:(k + 1) * D_MODEL] for k in range(N_MOD)]


def _inproj_kernel(x_ref, mod_ref, w_ref, *rest, n_side, mod_row):
    side_in, o_ref, side_out = rest[:n_side], rest[n_side], rest[n_side + 1:]
    sh, sc = _mod_vectors(mod_ref, mod_row)[0:2]
    half = ROW_TILE // 2
    for r in (slice(0, half), slice(half, ROW_TILE)):
        h = _ln(x_ref[r, :]) * (1.0 + sc) + sh
        o_ref[r, :] = _dot(h.astype(BF16), w_ref[...])
    _side_casts(side_in + side_out)


def _inproj(x2d, mod, w_in_bf, layer, mod_row, side=()):
    rows = x2d.shape[0]
    nsteps = rows // ROW_TILE
    s_in, s_out, s_shape, s_args = _side_cast_specs(side, nsteps)
    return pl.pallas_call(
        functools.partial(_inproj_kernel, n_side=len(side), mod_row=mod_row),
        grid=(nsteps,),
        in_specs=[pl.BlockSpec((ROW_TILE, D_MODEL), lambda i: (i, 0)),
                  pl.BlockSpec((MOD_ROWS, N_MOD * D_MODEL), lambda i: (0, 0)),
                  pl.BlockSpec((D_MODEL, IN_WIDTH), lambda i: (0, 0))] + s_in,
        out_specs=[pl.BlockSpec((ROW_TILE, IN_WIDTH), lambda i: (i, 0))] + s_out,
        out_shape=[jax.ShapeDtypeStruct((rows, IN_WIDTH), F32)] + s_shape,
        compiler_params=pltpu.CompilerParams(
            dimension_semantics=("arbitrary",), vmem_limit_bytes=VMEM_LIMIT),
        name="inproj",
    )(x2d, mod, w_in_bf, *s_args)


def _fnet_kernel(u_ref, w1_ref, cl_ref, sl_ref, o_ref, *, seq):
    u = u_ref[...].astype(BF16)
    t = _dot(u, w1_ref[...])
    uc = t[:, :FNET_WIDTH].astype(BF16)
    us = t[:, FNET_WIDTH:].astype(BF16)
    for s in range(u.shape[0] // seq):
        rows = slice(s * seq, (s + 1) * seq)
        o = _dot(cl_ref[...], uc[rows, :]) - _dot(sl_ref[...], us[rows, :])
        o_ref[rows, :] = o.astype(BF16)


def _fnet_consts(L):
    scale = 1.0 / np.sqrt(L * FNET_GDIM)
    k = np.arange(FNET_GDIM)
    ang = 2.0 * np.pi * ((k[:, None] * k[None, :]) % FNET_GDIM) / FNET_GDIM
    eye = np.eye(FNET_WIDTH // FNET_GDIM)
    w1 = np.concatenate([np.kron(eye, np.cos(ang)), np.kron(eye, np.sin(ang))], axis=1) * scale
    n = np.arange(L)
    angl = 2.0 * np.pi * ((n[:, None] * n[None, :]) % L) / L
    tables = (w1, np.cos(angl), np.sin(angl))
    return tuple(jnp.asarray(t, F32).astype(BF16) for t in tables)


def _fnet(proj, L):
    rows = proj.shape[0]
    w1, cl, sl = _fnet_consts(L)
    return pl.pallas_call(
        functools.partial(_fnet_kernel, seq=L),
        grid=(rows // MIX_ROWS,),
        in_specs=[pl.BlockSpec((MIX_ROWS, FNET_WIDTH), lambda b: (b, 0)),
                  pl.BlockSpec((FNET_WIDTH, 2 * FNET_WIDTH), lambda b: (0, 0)),
                  pl.BlockSpec((L, L), lambda b: (0, 0)),
                  pl.BlockSpec((L, L), lambda b: (0, 0))],
        out_specs=pl.BlockSpec((MIX_ROWS, FNET_WIDTH), lambda b: (b, 0)),
        out_shape=jax.ShapeDtypeStruct((rows, FNET_WIDTH), BF16),
        compiler_params=pltpu.CompilerParams(
            dimension_semantics=("arbitrary",), vmem_limit_bytes=VMEM_LIMIT),
        name="fnet",
    )(proj, w1, cl, sl)


def _rope_tables(L):
    half = HEAD_DIM // 2
    inv = ROPE_BASE ** (-np.arange(0, half, 2, dtype=np.float64) / half)
    n = np.arange(L)
    rows, cols = n // GRID_W, n % GRID_W
    lane = np.arange(LANES)
    f = lane % (half // 2)
    use_col = (lane % HEAD_DIM) >= half
    pos = np.where(use_col[None, :], cols[:, None], rows[:, None]).astype(np.float64)
    ang = pos * inv[f][None, :]
    lo = (lane % half) < (half // 2)
    cos = np.cos(ang)
    sin = np.where(lo[None, :], -np.sin(ang), np.sin(ang))
    reps = (MIX_ROWS // L, 1)
    return jnp.asarray(np.tile(cos, reps), F32), jnp.asarray(np.tile(sin, reps), F32)


def _rope(x, cos, sin_signed, lo_mask):
    quarter = HEAD_DIM // 4
    swapped = jnp.where(lo_mask, pltpu.roll(x, LANES - quarter, axis=1), pltpu.roll(x, quarter, axis=1))
    return x * cos + swapped * sin_signed


def _ret_kernel(*refs, L, seq, layer, use_rope, has_state, want_state, carried):
    it = iter(refs)
    lg_ref = next(it)
    q_ref, k_ref, v_ref, g_ref = next(it), next(it), next(it), next(it)
    if use_rope:
        cos_ref, sin_ref = next(it), next(it)
    if has_state:
        s0_ref = next(it)
    if carried:
        next(it)
    o_ref = next(it)
    if want_state:
        st_view = _state_out_view(next(it), layer, carried)
    oacc = next(it)

    p = pl.program_id(0)
    T = RET_ROWS
    nc = L // T
    ncs = seq // T
    lane = lax.broadcasted_iota(jnp.int32, (1, LANES), 1)
    h0 = lane < HEAD_DIM
    bd = _bd_ones()
    bd_mask = bd > 0

    def lg(direction, head):
        return lg_ref[layer * 2 * N_HEADS + direction * N_HEADS + 2 * p + head]

    i = lax.broadcasted_iota(jnp.int32, (T, T), 0)
    j = lax.broadcasted_iota(jnp.int32, (T, T), 1)
    rel = (i - j).astype(F32)
    masks = []
    for hh in range(2):
        arg = jnp.where(rel > 0, rel * lg(0, hh), -rel * lg(1, hh))
        masks.append(jnp.exp(arg) + jnp.where(rel == 0, 1.0, 0.0))
    kscale = HEAD_DIM ** -0.5
    mcat = (jnp.concatenate(masks, axis=0) * kscale).astype(BF16)

    lgf_vec = jnp.where(h0, lg(0, 0), lg(0, 1))
    lgb_vec = jnp.where(h0, lg(1, 0), lg(1, 1))
    pos = lax.broadcasted_iota(jnp.int32, (T, LANES), 0).astype(F32)
    wf = jnp.exp((pos + 1.0) * lgf_vec)
    wb = jnp.exp((float(T) - pos) * lgb_vec)
    zf = jnp.exp((float(T) - 1.0 - pos) * lgf_vec) * kscale
    zb = jnp.exp(pos * lgb_vec) * kscale
    gf_chunk = jnp.exp(float(T) * lgf_vec)
    gb_chunk = jnp.exp(float(T) * lgb_vec)
    use_states = has_state or ncs > 1
    if use_rope:
        lo_mask = (lane % (HEAD_DIM // 2)) < (HEAD_DIM // 4)

    chunk_rows = [slice(c * T, (c + 1) * T) for c in range(nc)]
    qk = []
    for rows in chunk_rows:
        qc = q_ref[rows, :]
        kc = k_ref[rows, :]
        if use_rope:
            qc = _rope(qc, cos_ref[rows, :], sin_ref[rows, :], lo_mask)
            kc = _rope(kc, cos_ref[rows, :], sin_ref[rows, :], lo_mask)
        qk.append((qc, kc))
    scores = []
    for qc, kc in qk:
        qs = jnp.concatenate([jnp.where(h0, qc, 0.0), jnp.where(h0, 0.0, qc)], axis=0).astype(BF16)
        scores.append(_dot_nt(qs, kc.astype(BF16)))
    dstf, dstb, xis = [], [], []
    if use_states or want_state:
        for rows, (qc, kc) in zip(chunk_rows, qk):
            vt = v_ref[rows, :].T.astype(BF16)
            dstf.append(jnp.where(bd_mask, _dot(vt, (kc * zf).astype(BF16)), 0.0))
            dstb.append(jnp.where(bd_mask, _dot(vt, (kc * zb).astype(BF16)), 0.0))
    if use_states:
        for qc, kc in qk:
            xis.append(jnp.concatenate([qc * wf, qc * wb], axis=1).astype(BF16))
    probs = [s.astype(BF16) * mcat for s in scores]
    outs = [_dot(pm, v_ref[rows, :].astype(BF16)) for pm, rows in zip(probs, chunk_rows)]
    for rows, o2 in zip(chunk_rows, outs):
        oacc[rows, :] = jnp.where(h0, o2[:T], o2[T:])

    stf_in, stb_in = [None] * nc, [None] * nc
    if use_states or want_state:
        for sq in range(nc // ncs):
            chunks = range(sq * ncs, (sq + 1) * ncs)
            if has_state:
                stf, stb = _pair_state_in(s0_ref.at[sq], 0), _pair_state_in(s0_ref.at[sq], 1)
            else:
                stf = jnp.zeros((LANES, LANES), F32)
                stb = stf
            for c in chunks:
                stf_in[c] = stf
                stf = stf * gf_chunk + dstf[c]
            for c in reversed(chunks):
                stb_in[c] = stb
                stb = stb * gb_chunk + dstb[c]
            if want_state:
                _pair_state_out(stf, st_view(sq), 0)
                _pair_state_out(stb, st_view(sq), 1)

    outs = [oacc[rows, :] for rows in chunk_rows]
    if use_states:
        inter = [_dot_nt(xis[c], jnp.concatenate([stf_in[c], stb_in[c]], axis=1).astype(BF16))
                 for c in range(nc)]
        outs = [o + x for o, x in zip(outs, inter)]
    means = [_head_sums(o, bd) * (1.0 / HEAD_DIM) for o in outs]
    cent = [o - mu for o, mu in zip(outs, means)]
    variances = [_head_sums_sq(oc, bd) * (1.0 / HEAD_DIM) for oc in cent]
    for rows, oc, var in zip(chunk_rows, cent, variances):
        y = oc * lax.rsqrt(var + LN_EPS)
        o_ref[rows, :] = (y * _silu(g_ref[rows, :])).astype(BF16)


def _mixer_specs(proj, cols, seq, layer, s0, want_state, carried):
    rows = proj.shape[0]
    blk = lambda col: pl.BlockSpec((MIX_ROWS, LANES), lambda p, b, col=col: (b, col + p))
    in_specs = [blk(col) for col in cols]
    state_spec = None
    if s0 is not None:
        state_spec = pl.BlockSpec((MIX_ROWS // seq, None, 2, 2, HEAD_DIM, HEAD_DIM),
                                  lambda p, b: (b, layer, 0, p, 0, 0))
    out_specs = [pl.BlockSpec((MIX_ROWS, LANES), lambda p, b: (b, p))]
    out_shape = [jax.ShapeDtypeStruct((rows, HEADS_WIDTH), BF16)]
    if want_state:
        if carried is None:
            out_specs.append(pl.BlockSpec((MIX_ROWS // seq, DEPTH, 2, 2, HEAD_DIM, HEAD_DIM),
                                          lambda p, b: (b, 0, 0, p, 0, 0)))
        else:
            out_specs.append(pl.BlockSpec((MIX_ROWS // seq, None, 2, 2, HEAD_DIM, HEAD_DIM),
                                          lambda p, b: (b, layer, 0, p, 0, 0)))
        out_shape.append(jax.ShapeDtypeStruct((rows // seq, DEPTH, 2, N_HEADS, HEAD_DIM, HEAD_DIM), F32))
    return in_specs, state_spec, out_specs, out_shape


def _carry_states(in_specs, args, carried):
    if carried is None:
        return {}
    in_specs.append(pl.BlockSpec(memory_space=pl.ANY))
    args.append(carried)
    return {len(args) - 1: 1}


def _retention(proj, lg_flat, seq, layer, rope, s0, want_state, carried=None):
    rows = proj.shape[0]
    in_specs, state_spec, out_specs, out_shape = _mixer_specs(
        proj, (COL_RQ, COL_RK, COL_RV, COL_RG), seq, layer, s0, want_state, carried)
    in_specs = [pl.BlockSpec(memory_space=pltpu.SMEM)] + in_specs
    args = [lg_flat, proj, proj, proj, proj]
    if rope is not None:
        in_specs += [pl.BlockSpec((MIX_ROWS, LANES), lambda p, b: (0, 0))] * 2
        args += list(rope)
    if s0 is not None:
        in_specs.append(state_spec)
        args.append(s0)
    aliases = _carry_states(in_specs, args, carried)
    kern = functools.partial(_ret_kernel, L=MIX_ROWS, seq=seq, layer=layer, use_rope=rope is not None,
                             has_state=s0 is not None, want_state=want_state, carried=carried is not None)
    return pl.pallas_call(
        kern,
        grid=(N_PAIRS, rows // MIX_ROWS),
        in_specs=in_specs,
        out_specs=out_specs,
        out_shape=out_shape,
        input_output_aliases=aliases,
        scratch_shapes=[pltpu.VMEM((MIX_ROWS, LANES), F32)],
        compiler_params=pltpu.CompilerParams(
            dimension_semantics=("arbitrary", "arbitrary"), vmem_limit_bytes=VMEM_LIMIT),
        name="retention",
    )(*args)


def _hgrn_gate(x, llb, l1mlb, omlb, zero_bound):
    z = jnp.exp(-jnp.abs(x))
    zp1 = 1.0 + z
    ls = jnp.minimum(x, 0.0) - jnp.log(zp1)
    sig_neg = jnp.where(x > 0.0, z, 1.0) / zp1
    if zero_bound:
        return ls, sig_neg
    t = l1mlb + ls
    logf = jnp.maximum(llb, t) + jnp.log(1.0 + jnp.exp(-jnp.abs(llb - t)))
    return logf, omlb * sig_neg


def _hgrn_kernel(*refs, L, seq, layer, has_state, want_state, carried):
    it = iter(refs)
    gp_ref = next(it)
    hq_ref, ff_ref, fb_ref, hi_ref, hg_ref = next(it), next(it), next(it), next(it), next(it)
    if has_state:
        s0_ref = next(it)
    if carried:
        next(it)
    o_ref = next(it)
    if want_state:
        st_view = _state_out_view(next(it), layer, carried)
    oacc, xi_ref, dstf_ref, dstb_ref, dec_ref, stcat_ref, lvl_ref = (next(it) for _ in range(7))
    q_s, kf_s, kb_s, bf_s, pb_s = (next(it) for _ in range(5))
    kfs_s, kbs_s = next(it), next(it)

    C = HG_CHUNK
    nc = L // C
    ncs = seq // C
    lane = lax.broadcasted_iota(jnp.int32, (1, LANES), 1)
    h0 = lane < HEAD_DIM
    bd = _bd_ones()
    bd_mask = bd > 0
    rowi = lax.broadcasted_iota(jnp.int32, (C, C), 0)
    coli = lax.broadcasted_iota(jnp.int32, (C, C), 1)
    tri = jnp.where(rowi >= coli, 1.0, 0.0).astype(BF16)
    sub = lax.broadcasted_iota(jnp.int32, (C // SUBLANES, SUBLANES, LANES), 1)
    row2 = lax.broadcasted_iota(jnp.int32, (2 * C, C), 0) % C
    col2 = lax.broadcasted_iota(jnp.int32, (2 * C, C), 1)
    lvl_ref[...] = 31 - lax.clz(row2 ^ col2)

    llb_f, llb_b = gp_ref[0:1, :], gp_ref[1:2, :]
    l1m_f, l1m_b = gp_ref[2:3, :], gp_ref[3:4, :]
    oml_f, oml_b = gp_ref[4:5, :], gp_ref[5:6, :]

    def to3(x):
        return x.reshape(C // SUBLANES, SUBLANES, LANES)

    def prep_gates(c):
        rows = pl.ds(c * C, C)
        q = _silu(hq_ref[rows, :])
        lff, kf = _hgrn_gate(ff_ref[rows, :], llb_f, l1m_f, oml_f, layer == 0)
        lfb, kb = _hgrn_gate(fb_ref[rows, :], llb_b, l1m_b, oml_b, layer == 0)
        lf2 = jnp.concatenate([lff, lfb], axis=1) * LOG2E
        return q, kf, kb, lf2, _split2(lf2)

    def prep_cumsum(vals):
        hi, lo = vals[4]
        return _dot(tri, hi) + _dot(tri, lo)

    def prep_decay(c, gmax, vals, cs):
        rows = pl.ds(c * C, C)
        q, kf, kb, lf2, _ = vals
        bf = cs[:, :LANES]
        bb_incl = cs[:, LANES:]
        pb = bb_incl - lf2[:, LANES:]
        bf_last = bf[C - 1:C, :]
        tot_b = bb_incl[C - 1:C, :]
        q_s[rows, :] = q
        kf_s[rows, :] = kf
        kb_s[rows, :] = kb
        bf_s[rows, :] = bf
        pb_s[rows, :] = pb
        dec_ref[c, 0:1, :] = jnp.exp2(bf_last)
        dec_ref[c, 1:2, :] = jnp.exp2(tot_b)
        xi_ref[rows, 0:LANES] = (q * jnp.exp2(bf)).astype(BF16)
        xi_ref[rows, LANES:] = (q * jnp.exp2(tot_b - pb)).astype(BF16)
        for blk in range(C // HG_BLOCK):
            s, e = blk * HG_BLOCK, (blk + 1) * HG_BLOCK - 1
            gmax = jnp.maximum(gmax, bf[s:s + 1, :] - bf[e:e + 1, :])
            gmax = jnp.maximum(gmax, pb[s:s + 1, :] - pb[e:e + 1, :])
        kfs = (kf * jnp.exp2(bf_last - bf)).astype(BF16)
        kbs = (kb * jnp.exp2(pb)).astype(BF16)
        kfs_s[rows, :] = kfs
        kbs_s[rows, :] = kbs
        vt = hi_ref[rows, :].T.astype(BF16)
        return gmax, (vt, kfs, kbs)

    def prep_states(c, ops):
        vt, kfs, kbs = ops
        dstf_ref[c] = jnp.where(bd_mask, _dot(vt, kfs), 0.0)
        dstb_ref[c] = jnp.where(bd_mask, _dot(vt, kbs), 0.0)

    jm = (lax.broadcasted_iota(jnp.int32, (2 * HG_BLOCK, C), 1)
          - lax.broadcasted_iota(jnp.int32, (2 * HG_BLOCK, C), 0) % HG_BLOCK)

    def fast_pack(c):
        rows = pl.ds(c * C, C)
        q, kf, kb, bf, pb = q_s[rows, :], kf_s[rows, :], kb_s[rows, :], bf_s[rows, :], pb_s[rows, :]
        packed = []
        for blk in range(C // HG_BLOCK):
            s, e = blk * HG_BLOCK, (blk + 1) * HG_BLOCK
            rf = bf[e - 1:e, :]
            rb_ = pb[s:s + 1, :]
            qf = q[s:e, :] * jnp.exp2(bf[s:e, :] - rf)
            qb = q[s:e, :] * jnp.exp2(rb_ - pb[s:e, :])
            x = jnp.concatenate([jnp.where(h0, qf, 0.0), jnp.where(h0, 0.0, qf),
                                 jnp.where(h0, qb, 0.0), jnp.where(h0, 0.0, qb)], axis=0).astype(BF16)
            if e == C:
                kfv = kfs_s[rows, :]
            else:
                kfv = (kf[:e, :] * jnp.exp2(rf - bf[:e, :])).astype(BF16)
            if s == 0:
                kbv = kbs_s[rows, :]
            else:
                kbv = (kb[s:, :] * jnp.exp2(pb[s:, :] - rb_)).astype(BF16)
            parts = [kfv]
            if e < C:
                parts.append(jnp.zeros((C - e, LANES), BF16))
            if s > 0:
                parts.append(jnp.zeros((s, LANES), BF16))
            parts.append(kbv)
            packed.append((x, jnp.concatenate(parts, axis=0)))
        return packed

    def fast_scores(packed):
        return [_dot_nt(x, w) for x, w in packed]

    def fast_mask(scores):
        a_rows = []
        for blk, sc in enumerate(scores):
            s = blk * HG_BLOCK
            af = sc[:2 * HG_BLOCK, :C]
            ab = sc[2 * HG_BLOCK:, C:]
            a_rows.append(jnp.where(jm < s, af, jnp.where(jm > s, ab, af + ab)))
        return jnp.concatenate(a_rows, axis=0).astype(BF16)

    def fast_store(c, o2):
        o_blocks = []
        for blk in range(C // HG_BLOCK):
            base = blk * 2 * HG_BLOCK
            o_blocks.append(jnp.where(h0, o2[base:base + HG_BLOCK], o2[base + HG_BLOCK:base + 2 * HG_BLOCK]))
        oacc[pl.ds(c * C, C), :] = jnp.concatenate(o_blocks, axis=0)

    def exact_body(c, carry):
        rows = pl.ds(pl.multiple_of(c * C, C), C)
        q, kf, kb, bf, pb = q_s[rows, :], kf_s[rows, :], kb_s[rows, :], bf_s[rows, :], pb_s[rows, :]
        v = hi_ref[rows, :]
        vb = v.astype(BF16)

        d0 = (q * (kf + kb)).astype(BF16)
        o = _dot(d0, bd) * v
        q3, v3, bf3, pb3, kf3, kb3 = to3(q), to3(v), to3(bf), to3(pb), to3(kf), to3(kb)
        nbf3, npb3 = -bf3, -pb3
        for dl in range(1, SUBLANES):
            src_fw = sub < (SUBLANES - dl)
            u = jnp.where(src_fw, nbf3, pb3)
            kmix = jnp.where(src_fw, kf3, kb3)
            w = jnp.where(sub >= dl, bf3, npb3)
            rel = w + pltpu.roll(u, dl, axis=1)
            d = (q3 * pltpu.roll(kmix, dl, axis=1) * jnp.exp2(rel)).reshape(C, LANES).astype(BF16)
            o = o + _dot(d, bd) * pltpu.roll(v3, dl, axis=1).reshape(C, LANES)

        lvl = lvl_ref[...]
        a_tot = jnp.zeros((2 * C, C), F32)
        m = SUBLANES
        while m < C:
            nblk = C // (2 * m)

            def halves(x):
                x4 = x.reshape(nblk, 2, m, LANES)
                return x4[:, 0], x4[:, 1]

            def join(lo, up):
                return jnp.stack([lo, up], axis=1).reshape(C, LANES)

            q_lo, q_up = halves(q)
            kf_lo, _ = halves(kf)
            _, kb_up = halves(kb)
            bf_lo, bf_up = halves(bf)
            pb_lo, pb_up = halves(pb)
            rf = bf_lo[:, m - 1:m, :]
            rb_ = pb_up[:, 0:1, :]
            xq = join(q_lo * jnp.exp2(rb_ - pb_lo), q_up * jnp.exp2(bf_up - rf))
            yk = join(kf_lo * jnp.exp2(rf - bf_lo), kb_up * jnp.exp2(pb_up - rb_))
            xs = jnp.concatenate([jnp.where(h0, xq, 0.0), jnp.where(h0, 0.0, xq)], axis=0).astype(BF16)
            a = _dot_nt(xs, yk.astype(BF16))
            a_tot = jnp.where(lvl == (m.bit_length() - 1), a, a_tot)
            m *= 2
        o2 = _dot(a_tot.astype(BF16), vb)
        o = o + jnp.where(h0, o2[:C], o2[C:])
        oacc[rows, :] = o
        return carry

    gmax = jnp.zeros((1, LANES), F32)
    for g0 in range(0, nc, HG_GROUP):
        grp = range(g0, g0 + HG_GROUP)
        vals = [prep_gates(c) for c in grp]
        sums = [prep_cumsum(vv) for vv in vals]
        ops = []
        for c, vv, cs in zip(grp, vals, sums):
            gmax, op = prep_decay(c, gmax, vv, cs)
            ops.append(op)
        for c, op in zip(grp, ops):
            prep_states(c, op)
    for g0 in range(0, nc, HG_GROUP):
        grp = range(g0, g0 + HG_GROUP)
        packed = [fast_pack(c) for c in grp]
        scores = [fast_scores(pk) for pk in packed]
        amats = [fast_mask(sc) for sc in scores]
        outs = [_dot(a, hi_ref[pl.ds(c * C, C), :].astype(BF16)) for a, c in zip(amats, grp)]
        for c, o2 in zip(grp, outs):
            fast_store(c, o2)

    for sq in range(nc // ncs):
        chunks = range(sq * ncs, (sq + 1) * ncs)
        if has_state:
            stf, stb = _pair_state_in(s0_ref.at[sq], 0), _pair_state_in(s0_ref.at[sq], 1)
        else:
            stf = jnp.zeros((LANES, LANES), F32)
            stb = stf
        for c in chunks:
            stcat_ref[c, :, 0:LANES] = stf.astype(BF16)
            stf = stf * dec_ref[c, 0:1, :] + dstf_ref[c]
        for c in reversed(chunks):
            stcat_ref[c, :, LANES:] = stb.astype(BF16)
            stb = stb * dec_ref[c, 1:2, :] + dstb_ref[c]
        if want_state:
            _pair_state_out(stf, st_view(sq), 0)
            _pair_state_out(stb, st_view(sq), 1)

    def finish():
        for g0 in range(0, nc, HG_GROUP):
            grp = range(g0, g0 + HG_GROUP)
            rows = [slice(c * C, (c + 1) * C) for c in grp]
            inter = [_dot_nt(xi_ref[r, :], stcat_ref[c]) for c, r in zip(grp, rows)]
            outs = [oacc[r, :] + x for r, x in zip(rows, inter)]
            sums = [_head_sums_sq(o, bd) for o in outs]
            for r, o, ms in zip(rows, outs, sums):
                y = o * lax.rsqrt(ms * (1.0 / HEAD_DIM) + LN_EPS)
                o_ref[r, :] = (y * _silu(hg_ref[r, :])).astype(BF16)

    finish()

    @pl.when(jnp.logical_not(jnp.max(gmax) < HG_SAFE_LOG2))
    def _():
        lax.fori_loop(0, nc, exact_body, 0)
        finish()


def _hgrn(proj, gate_params, seq, layer, s0, want_state, carried=None):
    rows = proj.shape[0]
    L = MIX_ROWS
    nc = L // HG_CHUNK
    in_specs, state_spec, out_specs, out_shape = _mixer_specs(
        proj, (COL_HQ, COL_HFF, COL_HFB, COL_HI, COL_HG), seq, layer, s0, want_state, carried)
    in_specs = [pl.BlockSpec((None, None, SUBLANES, LANES), lambda p, b: (layer, p, 0, 0))] + in_specs
    args = [gate_params, proj, proj, proj, proj, proj]
    if s0 is not None:
        in_specs.append(state_spec)
        args.append(s0)
    aliases = _carry_states(in_specs, args, carried)
    kern = functools.partial(_hgrn_kernel, L=L, seq=seq, layer=layer, has_state=s0 is not None,
                             want_state=want_state, carried=carried is not None)
    return pl.pallas_call(
        kern,
        grid=(N_PAIRS, rows // L),
        in_specs=in_specs,
        out_specs=out_specs,
        out_shape=out_shape,
        input_output_aliases=aliases,
        scratch_shapes=[pltpu.VMEM((L, LANES), F32),
                        pltpu.VMEM((L, 2 * LANES), BF16),
                        pltpu.VMEM((nc, LANES, LANES), F32),
                        pltpu.VMEM((nc, LANES, LANES), F32),
                        pltpu.VMEM((nc, SUBLANES, LANES), F32),
                        pltpu.VMEM((nc, LANES, 2 * LANES), BF16),
                        pltpu.VMEM((2 * HG_CHUNK, HG_CHUNK), jnp.int32)]
                       + [pltpu.VMEM((L, LANES), F32)] * 5
                       + [pltpu.VMEM((L, LANES), BF16)] * 2,
        compiler_params=pltpu.CompilerParams(
            dimension_semantics=("arbitrary", "arbitrary"), vmem_limit_bytes=VMEM_LIMIT),
        name="hgrn2",
    )(*args)


def _ffn_kernel(x_ref, of_ref, or_ref, oh_ref, mod_ref, wo_ref, lng_ref, lnb_ref,
                wg_ref, wu_ref, wd_ref, *rest, n_side, mod_row, next_mod_layer):
    if next_mod_layer is not None:
        _mod_block(*rest[:3], rest[-1], next_mod_layer)
        rest = rest[3:-1]
    side_in, y_ref, side_out = rest[:n_side], rest[n_side], rest[n_side + 1:]
    g1, sh2, sc2, g2 = _mod_vectors(mod_ref, mod_row)[2:6]
    half = ROW_TILE // 2
    halves = (slice(0, half), slice(half, ROW_TILE))
    mix = [_dot(jnp.concatenate([of_ref[r, :], or_ref[r, :], oh_ref[r, :]], axis=1), wo_ref[...])
           for r in halves]
    x1, gate, up, ffn = [], [], [], []
    for r, m in zip(halves, mix):
        x1.append(_ln(ALPHA * x_ref[r, :] + g1 * m) * lng_ref[0:1, :] + lnb_ref[0:1, :])
        h2 = (_ln(x1[-1]) * (1.0 + sc2) + sh2).astype(BF16)
        gate.append(_dot(h2, wg_ref[...]))
        up.append(_dot(h2, wu_ref[...]))
    for g, u in zip(gate, up):
        act = (_silu(g) * u).astype(BF16)
        ffn.append(_dot(act, wd_ref[...]))
    for r, x, f in zip(halves, x1, ffn):
        y_ref[r, :] = _ln(ALPHA * x + g2 * f) * lng_ref[1:2, :] + lnb_ref[1:2, :]
    _side_casts(side_in + side_out)


def _out_ffn(x2d, o_f, o_r, o_h, mod, w_out_bf, ln_g, ln_b, wg_bf, wu_bf, wd_bf, layer, mod_row, side=(),
             next_mod=None):
    rows = x2d.shape[0]
    nsteps = rows // ROW_TILE
    s_in, s_out, s_shape, s_args = _side_cast_specs(side, nsteps)
    if next_mod is not None:
        width = N_MOD * D_MODEL
        m_in, m_out = _mod_specs(layer + 1, width // nsteps, lambda i: i)
        s_in, s_args = m_in + s_in, list(next_mod) + s_args
        s_out = s_out + [m_out]
        s_shape = s_shape + [jax.ShapeDtypeStruct((MOD_ROWS, width), F32)]
    once = pl.Buffered(1)
    row_blk = lambda w: pl.BlockSpec((ROW_TILE, w), lambda i: (i, 0))
    return pl.pallas_call(
        functools.partial(_ffn_kernel, n_side=len(side), mod_row=mod_row,
                          next_mod_layer=None if next_mod is None else layer + 1),
        grid=(nsteps,),
        in_specs=[row_blk(D_MODEL), row_blk(FNET_WIDTH), row_blk(HEADS_WIDTH), row_blk(HEADS_WIDTH),
                  pl.BlockSpec((MOD_ROWS, N_MOD * D_MODEL), lambda i: (0, 0)),
                  pl.BlockSpec((D_MODEL, D_MODEL), lambda i: (0, 0), pipeline_mode=once),
                  pl.BlockSpec((None, 2, D_MODEL), lambda i: (layer, 0, 0)),
                  pl.BlockSpec((None, 2, D_MODEL), lambda i: (layer, 0, 0)),
                  pl.BlockSpec((D_MODEL, D_FF), lambda i: (0, 0), pipeline_mode=once),
                  pl.BlockSpec((D_MODEL, D_FF), lambda i: (0, 0), pipeline_mode=once),
                  pl.BlockSpec((D_FF, D_MODEL), lambda i: (0, 0), pipeline_mode=once)] + s_in,
        out_specs=[row_blk(D_MODEL)] + s_out,
        out_shape=[jax.ShapeDtypeStruct((rows, D_MODEL), F32)] + s_shape,
        compiler_params=pltpu.CompilerParams(
            dimension_semantics=("arbitrary",), vmem_limit_bytes=VMEM_LIMIT),
        name="out_ffn",
    )(x2d, o_f, o_r, o_h, mod, w_out_bf, ln_g, ln_b, wg_bf, wu_bf, wd_bf, *s_args)


def kernel(x_prompt, x_sample, c, state_ret, state_hgrn, c_ctx, w_mod, b_mod, w_in, w_out,
           ret_log_decay, hg_lower_bound, ln_g, ln_b, w_gate, w_up, w_down):
    B, S, _ = x_prompt.shape
    DB, DS, _ = x_sample.shape

    p = jax.nn.softmax(hg_lower_bound.astype(F32), axis=1)
    cum = jnp.cumsum(p, axis=1)
    lbs = cum - cum[:, :1]
    kinds = jnp.stack([jnp.log(lbs), jnp.log1p(-lbs), 1.0 - lbs], axis=0)
    gp = jnp.transpose(kinds.reshape(3, 2, DEPTH, N_PAIRS, LANES), (2, 3, 0, 1, 4))
    gp = gp.reshape(DEPTH, N_PAIRS, 6, LANES)
    gp = jnp.concatenate([gp, jnp.zeros((DEPTH, N_PAIRS, SUBLANES - 6, LANES), F32)], axis=2)
    lg_flat = (-jnp.exp(ret_log_decay.astype(F32))).reshape(-1)

    cv =jnp.concatenate([c_ctx[None, :], c, jnp.zeros((MOD_ROWS - 1 - DB, D_MODEL), F32)], axis=0)
    mod = _modulation(cv, w_mod, b_mod, 0)

    rope = _rope_tables(DS)

    ctx_row = lambda i: 0
    smp_row = lambda i: 1 + i // (DS // ROW_TILE)

    y = x_prompt.reshape(B * S, D_MODEL)
    z = x_sample.reshape(DB * DS, D_MODEL)
    new_state_ret, new_state_hgrn = None, None
    w_in_bf = w_in[0].astype(BF16)
    w_out_bf = wg_bf = wu_bf = wd_bf = None
    for l in range(DEPTH):
        side_c = [(w_out, l), (w_gate, l)] if l == 0 else []
        side_s = [(w_up, l), (w_down, l)] if l == 0 else []
        proj_c, *cast_c = _inproj(y, mod, w_in_bf, l, ctx_row, side_c)
        proj_s, *cast_s = _inproj(z, mod, w_in_bf, l, smp_row, side_s)
        if l == 0:
            (w_out_bf, wg_bf), (wu_bf, wd_bf) = cast_c, cast_s
        of_c = _fnet(proj_c, S)
        or_c, new_state_ret = _retention(proj_c, lg_flat, S, l, None, None, True, new_state_ret)
        oh_c, new_state_hgrn = _hgrn(proj_c, gp, S, l, None, True, new_state_hgrn)
        of_s = _fnet(proj_s, DS)
        (or_s,) = _retention(proj_s, lg_flat, DS, l, rope, state_ret, False)
        (oh_s,) = _hgrn(proj_s, gp, DS, l, state_hgrn, False)
        last = l + 1 == DEPTH
        side_c = [] if last else [(w_in, l + 1), (w_out, l + 1)]
        side_s = [] if last else [(w_gate, l + 1), (w_up, l + 1), (w_down, l + 1)]
        weights = (w_out_bf, ln_g, ln_b, wg_bf, wu_bf, wd_bf)
        next_mod = None if last else (cv, w_mod, b_mod)
        y, *cast_c = _out_ffn(y, of_c, or_c, oh_c, mod, *weights, l, ctx_row, side_c, next_mod)
        z, *cast_s = _out_ffn(z, of_s, or_s, oh_s, mod, *weights, l, smp_row, side_s)
        if not last:
            (w_in_bf, w_out_bf, mod), (wg_bf, wu_bf, wd_bf) = cast_c, cast_s

    return (y.reshape(B, S, D_MODEL), z.reshape(DB, DS, D_MODEL), new_state_ret, new_state_hgrn)
```

```python
import functools

import numpy as np
import jax
import jax.numpy as jnp
from jax import lax
from jax.experimental import pallas as pl
from jax.experimental.pallas import tpu as pltpu

F32 = jnp.float32
BF16 = jnp.bfloat16

D_MODEL = 1024
DEPTH = 2
GRID_W = 64
FNET_WIDTH = 256
FNET_GDIM = 64
N_HEADS = 6
N_PAIRS = N_HEADS // 2
HEAD_DIM = 64
LANES = 128
SUBLANES = 8
HEADS_WIDTH = N_HEADS * HEAD_DIM
IN_WIDTH = 3712
D_FF = 2816
ROPE_BASE = 10000.0
LN_EPS = 1e-5
LOG2E = 1.4426950408889634
N_MOD = 6
ALPHA = (2 * DEPTH) ** 0.25
MOD_ROWS = 8
ROW_TILE = 512
RET_ROWS = 256
MIX_ROWS = 2048
HG_CHUNK = 128
HG_GROUP = 16
HG_BLOCK = 32
HG_SAFE_LOG2 = 96.0
VMEM_LIMIT = 56 * 1024 * 1024

COL_RQ, COL_RK, COL_RV, COL_RG = 2, 5, 8, 11
COL_HQ, COL_HI, COL_HG = 14, 17, 20
GATE_LO, GATE_HI = 2176, 2944
GCOL_FF, GCOL_FB = 0, 3


def _silu(x):
    return x / (1.0 + jnp.exp(-x))


def _ln(x):
    mu = jnp.mean(x, axis=-1, keepdims=True)
    xc = x - mu
    var = jnp.mean(xc * xc, axis=-1, keepdims=True)
    return xc * lax.rsqrt(var + LN_EPS)


def _dot(a, b):
    return jnp.dot(a, b, preferred_element_type=F32)


def _dot_nt(a, b):
    return lax.dot_general(a, b, (((1,), (1,)), ((), ())), preferred_element_type=F32)


def _split2(x):
    hi = x.astype(BF16)
    lo = (x - hi.astype(F32)).astype(BF16)
    return hi, lo


def _head_sums(x, bd_ones):
    return _dot(x.astype(BF16), bd_ones)


def _head_sums_sq(x, bd_ones):
    return _head_sums(x * x, bd_ones)


def _bd_ones():
    r = lax.broadcasted_iota(jnp.int32, (LANES, LANES), 0)
    c = lax.broadcasted_iota(jnp.int32, (LANES, LANES), 1)
    return jnp.where((r < HEAD_DIM) == (c < HEAD_DIM), 1.0, 0.0).astype(BF16)


def _pair_state_in(s_ref, direction):
    z = jnp.zeros((HEAD_DIM, HEAD_DIM), F32)
    top = jnp.concatenate([s_ref[direction, 0], z], axis=1)
    bot = pltpu.roll(jnp.concatenate([s_ref[direction, 1], z], axis=1), HEAD_DIM, axis=1)
    return jnp.concatenate([top, bot], axis=0).T


def _state_out_view(st_ref, layer, carried):
    if carried:
        return lambda sq: st_ref.at[sq]
    for other in range(DEPTH):
        if other != layer:
            st_ref[:, other] = jnp.zeros(st_ref.shape[:1] + st_ref.shape[2:], F32)
    return lambda sq: st_ref.at[sq, layer]


def _pair_state_out(st, st_ref, direction):
    nat = st.T
    st_ref[direction, 0] = nat[:HEAD_DIM, :HEAD_DIM]
    st_ref[direction, 1] = pltpu.roll(nat[HEAD_DIM:, :], HEAD_DIM, axis=1)[:, :HEAD_DIM]


def _mod_block(cv_ref, w_ref, b_ref, o_ref, layer):
    s = _silu(cv_ref[...]).astype(BF16)
    o_ref[...] = _dot(s, w_ref[...].astype(BF16)) + b_ref[layer:layer + 1, :]


def _mod_specs(layer, tn, col):
    return ([pl.BlockSpec((MOD_ROWS, D_MODEL), lambda *g: (0, 0)),
             pl.BlockSpec((None, D_MODEL, tn), lambda *g: (layer, 0, col(*g))),
             pl.BlockSpec((DEPTH, tn), lambda *g: (0, col(*g)))],
            pl.BlockSpec((MOD_ROWS, tn), lambda *g: (0, col(*g))))


def _modulation(cv, w_mod, b_mod, layer):
    tn = 3072
    width = N_MOD * D_MODEL
    in_specs, out_spec = _mod_specs(layer, tn, lambda j: j)
    return pl.pallas_call(
        functools.partial(_mod_block, layer=layer),
        grid=(width // tn,),
        in_specs=in_specs,
        out_specs=out_spec,
        out_shape=jax.ShapeDtypeStruct((MOD_ROWS, width), F32),
        compiler_params=pltpu.CompilerParams(
            dimension_semantics=("arbitrary",), vmem_limit_bytes=VMEM_LIMIT),
        name="modulation",
    )(cv, w_mod, b_mod)


def _side_cast_specs(side, nsteps):
    in_specs, out_specs, out_shape, args = [], [], [], []
    for w, layer in side:
        _, r, c = w.shape
        in_specs.append(pl.BlockSpec((None, r // nsteps, c), lambda i, layer=layer: (layer, i, 0)))
        out_specs.append(pl.BlockSpec((r // nsteps, c), lambda i: (i, 0)))
        out_shape.append(jax.ShapeDtypeStruct((r, c), BF16))
        args.append(w)
    return in_specs, out_specs, out_shape, args


def _side_casts(side_refs):
    n = len(side_refs) // 2
    for src, dst in zip(side_refs[:n], side_refs[n:]):
        dst[...] = src[...].astype(BF16)


def _mod_vectors(mod_ref, mod_row):
    m = mod_ref[pl.ds(mod_row(pl.program_id(0)), 1), :]
    return [m[:, k * D_MODEL:(k + 1) * D_MODEL] for k in range(N_MOD)]


def _inproj_kernel(x_ref, mod_ref, w_ref, *rest, n_side, mod_row):
    side_in, (o_ref, g_ref), side_out = rest[:n_side], rest[n_side:n_side + 2], rest[n_side + 2:]
    sh, sc = _mod_vectors(mod_ref, mod_row)[0:2]
    half = ROW_TILE // 2
    for r in (slice(0, half), slice(half, ROW_TILE)):
        h = _ln(x_ref[r, :]) * (1.0 + sc) + sh
        o = _dot(h.astype(BF16), w_ref[...])
        o_ref[r, 0:GATE_LO] = o[:, 0:GATE_LO].astype(BF16)
        o_ref[r, GATE_LO:] = o[:, GATE_HI:].astype(BF16)
        g_ref[r, :] = o[:, GATE_LO:GATE_HI]
    _side_casts(side_in + side_out)


def _inproj(x2d, mod, w_in_bf, layer, mod_row, side=()):
    rows = x2d.shape[0]
    nsteps = rows // ROW_TILE
    s_in, s_out, s_shape, s_args = _side_cast_specs(side, nsteps)
    return pl.pallas_call(
        functools.partial(_inproj_kernel, n_side=len(side), mod_row=mod_row),
        grid=(nsteps,),
        in_specs=[pl.BlockSpec((ROW_TILE, D_MODEL), lambda i: (i, 0)),
                  pl.BlockSpec((MOD_ROWS, N_MOD * D_MODEL), lambda i: (0, 0)),
                  pl.BlockSpec((D_MODEL, IN_WIDTH), lambda i: (0, 0))] + s_in,
        out_specs=[pl.BlockSpec((ROW_TILE, IN_WIDTH - (GATE_HI - GATE_LO)), lambda i: (i, 0)),
                   pl.BlockSpec((ROW_TILE, GATE_HI - GATE_LO), lambda i: (i, 0))] + s_out,
        out_shape=[jax.ShapeDtypeStruct((rows, IN_WIDTH - (GATE_HI - GATE_LO)), BF16),
                   jax.ShapeDtypeStruct((rows, GATE_HI - GATE_LO), F32)] + s_shape,
        compiler_params=pltpu.CompilerParams(
            dimension_semantics=("arbitrary",), vmem_limit_bytes=VMEM_LIMIT),
        name="inproj",
    )(x2d, mod, w_in_bf, *s_args)


def _fnet_kernel(u_ref, w1_ref, cl_ref, sl_ref, o_ref, *, seq):
    u = u_ref[...].astype(BF16)
    t = _dot(u, w1_ref[...])
    uc = t[:, :FNET_WIDTH].astype(BF16)
    us = t[:, FNET_WIDTH:].astype(BF16)
    for s in range(u.shape[0] // seq):
        rows = slice(s * seq, (s + 1) * seq)
        o = _dot(cl_ref[...], uc[rows, :]) - _dot(sl_ref[...], us[rows, :])
        o_ref[rows, :] = o.astype(BF16)


def _fnet_consts(L):
    scale = 1.0 / np.sqrt(L * FNET_GDIM)
    k = np.arange(FNET_GDIM)
    ang = 2.0 * np.pi * ((k[:, None] * k[None, :]) % FNET_GDIM) / FNET_GDIM
    eye = np.eye(FNET_WIDTH // FNET_GDIM)
    w1 = np.concatenate([np.kron(eye, np.cos(ang)), np.kron(eye, np.sin(ang))], axis=1) * scale
    n = np.arange(L)
    angl = 2.0 * np.pi * ((n[:, None] * n[None, :]) % L) / L
    tables = (w1, np.cos(angl), np.sin(angl))
    return tuple(jnp.asarray(t, F32).astype(BF16) for t in tables)


def _fnet(proj, L):
    rows = proj.shape[0]
    w1, cl, sl = _fnet_consts(L)
    return pl.pallas_call(
        functools.partial(_fnet_kernel, seq=L),
        grid=(rows // MIX_ROWS,),
        in_specs=[pl.BlockSpec((MIX_ROWS, FNET_WIDTH), lambda b: (b, 0)),
                  pl.BlockSpec((FNET_WIDTH, 2 * FNET_WIDTH), lambda b: (0, 0)),
                  pl.BlockSpec((L, L), lambda b: (0, 0)),
                  pl.BlockSpec((L, L), lambda b: (0, 0))],
        out_specs=pl.BlockSpec((MIX_ROWS, FNET_WIDTH), lambda b: (b, 0)),
        out_shape=jax.ShapeDtypeStruct((rows, FNET_WIDTH), BF16),
        compiler_params=pltpu.CompilerParams(
            dimension_semantics=("arbitrary",), vmem_limit_bytes=VMEM_LIMIT),
        name="fnet",
    )(proj, w1, cl, sl)


def _rope_tables(L):
    half = HEAD_DIM // 2
    inv = ROPE_BASE ** (-np.arange(0, half, 2, dtype=np.float64) / half)
    n = np.arange(L)
    rows, cols = n // GRID_W, n % GRID_W
    lane = np.arange(LANES)
    f = lane % (half // 2)
    use_col = (lane % HEAD_DIM) >= half
    pos = np.where(use_col[None, :], cols[:, None], rows[:, None]).astype(np.float64)
    ang = pos * inv[f][None, :]
    lo = (lane % half) < (half // 2)
    cos = np.cos(ang)
    sin = np.where(lo[None, :], -np.sin(ang), np.sin(ang))
    reps = (MIX_ROWS // L, 1)
    return jnp.asarray(np.tile(cos, reps), F32), jnp.asarray(np.tile(sin, reps), F32)


def _rope(x, cos, sin_signed, lo_mask):
    quarter = HEAD_DIM // 4
    swapped = jnp.where(lo_mask, pltpu.roll(x, LANES - quarter, axis=1), pltpu.roll(x, quarter, axis=1))
    return x * cos + swapped * sin_signed


def _ret_kernel(*refs, L, seq, layer, use_rope, has_state, want_state, carried):
    it = iter(refs)
    lg_ref = next(it)
    q_ref, k_ref, v_ref, g_ref = next(it), next(it), next(it), next(it)
    if use_rope:
        cos_ref, sin_ref = next(it), next(it)
    if has_state:
        s0_ref = next(it)
    if carried:
        next(it)
    o_ref = next(it)
    if want_state:
        st_view = _state_out_view(next(it), layer, carried)
    oacc = next(it)

    p = pl.program_id(0)
    T = RET_ROWS
    nc = L // T
    ncs = seq // T
    lane = lax.broadcasted_iota(jnp.int32, (1, LANES), 1)
    h0 = lane < HEAD_DIM
    bd = _bd_ones()
    bd_mask = bd > 0

    def lg(direction, head):
        return lg_ref[layer * 2 * N_HEADS + direction * N_HEADS + 2 * p + head]

    i = lax.broadcasted_iota(jnp.int32, (T, T), 0)
    j = lax.broadcasted_iota(jnp.int32, (T, T), 1)
    rel = (i - j).astype(F32)
    masks = []
    for hh in range(2):
        arg = jnp.where(rel > 0, rel * lg(0, hh), -rel * lg(1, hh))
        masks.append(jnp.exp(arg) + jnp.where(rel == 0, 1.0, 0.0))
    kscale = HEAD_DIM ** -0.5
    mcat = (jnp.concatenate(masks, axis=0) * kscale).astype(BF16)

    lgf_vec = jnp.where(h0, lg(0, 0), lg(0, 1))
    lgb_vec = jnp.where(h0, lg(1, 0), lg(1, 1))
    pos = lax.broadcasted_iota(jnp.int32, (T, LANES), 0).astype(F32)
    wf = jnp.exp((pos + 1.0) * lgf_vec)
    wb = jnp.exp((float(T) - pos) * lgb_vec)
    zf = jnp.exp((float(T) - 1.0 - pos) * lgf_vec) * kscale
    zb = jnp.exp(pos * lgb_vec) * kscale
    gf_chunk = jnp.exp(float(T) * lgf_vec)
    gb_chunk = jnp.exp(float(T) * lgb_vec)
    use_states = has_state or ncs > 1
    if use_rope:
        lo_mask = (lane % (HEAD_DIM // 2)) < (HEAD_DIM // 4)

    chunk_rows = [slice(c * T, (c + 1) * T) for c in range(nc)]
    qk = []
    for rows in chunk_rows:
        qc = q_ref[rows, :].astype(F32)
        kc = k_ref[rows, :].astype(F32)
        if use_rope:
            qc = _rope(qc, cos_ref[rows, :], sin_ref[rows, :], lo_mask)
            kc = _rope(kc, cos_ref[rows, :], sin_ref[rows, :], lo_mask)
        qk.append((qc, kc))
    scores = []
    for qc, kc in qk:
        qs = jnp.concatenate([jnp.where(h0, qc, 0.0), jnp.where(h0, 0.0, qc)], axis=0).astype(BF16)
        scores.append(_dot_nt(qs, kc.astype(BF16)))
    dstf, dstb, xis = [], [], []
    if use_states or want_state:
        for rows, (qc, kc) in zip(chunk_rows, qk):
            vt = v_ref[rows, :].astype(F32).T.astype(BF16)
            dstf.append(jnp.where(bd_mask, _dot(vt, (kc * zf).astype(BF16)), 0.0))
            dstb.append(jnp.where(bd_mask, _dot(vt, (kc * zb).astype(BF16)), 0.0))
    if use_states:
        for qc, kc in qk:
            xis.append(jnp.concatenate([qc * wf, qc * wb], axis=1).astype(BF16))
    probs = [s.astype(BF16) * mcat for s in scores]
    outs = [_dot(pm, v_ref[rows, :].astype(BF16)) for pm, rows in zip(probs, chunk_rows)]
    for rows, o2 in zip(chunk_rows, outs):
        oacc[rows, :] = jnp.where(h0, o2[:T], o2[T:])

    stf_in, stb_in = [None] * nc, [None] * nc
    if use_states or want_state:
        for sq in range(nc // ncs):
            chunks = range(sq * ncs, (sq + 1) * ncs)
            if has_state:
                stf, stb = _pair_state_in(s0_ref.at[sq], 0), _pair_state_in(s0_ref.at[sq], 1)
            else:
                stf = jnp.zeros((LANES, LANES), F32)
                stb = stf
            for c in chunks:
                stf_in[c] = stf
                stf = stf * gf_chunk + dstf[c]
            for c in reversed(chunks):
                stb_in[c] = stb
                stb = stb * gb_chunk + dstb[c]
            if want_state:
                _pair_state_out(stf, st_view(sq), 0)
                _pair_state_out(stb, st_view(sq), 1)

    outs = [oacc[rows, :] for rows in chunk_rows]
    if use_states:
        inter = [_dot_nt(xis[c], jnp.concatenate([stf_in[c], stb_in[c]], axis=1).astype(BF16))
                 for c in range(nc)]
        outs = [o + x for o, x in zip(outs, inter)]
    means = [_head_sums(o, bd) * (1.0 / HEAD_DIM) for o in outs]
    cent = [o - mu for o, mu in zip(outs, means)]
    variances = [_head_sums_sq(oc, bd) * (1.0 / HEAD_DIM) for oc in cent]
    for rows, oc, var in zip(chunk_rows, cent, variances):
        y = oc * lax.rsqrt(var + LN_EPS)
        o_ref[rows, :] = (y * _silu(g_ref[rows, :].astype(F32))).astype(BF16)


def _mixer_specs(rows, cols, seq, layer, s0, want_state, carried):
    blk = lambda col: pl.BlockSpec((MIX_ROWS, LANES), lambda p, b, col=col: (b, col + p))
    in_specs = [blk(col) for col in cols]
    state_spec = None
    if s0 is not None:
        state_spec = pl.BlockSpec((MIX_ROWS // seq, None, 2, 2, HEAD_DIM, HEAD_DIM),
                                  lambda p, b: (b, layer, 0, p, 0, 0))
    out_specs = [pl.BlockSpec((MIX_ROWS, LANES), lambda p, b: (b, p))]
    out_shape = [jax.ShapeDtypeStruct((rows, HEADS_WIDTH), BF16)]
    if want_state:
        if carried is None:
            out_specs.append(pl.BlockSpec((MIX_ROWS // seq, DEPTH, 2, 2, HEAD_DIM, HEAD_DIM),
                                          lambda p, b: (b, 0, 0, p, 0, 0)))
        else:
            out_specs.append(pl.BlockSpec((MIX_ROWS // seq, None, 2, 2, HEAD_DIM, HEAD_DIM),
                                          lambda p, b: (b, layer, 0, p, 0, 0)))
        out_shape.append(jax.ShapeDtypeStruct((rows // seq, DEPTH, 2, N_HEADS, HEAD_DIM, HEAD_DIM), F32))
    return in_specs, state_spec, out_specs, out_shape


def _carry_states(in_specs, args, carried):
    if carried is None:
        return {}
    in_specs.append(pl.BlockSpec(memory_space=pl.ANY))
    args.append(carried)
    return {len(args) - 1: 1}


def _retention(proj, lg_flat, seq, layer, rope, s0, want_state, carried=None):
    rows = proj.shape[0]
    in_specs, state_spec, out_specs, out_shape = _mixer_specs(
        rows, (COL_RQ, COL_RK, COL_RV, COL_RG), seq, layer, s0, want_state, carried)
    in_specs = [pl.BlockSpec(memory_space=pltpu.SMEM)] + in_specs
    args = [lg_flat, proj, proj, proj, proj]
    if rope is not None:
        in_specs += [pl.BlockSpec((MIX_ROWS, LANES), lambda p, b: (0, 0))] * 2
        args += list(rope)
    if s0 is not None:
        in_specs.append(state_spec)
        args.append(s0)
    aliases = _carry_states(in_specs, args, carried)
    kern = functools.partial(_ret_kernel, L=MIX_ROWS, seq=seq, layer=layer, use_rope=rope is not None,
                             has_state=s0 is not None, want_state=want_state, carried=carried is not None)
    return pl.pallas_call(
        kern,
        grid=(N_PAIRS, rows // MIX_ROWS),
        in_specs=in_specs,
        out_specs=out_specs,
        out_shape=out_shape,
        input_output_aliases=aliases,
        scratch_shapes=[pltpu.VMEM((MIX_ROWS, LANES), F32)],
        compiler_params=pltpu.CompilerParams(
            dimension_semantics=("arbitrary", "arbitrary"), vmem_limit_bytes=VMEM_LIMIT),
        name="retention",
    )(*args)


def _hgrn_gate(x, llb, l1mlb, omlb, zero_bound):
    z = jnp.exp(-jnp.abs(x))
    zp1 = 1.0 + z
    ls = jnp.minimum(x, 0.0) - jnp.log(zp1)
    sig_neg = jnp.where(x > 0.0, z, 1.0) / zp1
    if zero_bound:
        return ls, sig_neg
    t = l1mlb + ls
    logf = jnp.maximum(llb, t) + jnp.log(1.0 + jnp.exp(-jnp.abs(llb - t)))
    return logf, omlb * sig_neg


def _hgrn_kernel(*refs, L, seq, layer, has_state, want_state, carried):
    it = iter(refs)
    gp_ref = next(it)
    hq_ref, ff_ref, fb_ref, hi_ref, hg_ref = next(it), next(it), next(it), next(it), next(it)
    if has_state:
        s0_ref = next(it)
    if carried:
        next(it)
    o_ref = next(it)
    if want_state:
        st_view = _state_out_view(next(it), layer, carried)
    oacc, xi_ref, dstf_ref, dstb_ref, dec_ref, stcat_ref, lvl_ref = (next(it) for _ in range(7))
    q_s, kf_s, kb_s, bf_s, pb_s = (next(it) for _ in range(5))
    kfs_s, kbs_s = next(it), next(it)

    C = HG_CHUNK
    nc = L // C
    ncs = seq // C
    lane = lax.broadcasted_iota(jnp.int32, (1, LANES), 1)
    h0 = lane < HEAD_DIM
    bd = _bd_ones()
    bd_mask = bd > 0
    rowi = lax.broadcasted_iota(jnp.int32, (C, C), 0)
    coli = lax.broadcasted_iota(jnp.int32, (C, C), 1)
    tri = jnp.where(rowi >= coli, 1.0, 0.0).astype(BF16)
    sub = lax.broadcasted_iota(jnp.int32, (C // SUBLANES, SUBLANES, LANES), 1)
    row2 = lax.broadcasted_iota(jnp.int32, (2 * C, C), 0) % C
    col2 = lax.broadcasted_iota(jnp.int32, (2 * C, C), 1)
    lvl_ref[...] = 31 - lax.clz(row2 ^ col2)

    llb_f, llb_b = gp_ref[0:1, :], gp_ref[1:2, :]
    l1m_f, l1m_b = gp_ref[2:3, :], gp_ref[3:4, :]
    oml_f, oml_b = gp_ref[4:5, :], gp_ref[5:6, :]

    def to3(x):
        return x.reshape(C // SUBLANES, SUBLANES, LANES)

    def prep_gates(c):
        rows = pl.ds(c * C, C)
        q = _silu(hq_ref[rows, :].astype(F32))
        lff, kf = _hgrn_gate(ff_ref[rows, :], llb_f, l1m_f, oml_f, layer == 0)
        lfb, kb = _hgrn_gate(fb_ref[rows, :], llb_b, l1m_b, oml_b, layer == 0)
        lf2 = jnp.concatenate([lff, lfb], axis=1) * LOG2E
        return q, kf, kb, lf2, _split2(lf2)

    def prep_cumsum(vals):
        hi, lo = vals[4]
        return _dot(tri, hi) + _dot(tri, lo)

    def prep_decay(c, gmax, vals, cs):
        rows = pl.ds(c * C, C)
        q, kf, kb, lf2, _ = vals
        bf = cs[:, :LANES]
        bb_incl = cs[:, LANES:]
        pb = bb_incl - lf2[:, LANES:]
        bf_last = bf[C - 1:C, :]
        tot_b = bb_incl[C - 1:C, :]
        q_s[rows, :] = q
        kf_s[rows, :] = kf
        kb_s[rows, :] = kb
        bf_s[rows, :] = bf
        pb_s[rows, :] = pb
        dec_ref[c, 0:1, :] = jnp.exp2(bf_last)
        dec_ref[c, 1:2, :] = jnp.exp2(tot_b)
        xi_ref[rows, 0:LANES] = (q * jnp.exp2(bf)).astype(BF16)
        xi_ref[rows, LANES:] = (q * jnp.exp2(tot_b - pb)).astype(BF16)
        for blk in range(C // HG_BLOCK):
            s, e = blk * HG_BLOCK, (blk + 1) * HG_BLOCK - 1
            gmax = jnp.maximum(gmax, bf[s:s + 1, :] - bf[e:e + 1, :])
            gmax = jnp.maximum(gmax, pb[s:s + 1, :] - pb[e:e + 1, :])
        kfs = (kf * jnp.exp2(bf_last - bf)).astype(BF16)
        kbs = (kb * jnp.exp2(pb)).astype(BF16)
        kfs_s[rows, :] = kfs
        kbs_s[rows, :] = kbs
        vt = hi_ref[rows, :].astype(F32).T.astype(BF16)
        return gmax, (vt, kfs, kbs)

    def prep_states(c, ops):
        vt, kfs, kbs = ops
        dstf_ref[c] = jnp.where(bd_mask, _dot(vt, kfs), 0.0)
        dstb_ref[c] = jnp.where(bd_mask, _dot(vt, kbs), 0.0)

    jm = (lax.broadcasted_iota(jnp.int32, (2 * HG_BLOCK, C), 1)
          - lax.broadcasted_iota(jnp.int32, (2 * HG_BLOCK, C), 0) % HG_BLOCK)

    def fast_pack(c):
        rows = pl.ds(c * C, C)
        q, kf, kb, bf, pb = q_s[rows, :], kf_s[rows, :], kb_s[rows, :], bf_s[rows, :], pb_s[rows, :]
        packed = []
        for blk in range(C // HG_BLOCK):
            s, e = blk * HG_BLOCK, (blk + 1) * HG_BLOCK
            rf = bf[e - 1:e, :]
            rb_ = pb[s:s + 1, :]
            qf = q[s:e, :] * jnp.exp2(bf[s:e, :] - rf)
            qb = q[s:e, :] * jnp.exp2(rb_ - pb[s:e, :])
            x = jnp.concatenate([jnp.where(h0, qf, 0.0), jnp.where(h0, 0.0, qf),
                                 jnp.where(h0, qb, 0.0), jnp.where(h0, 0.0, qb)], axis=0).astype(BF16)
            if e == C:
                kfv = kfs_s[rows, :]
            else:
                kfv = (kf[:e, :] * jnp.exp2(rf - bf[:e, :])).astype(BF16)
            if s == 0:
                kbv = kbs_s[rows, :]
            else:
                kbv = (kb[s:, :] * jnp.exp2(pb[s:, :] - rb_)).astype(BF16)
            parts = [kfv]
            if e < C:
                parts.append(jnp.zeros((C - e, LANES), BF16))
            if s > 0:
                parts.append(jnp.zeros((s, LANES), BF16))
            parts.append(kbv)
            packed.append((x, jnp.concatenate(parts, axis=0)))
        return packed

    def fast_scores(packed):
        return [_dot_nt(x, w) for x, w in packed]

    def fast_mask(scores):
        a_rows = []
        for blk, sc in enumerate(scores):
            s = blk * HG_BLOCK
            af = sc[:2 * HG_BLOCK, :C]
            ab = sc[2 * HG_BLOCK:, C:]
            a_rows.append(jnp.where(jm < s, af, jnp.where(jm > s, ab, af + ab)))
        return jnp.concatenate(a_rows, axis=0).astype(BF16)

    def fast_store(c, o2):
        o_blocks = []
        for blk in range(C // HG_BLOCK):
            base = blk * 2 * HG_BLOCK
            o_blocks.append(jnp.where(h0, o2[base:base + HG_BLOCK], o2[base + HG_BLOCK:base + 2 * HG_BLOCK]))
        oacc[pl.ds(c * C, C), :] = jnp.concatenate(o_blocks, axis=0)

    def exact_body(c, carry):
        rows = pl.ds(pl.multiple_of(c * C, C), C)
        q, kf, kb, bf, pb = q_s[rows, :], kf_s[rows, :], kb_s[rows, :], bf_s[rows, :], pb_s[rows, :]
        v = hi_ref[rows, :].astype(F32)
        vb = v.astype(BF16)

        d0 = (q * (kf + kb)).astype(BF16)
        o = _dot(d0, bd) * v
        q3, v3, bf3, pb3, kf3, kb3 = to3(q), to3(v), to3(bf), to3(pb), to3(kf), to3(kb)
        nbf3, npb3 = -bf3, -pb3
        for dl in range(1, SUBLANES):
            src_fw = sub < (SUBLANES - dl)
            u = jnp.where(src_fw, nbf3, pb3)
            kmix = jnp.where(src_fw, kf3, kb3)
            w = jnp.where(sub >= dl, bf3, npb3)
            rel = w + pltpu.roll(u, dl, axis=1)
            d = (q3 * pltpu.roll(kmix, dl, axis=1) * jnp.exp2(rel)).reshape(C, LANES).astype(BF16)
            o = o + _dot(d, bd) * pltpu.roll(v3, dl, axis=1).reshape(C, LANES)

        lvl = lvl_ref[...]
        a_tot = jnp.zeros((2 * C, C), F32)
        m = SUBLANES
        while m < C:
            nblk = C // (2 * m)

            def halves(x):
                x4 = x.reshape(nblk, 2, m, LANES)
                return x4[:, 0], x4[:, 1]

            def join(lo, up):
                return jnp.stack([lo, up], axis=1).reshape(C, LANES)

            q_lo, q_up = halves(q)
            kf_lo, _ = halves(kf)
            _, kb_up = halves(kb)
            bf_lo, bf_up = halves(bf)
            pb_lo, pb_up = halves(pb)
            rf = bf_lo[:, m - 1:m, :]
            rb_ = pb_up[:, 0:1, :]
            xq = join(q_lo * jnp.exp2(rb_ - pb_lo), q_up * jnp.exp2(bf_up - rf))
            yk = join(kf_lo * jnp.exp2(rf - bf_lo), kb_up * jnp.exp2(pb_up - rb_))
            xs = jnp.concatenate([jnp.where(h0, xq, 0.0), jnp.where(h0, 0.0, xq)], axis=0).astype(BF16)
            a = _dot_nt(xs, yk.astype(BF16))
            a_tot = jnp.where(lvl == (m.bit_length() - 1), a, a_tot)
            m *= 2
        o2 = _dot(a_tot.astype(BF16), vb)
        o = o + jnp.where(h0, o2[:C], o2[C:])
        oacc[rows, :] = o
        return carry

    gmax = jnp.zeros((1, LANES), F32)
    for g0 in range(0, nc, HG_GROUP):
        grp = range(g0, g0 + HG_GROUP)
        vals = [prep_gates(c) for c in grp]
        sums = [prep_cumsum(vv) for vv in vals]
        ops = []
        for c, vv, cs in zip(grp, vals, sums):
            gmax, op = prep_decay(c, gmax, vv, cs)
            ops.append(op)
        for c, op in zip(grp, ops):
            prep_states(c, op)
    for g0 in range(0, nc, HG_GROUP):
        grp = range(g0, g0 + HG_GROUP)
        packed = [fast_pack(c) for c in grp]
        scores = [fast_scores(pk) for pk in packed]
        amats = [fast_mask(sc) for sc in scores]
        outs = [_dot(a, hi_ref[pl.ds(c * C, C), :].astype(BF16)) for a, c in zip(amats, grp)]
        for c, o2 in zip(grp, outs):
            fast_store(c, o2)

    for sq in range(nc // ncs):
        chunks = range(sq * ncs, (sq + 1) * ncs)
        if has_state:
            stf, stb = _pair_state_in(s0_ref.at[sq], 0), _pair_state_in(s0_ref.at[sq], 1)
        else:
            stf = jnp.zeros((LANES, LANES), F32)
            stb = stf
        for c in chunks:
            stcat_ref[c, :, 0:LANES] = stf.astype(BF16)
            stf = stf * dec_ref[c, 0:1, :] + dstf_ref[c]
        for c in reversed(chunks):
            stcat_ref[c, :, LANES:] = stb.astype(BF16)
            stb = stb * dec_ref[c, 1:2, :] + dstb_ref[c]
        if want_state:
            _pair_state_out(stf, st_view(sq), 0)
            _pair_state_out(stb, st_view(sq), 1)

    def finish():
        for g0 in range(0, nc, HG_GROUP):
            grp = range(g0, g0 + HG_GROUP)
            rows = [slice(c * C, (c + 1) * C) for c in grp]
            inter = [_dot_nt(xi_ref[r, :], stcat_ref[c]) for c, r in zip(grp, rows)]
            outs = [oacc[r, :] + x for r, x in zip(rows, inter)]
            sums = [_head_sums_sq(o, bd) for o in outs]
            for r, o, ms in zip(rows, outs, sums):
                y = o * lax.rsqrt(ms * (1.0 / HEAD_DIM) + LN_EPS)
                o_ref[r, :] = (y * _silu(hg_ref[r, :].astype(F32))).astype(BF16)

    finish()

    @pl.when(jnp.logical_not(jnp.max(gmax) < HG_SAFE_LOG2))
    def _():
        lax.fori_loop(0, nc, exact_body, 0)
        finish()


def _hgrn(proj, gates, gate_params, seq, layer, s0, want_state, carried=None):
    rows = proj.shape[0]
    L = MIX_ROWS
    nc = L // HG_CHUNK
    in_specs, state_spec, out_specs, out_shape = _mixer_specs(
        rows, (COL_HQ, GCOL_FF, GCOL_FB, COL_HI, COL_HG), seq, layer, s0, want_state, carried)
    in_specs = [pl.BlockSpec((None, None, SUBLANES, LANES), lambda p, b: (layer, p, 0, 0))] + in_specs
    args = [gate_params, proj, gates, gates, proj, proj]
    if s0 is not None:
        in_specs.append(state_spec)
        args.append(s0)
    aliases = _carry_states(in_specs, args, carried)
    kern = functools.partial(_hgrn_kernel, L=L, seq=seq, layer=layer, has_state=s0 is not None,
                             want_state=want_state, carried=carried is not None)
    return pl.pallas_call(
        kern,
        grid=(N_PAIRS, rows // L),
        in_specs=in_specs,
        out_specs=out_specs,
        out_shape=out_shape,
        input_output_aliases=aliases,
        scratch_shapes=[pltpu.VMEM((L, LANES), F32),
                        pltpu.VMEM((L, 2 * LANES), BF16),
                        pltpu.VMEM((nc, LANES, LANES), F32),
                        pltpu.VMEM((nc, LANES, LANES), F32),
                        pltpu.VMEM((nc, SUBLANES, LANES), F32),
                        pltpu.VMEM((nc, LANES, 2 * LANES), BF16),
                        pltpu.VMEM((2 * HG_CHUNK, HG_CHUNK), jnp.int32)]
                       + [pltpu.VMEM((L, LANES), F32)] * 5
                       + [pltpu.VMEM((L, LANES), BF16)] * 2,
        compiler_params=pltpu.CompilerParams(
            dimension_semantics=("arbitrary", "arbitrary"), vmem_limit_bytes=VMEM_LIMIT),
        name="hgrn2",
    )(*args)


def _ffn_kernel(x_ref, of_ref, or_ref, oh_ref, mod_ref, wo_ref, lng_ref, lnb_ref,
                wg_ref, wu_ref, wd_ref, *rest, n_side, mod_row, next_mod_layer):
    if next_mod_layer is not None:
        _mod_block(*rest[:3], rest[-1], next_mod_layer)
        rest = rest[3:-1]
    side_in, y_ref, side_out = rest[:n_side], rest[n_side], rest[n_side + 1:]
    g1, sh2, sc2, g2 = _mod_vectors(mod_ref, mod_row)[2:6]
    half = ROW_TILE // 2
    halves = (slice(0, half), slice(half, ROW_TILE))
    mix = [_dot(jnp.concatenate([of_ref[r, :], or_ref[r, :], oh_ref[r, :]], axis=1), wo_ref[...])
           for r in halves]
    x1, gate, up, ffn = [], [], [], []
    for r, m in zip(halves, mix):
        x1.append(_ln(ALPHA * x_ref[r, :] + g1 * m) * lng_ref[0:1, :] + lnb_ref[0:1, :])
        h2 = (_ln(x1[-1]) * (1.0 + sc2) + sh2).astype(BF16)
        gate.append(_dot(h2, wg_ref[...]))
        up.append(_dot(h2, wu_ref[...]))
    for g, u in zip(gate, up):
        act = (_silu(g) * u).astype(BF16)
        ffn.append(_dot(act, wd_ref[...]))
    for r, x, f in zip(halves, x1, ffn):
        y_ref[r, :] = _ln(ALPHA * x + g2 * f) * lng_ref[1:2, :] + lnb_ref[1:2, :]
    _side_casts(side_in + side_out)


def _out_ffn(x2d, o_f, o_r, o_h, mod, w_out_bf, ln_g, ln_b, wg_bf, wu_bf, wd_bf, layer, mod_row, side=(),
             next_mod=None):
    rows = x2d.shape[0]
    nsteps = rows // ROW_TILE
    s_in, s_out, s_shape, s_args = _side_cast_specs(side, nsteps)
    if next_mod is not None:
        width = N_MOD * D_MODEL
        m_in, m_out = _mod_specs(layer + 1, width // nsteps, lambda i: i)
        s_in, s_args = m_in + s_in, list(next_mod) + s_args
        s_out = s_out + [m_out]
        s_shape = s_shape + [jax.ShapeDtypeStruct((MOD_ROWS, width), F32)]
    once = pl.Buffered(1)
    row_blk = lambda w: pl.BlockSpec((ROW_TILE, w), lambda i: (i, 0))
    return pl.pallas_call(
        functools.partial(_ffn_kernel, n_side=len(side), mod_row=mod_row,
                          next_mod_layer=None if next_mod is None else layer + 1),
        grid=(nsteps,),
        in_specs=[row_blk(D_MODEL), row_blk(FNET_WIDTH), row_blk(HEADS_WIDTH), row_blk(HEADS_WIDTH),
                  pl.BlockSpec((MOD_ROWS, N_MOD * D_MODEL), lambda i: (0, 0)),
                  pl.BlockSpec((D_MODEL, D_MODEL), lambda i: (0, 0), pipeline_mode=once),
                  pl.BlockSpec((None, 2, D_MODEL), lambda i: (layer, 0, 0)),
                  pl.BlockSpec((None, 2, D_MODEL), lambda i: (layer, 0, 0)),
                  pl.BlockSpec((D_MODEL, D_FF), lambda i: (0, 0), pipeline_mode=once),
                  pl.BlockSpec((D_MODEL, D_FF), lambda i: (0, 0), pipeline_mode=once),
                  pl.BlockSpec((D_FF, D_MODEL), lambda i: (0, 0), pipeline_mode=once)] + s_in,
        out_specs=[row_blk(D_MODEL)] + s_out,
        out_shape=[jax.ShapeDtypeStruct((rows, D_MODEL), F32)] + s_shape,
        compiler_params=pltpu.CompilerParams(
            dimension_semantics=("arbitrary",), vmem_limit_bytes=VMEM_LIMIT),
        name="out_ffn",
    )(x2d, o_f, o_r, o_h, mod, w_out_bf, ln_g, ln_b, wg_bf, wu_bf, wd_bf, *s_args)


def kernel(x_prompt, x_sample, c, state_ret, state_hgrn, c_ctx, w_mod, b_mod, w_in, w_out,
           ret_log_decay, hg_lower_bound, ln_g, ln_b, w_gate, w_up, w_down):
    B, S, _ = x_prompt.shape
    DB, DS, _ = x_sample.shape

    p = jax.nn.softmax(hg_lower_bound.astype(F32), axis=1)
    cum = jnp.cumsum(p, axis=1)
    lbs = cum - cum[:, :1]
    kinds = jnp.stack([jnp.log(lbs), jnp.log1p(-lbs), 1.0 - lbs], axis=0)
    gp = jnp.transpose(kinds.reshape(3, 2, DEPTH, N_PAIRS, LANES), (2, 3, 0, 1, 4))
    gp = gp.reshape(DEPTH, N_PAIRS, 6, LANES)
    gp = jnp.concatenate([gp, jnp.zeros((DEPTH, N_PAIRS, SUBLANES - 6, LANES), F32)], axis=2)
    lg_flat = (-jnp.exp(ret_log_decay.astype(F32))).reshape(-1)

    cv =jnp.concatenate([c_ctx[None, :], c, jnp.zeros((MOD_ROWS - 1 - DB, D_MODEL), F32)], axis=0)
    mod = _modulation(cv, w_mod, b_mod, 0)

    rope = _rope_tables(DS)

    ctx_row = lambda i: 0
    smp_row = lambda i: 1 + i // (DS // ROW_TILE)

    y = x_prompt.reshape(B * S, D_MODEL)
    z = x_sample.reshape(DB * DS, D_MODEL)
    new_state_ret, new_state_hgrn = None, None
    w_in_bf = w_in[0].astype(BF16)
    w_out_bf = wg_bf = wu_bf = wd_bf = None
    for l in range(DEPTH):
        side_c = [(w_out, l), (w_gate, l)] if l == 0 else []
        side_s = [(w_up, l), (w_down, l)] if l == 0 else []
        proj_c, gates_c, *cast_c = _inproj(y, mod, w_in_bf, l, ctx_row, side_c)
        proj_s, gates_s, *cast_s = _inproj(z, mod, w_in_bf, l, smp_row, side_s)
        if l == 0:
            (w_out_bf, wg_bf), (wu_bf, wd_bf) = cast_c, cast_s
        of_c = _fnet(proj_c, S)
        or_c, new_state_ret = _retention(proj_c, lg_flat, S, l, None, None, True, new_state_ret)
        oh_c, new_state_hgrn = _hgrn(proj_c, gates_c, gp, S, l, None, True, new_state_hgrn)
        of_s = _fnet(proj_s, DS)
        (or_s,) = _retention(proj_s, lg_flat, DS, l, rope, state_ret, False)
        (oh_s,) = _hgrn(proj_s, gates_s, gp, DS, l, state_hgrn, False)
        last = l + 1 == DEPTH
        side_c = [] if last else [(w_in, l + 1), (w_out, l + 1)]
        side_s = [] if last else [(w_gate, l + 1), (w_up, l + 1), (w_down, l + 1)]
        weights = (w_out_bf, ln_g, ln_b, wg_bf, wu_bf, wd_bf)
        next_mod = None if last else (cv, w_mod, b_mod)
        y, *cast_c = _out_ffn(y, of_c, or_c, oh_c, mod, *weights, l, ctx_row, side_c, next_mod)
        z, *cast_s = _out_ffn(z, of_s, or_s, oh_s, mod, *weights, l, smp_row, side_s)
        if not last:
            (w_in_bf, w_out_bf, mod), (wg_bf, wu_bf, wd_bf) = cast_c, cast_s

    return (y.reshape(B, S, D_MODEL), z.reshape(DB, DS, D_MODEL), new_state_ret, new_state_hgrn)
```
